```python
import jax
import jax.numpy as jnp
from jax import lax
import numpy as np

D_MODEL = 2048
BATCH = 8
SEQ = 2048
DEPTH = 1
DEC_BATCH = 16
DEC_SEQ = 32
PAST_LEN = 1024

CHUNK = 64
D_MIX = D_MODEL
LRU_WIDTH = D_MIX // 2
LRU_BLOCKS = 8
LRU_BLOCK = LRU_WIDTH // LRU_BLOCKS
CONV_WIDTH = 4
LRU_C = 8.0
RET_WIDTH = D_MIX - LRU_WIDTH
RET_HEADS = 8
RET_HEAD_DIM = RET_WIDTH // RET_HEADS
ROPE_BASE = 10000.0
IN_COLS = 2 * LRU_WIDTH + 4 * RET_WIDTH
N_EXPERTS = 64
TOP_K = 8
N_GROUPS = 8
TOPK_GROUPS = 4
EXPERT_HIDDEN = D_MODEL // 4
SHARED_HIDDEN = EXPERT_HIDDEN
ROUTED_SCALE = 2.5
PLE_DIM = 256
LN_EPS = 1e-5
GN_EPS = 1e-6
ALPHA = (2 * DEPTH) ** 0.25
BETA = (8 * DEPTH) ** -0.25

kernel_name = "hybrid_rglru_retention_moe_stream_step"

F32 = jnp.float32


def _layer_norm(x, g, b):
    xf = x.astype(F32)
    mu = jnp.mean(xf, -1, keepdims=True)
    var = jnp.mean(jnp.square(xf - mu), -1, keepdims=True)
    return ((xf - mu) * lax.rsqrt(var + LN_EPS) * g.astype(F32) + b.astype(F32)).astype(x.dtype)


def _lin_combine(left, right):
    a1, b1 = left
    a2, b2 = right
    return a1 * a2, a2 * b1 + b2


def _rglru(xc, h0, wa, ba, wx, bx, lam):
    B, T, W = xc.shape
    xb = xc.reshape(B, T, LRU_BLOCKS, LRU_BLOCK)
    r = jax.nn.sigmoid((jnp.einsum('btnk,nkj->btnj', xb, wa).reshape(B, T, W) + ba).astype(F32))
    i = jax.nn.sigmoid((jnp.einsum('btnk,nkj->btnj', xb, wx).reshape(B, T, W) + bx).astype(F32))
    log_a = -LRU_C * r * jax.nn.softplus(-lam.astype(F32))
    a = jnp.exp(log_a)
    b = jnp.sqrt(-jnp.expm1(2.0 * log_a)) * (i * xc.astype(F32))
    b = b.at[:, 0].add(a[:, 0] * h0.astype(F32))
    _, h = lax.associative_scan(_lin_combine, (a, b), axis=1)
    return h, h[:, -1]


def _rope(t, pos):
    half = t.shape[-1] // 2
    inv = ROPE_BASE ** (-jnp.arange(half, dtype=F32) / half)
    ang = pos.astype(F32)[:, None] * inv[None, :]
    cos = jnp.cos(ang)[None, :, None, :]
    sin = jnp.sin(ang)[None, :, None, :]
    t1 = t[..., :half].astype(F32)
    t2 = t[..., half:].astype(F32)
    return jnp.concatenate([t1 * cos - t2 * sin, t1 * sin + t2 * cos], axis=-1)


def _retention(q, k, v, s0, chunk):
    B, T, H, dk = q.shape
    dv = v.shape[-1]
    n = T // chunk
    log_g = jnp.log1p(-jnp.exp2(-5.0 - jnp.arange(H, dtype=F32)))
    idx = jnp.arange(chunk, dtype=F32)
    d_intra = jnp.exp(jnp.abs(idx[:, None] - idx[None, :])[None] * log_g[:, None, None])
    k_dec = jnp.exp((chunk - 1.0 - idx)[None, :] * log_g[:, None])
    q_dec = jnp.exp((idx + 1.0)[None, :] * log_g[:, None])
    chunk_dec = jnp.exp(chunk * log_g)[:, None, None]
    qc = q.reshape(B, n, chunk, H, dk)
    kc = k.reshape(B, n, chunk, H, dk)
    vc = v.astype(F32).reshape(B, n, chunk, H, dv)
    scores = jnp.einsum('bnihd,bnjhd->bnhij', qc, kc) * d_intra
    o_intra = jnp.einsum('bnhij,bnjhe->bnihe', scores, vc)
    kv = jnp.einsum('bnjhd,bnjhe,hj->nbhde', kc, vc, k_dec)

    def step(s, kv_c):
        return chunk_dec * s + kv_c, s

    s_last, s_start = lax.scan(step, s0.astype(F32), kv)
    o_inter = jnp.einsum('bnihd,nbhde,hi->bnihe', qc, s_start, q_dec)
    return (o_intra + o_inter).reshape(B, T, H, dv), s_last


def _moe(x2d, router_w, router_b, e_gate, e_up, e_down, s_gate, s_up, s_down):
    n_tok = x2d.shape[0]
    s = jax.nn.sigmoid(x2d.astype(F32) @ router_w.astype(F32))
    sb = s + router_b.astype(F32)
    grp = sb.reshape(n_tok, N_GROUPS, N_EXPERTS // N_GROUPS)
    grp_score = jnp.sum(lax.top_k(grp, 2)[0], axis=-1)
    _, gidx = lax.top_k(grp_score, TOPK_GROUPS)
    gsel = jnp.any(gidx[..., None] == jnp.arange(N_GROUPS), axis=-2)
    emask = jnp.repeat(gsel, N_EXPERTS // N_GROUPS, axis=-1)
    _, eidx = lax.top_k(jnp.where(emask, sb, -jnp.inf), TOP_K)
    wts = jnp.take_along_axis(s, eidx, axis=-1)
    wts = wts / jnp.sum(wts, -1, keepdims=True) * ROUTED_SCALE
    flat_e = eidx.reshape(-1)
    order = jnp.argsort(flat_e)
    tok = order // TOP_K
    sizes = jnp.bincount(flat_e, length=N_EXPERTS).astype(jnp.int32)
    xs = x2d[tok]
    hid = jax.nn.silu(lax.ragged_dot(xs, e_gate, sizes)) * lax.ragged_dot(xs, e_up, sizes)
    ys = lax.ragged_dot(hid, e_down, sizes)
    routed = jax.ops.segment_sum(ys.astype(F32) * wts.reshape(-1)[order][:, None], tok, num_segments=n_tok)
    shared = (jax.nn.silu(x2d @ s_gate) * (x2d @ s_up)) @ s_down
    return (routed + shared.astype(F32)).astype(x2d.dtype)


def _layer(x, p, conv_st, lru_st, ret_st, pos0, chunk, params):
    (w_in, conv_w, conv_b, lru_wa, lru_ba, lru_wx, lru_bx, lru_lambda, ret_gn, w_out,
     ln1_g, ln1_b, router_w, router_b, e_gate, e_up, e_down, s_gate, s_up, s_down,
     ln2_g, ln2_b, ple_w_proj, ple_w_gate, ple_b_gate) = params
    B, T, _ = x.shape
    z = x @ w_in
    cuts = [LRU_WIDTH, 2 * LRU_WIDTH, 2 * LRU_WIDTH + RET_WIDTH,
            2 * LRU_WIDTH + 2 * RET_WIDTH, 2 * LRU_WIDTH + 3 * RET_WIDTH]
    xl, gl, q, k, v, g = jnp.split(z, cuts, axis=-1)

    xcat = jnp.concatenate([conv_st.astype(x.dtype), xl], axis=1)
    xc = conv_b
    for j in range(CONV_WIDTH):
        xc = xc + xcat[:, j:j + T] * conv_w[j]
    new_conv = xcat[:, T:]
    h, new_lru = _rglru(xc, lru_st, lru_wa, lru_ba, lru_wx, lru_bx, lru_lambda)
    y_lru = (h * jax.nn.gelu(gl.astype(F32))).astype(x.dtype)

    pos = pos0 + jnp.arange(T)
    qh = _rope(q.reshape(B, T, RET_HEADS, RET_HEAD_DIM), pos)
    kh = _rope(k.reshape(B, T, RET_HEADS, RET_HEAD_DIM), pos) * (RET_HEAD_DIM ** -0.5)
    vh = v.reshape(B, T, RET_HEADS, RET_HEAD_DIM)
    o, new_ret = _retention(qh, kh, vh, ret_st, chunk)
    mu = jnp.mean(o, -1, keepdims=True)
    var = jnp.mean(jnp.square(o - mu), -1, keepdims=True)
    o = ((o - mu) * lax.rsqrt(var + GN_EPS)).reshape(B, T, RET_WIDTH) * ret_gn.astype(F32)
    y_ret = (jax.nn.silu(g.astype(F32)) * o).astype(x.dtype)

    mix = jnp.concatenate([y_lru, y_ret], axis=-1) @ w_out
    x1 = _layer_norm(ALPHA * x + mix, ln1_g, ln1_b)
    ffn = _moe(x1.reshape(B * T, D_MODEL), router_w, router_b, e_gate, e_up, e_down,
               s_gate, s_up, s_down).reshape(B, T, D_MODEL)
    x2 = _layer_norm(ALPHA * x1 + ffn, ln2_g, ln2_b)
    out = x2 + jax.nn.sigmoid(x2 @ ple_w_gate + ple_b_gate) * (p @ ple_w_proj)
    return out.astype(x.dtype), new_conv, new_lru, new_ret


def setup_inputs(seed: int = 0) -> dict:
    key = jax.random.key(seed)
    ks = jax.random.split(key, 40)

    def nrm(k, shape, scale):
        return jax.random.normal(k, shape, F32) * scale

    col_scale = jnp.concatenate([
        jnp.full((LRU_WIDTH,), BETA, F32),
        jnp.ones((LRU_WIDTH + 2 * RET_WIDTH,), F32),
        jnp.full((RET_WIDTH,), BETA, F32),
        jnp.ones((RET_WIDTH,), F32)])
    u = jax.random.uniform(ks[10], (DEPTH, LRU_WIDTH), F32, 0.9, 0.999)
    a = u ** (1.0 / LRU_C)
    lru_lambda = jnp.log(a) - jnp.log1p(-a)
    return {
        'x_prompt': nrm(ks[0], (BATCH, SEQ, D_MODEL), 1.0),
        'x_sample': nrm(ks[1], (DEC_BATCH, DEC_SEQ, D_MODEL), 1.0),
        'p_prompt': nrm(ks[2], (DEPTH, BATCH, SEQ, PLE_DIM), 1.0),
        'p_sample': nrm(ks[3], (DEPTH, DEC_BATCH, DEC_SEQ, PLE_DIM), 1.0),
        'state_conv': nrm(ks[4], (DEPTH, DEC_BATCH, CONV_WIDTH - 1, LRU_WIDTH), 0.5),
        'state_lru': nrm(ks[5], (DEPTH, DEC_BATCH, LRU_WIDTH), 0.5),
        'state_ret': nrm(ks[6], (DEPTH, DEC_BATCH, RET_HEADS, RET_HEAD_DIM, RET_HEAD_DIM), 0.3),
        'w_in': nrm(ks[7], (DEPTH, D_MODEL, IN_COLS), D_MODEL ** -0.5) * col_scale,
        'conv_w': nrm(ks[8], (DEPTH, CONV_WIDTH, LRU_WIDTH), CONV_WIDTH ** -0.5),
        'conv_b': nrm(ks[9], (DEPTH, LRU_WIDTH), 0.02),
        'lru_wa': nrm(ks[11], (DEPTH, LRU_BLOCKS, LRU_BLOCK, LRU_BLOCK), LRU_BLOCK ** -0.5),
        'lru_ba': nrm(ks[12], (DEPTH, LRU_WIDTH), 0.02),
        'lru_wx': nrm(ks[13], (DEPTH, LRU_BLOCKS, LRU_BLOCK, LRU_BLOCK), LRU_BLOCK ** -0.5),
        'lru_bx': nrm(ks[14], (DEPTH, LRU_WIDTH), 0.02),
        'lru_lambda': lru_lambda,
        'ret_gn': 1.0 + nrm(ks[15], (DEPTH, RET_WIDTH), 0.02),
        'w_out': nrm(ks[16], (DEPTH, D_MIX, D_MODEL), D_MIX ** -0.5 * BETA),
        'ln1_g': 1.0 + nrm(ks[17], (DEPTH, D_MODEL), 0.02),
        'ln1_b': nrm(ks[18], (DEPTH, D_MODEL), 0.02),
        'router_w': nrm(ks[19], (DEPTH, D_MODEL, N_EXPERTS), D_MODEL ** -0.5),
        'router_b': nrm(ks[20], (DEPTH, N_EXPERTS), 0.01),
        'exp_w_gate': nrm(ks[21], (DEPTH, N_EXPERTS, D_MODEL, EXPERT_HIDDEN), D_MODEL ** -0.5),
        'exp_w_up': nrm(ks[22], (DEPTH, N_EXPERTS, D_MODEL, EXPERT_HIDDEN), D_MODEL ** -0.5),
        'exp_w_down': nrm(ks[23], (DEPTH, N_EXPERTS, EXPERT_HIDDEN, D_MODEL), EXPERT_HIDDEN ** -0.5 * BETA),
        'sh_w_gate': nrm(ks[24], (DEPTH, D_MODEL, SHARED_HIDDEN), D_MODEL ** -0.5),
        'sh_w_up': nrm(ks[25], (DEPTH, D_MODEL, SHARED_HIDDEN), D_MODEL ** -0.5),
        'sh_w_down': nrm(ks[26], (DEPTH, SHARED_HIDDEN, D_MODEL), SHARED_HIDDEN ** -0.5 * BETA),
        'ln2_g': 1.0 + nrm(ks[27], (DEPTH, D_MODEL), 0.02),
        'ln2_b': nrm(ks[28], (DEPTH, D_MODEL), 0.02),
        'ple_w_proj': nrm(ks[29], (DEPTH, PLE_DIM, D_MODEL), PLE_DIM ** -0.5),
        'ple_w_gate': nrm(ks[30], (DEPTH, D_MODEL, D_MODEL), D_MODEL ** -0.5),
        'ple_b_gate': nrm(ks[31], (DEPTH, D_MODEL), 0.02),
    }


def reference(x_prompt, x_sample, p_prompt, p_sample, state_conv, state_lru, state_ret,
              w_in, conv_w, conv_b, lru_wa, lru_ba, lru_wx, lru_bx, lru_lambda, ret_gn, w_out,
              ln1_g, ln1_b, router_w, router_b, exp_w_gate, exp_w_up, exp_w_down,
              sh_w_gate, sh_w_up, sh_w_down, ln2_g, ln2_b, ple_w_proj, ple_w_gate, ple_b_gate):
    b_p = x_prompt.shape[0]
    t_s = x_sample.shape[1]
    hp = x_prompt
    hs = x_sample
    conv_p, lru_p, ret_p, conv_s, lru_s, ret_s = [], [], [], [], [], []
    for i in range(DEPTH):
        params = (w_in[i], conv_w[i], conv_b[i], lru_wa[i], lru_ba[i], lru_wx[i], lru_bx[i],
                  lru_lambda[i], ret_gn[i], w_out[i], ln1_g[i], ln1_b[i], router_w[i], router_b[i],
                  exp_w_gate[i], exp_w_up[i], exp_w_down[i], sh_w_gate[i], sh_w_up[i], sh_w_down[i],
                  ln2_g[i], ln2_b[i], ple_w_proj[i], ple_w_gate[i], ple_b_gate[i])
        zc = jnp.zeros((b_p, CONV_WIDTH - 1, LRU_WIDTH), x_prompt.dtype)
        zl = jnp.zeros((b_p, LRU_WIDTH), F32)
        zr = jnp.zeros((b_p, RET_HEADS, RET_HEAD_DIM, RET_HEAD_DIM), F32)
        hp, c, l, r = _layer(hp, p_prompt[i], zc, zl, zr, 0, CHUNK, params)
        conv_p.append(c)
        lru_p.append(l)
        ret_p.append(r)
        hs, c, l, r = _layer(hs, p_sample[i], state_conv[i], state_lru[i], state_ret[i],
                             PAST_LEN, t_s, params)
        conv_s.append(c)
        lru_s.append(l)
        ret_s.append(r)
    return (hp, hs, jnp.stack(conv_p), jnp.stack(lru_p), jnp.stack(ret_p),
            jnp.stack(conv_s), jnp.stack(lru_s), jnp.stack(ret_s))
```

```python
import functools

import jax
import jax.numpy as jnp
from jax import lax
from jax.experimental import pallas as pl
from jax.experimental.pallas import tpu as pltpu

F32 = jnp.float32
BF16 = jnp.bfloat16
U32 = jnp.uint32
I32 = jnp.int32

CHUNK = 64
PAST_LEN = 1024
CONV_WIDTH = 4
LRU_C = 8.0
LRU_BLOCKS = 8
RET_HEADS = 8
ROPE_BASE = 10000.0
N_EXPERTS = 64
TOP_K = 8
N_GROUPS = 8
TOPK_GROUPS = 4
ROUTED_SCALE = 2.5
LN_EPS = 1e-5
GN_EPS = 1e-6

LANES = 128
ROWS_PER_STEP = 512
MIX_ROWS = 256
GMM_ROWS = 256
RET_GROUP_ROWS = 256
LRU_TIME_TILE = 64
VMEM_LIMIT = 56 * 1024 * 1024


def _const_spec(shape):
    zeros = (0,) * len(shape)
    return pl.BlockSpec(shape, lambda *_: zeros, pipeline_mode=pl.Buffered(1))


def _params(n_axes):
    return pltpu.CompilerParams(dimension_semantics=("arbitrary",) * n_axes,
                                vmem_limit_bytes=VMEM_LIMIT)


def _layer_norm(x, g, b):
    mu = jnp.mean(x, axis=-1, keepdims=True)
    xc = x - mu
    var = jnp.mean(xc * xc, axis=-1, keepdims=True)
    return xc * lax.rsqrt(var + LN_EPS) * g + b


def _pack_bf16_pair(x):
    c = x.shape[1] // 2
    xb = x.astype(BF16).astype(F32)
    hi = pltpu.bitcast(xb[:, :c], U32)
    lo = pltpu.bitcast(xb[:, c:], U32)
    return hi | (lo >> 16)


def _unpack_bf16_pair(pk):
    hi = pltpu.bitcast(pk & jnp.uint32(0xFFFF0000), F32)
    lo = pltpu.bitcast(pk << 16, F32)
    return hi, lo


def _inproj_body(x_ref, w_ref, zl_ref, zr_ref, *, bb, tm, lru_cols, tn):
    d = x_ref.shape[-1]
    x = x_ref[...].reshape(bb * tm, d).astype(BF16)
    for j in range(w_ref.shape[1] // tn):
        c0 = j * tn
        acc = jnp.dot(x, w_ref[:, c0:c0 + tn], preferred_element_type=F32)
        for b in range(bb):
            rows = acc[b * tm:(b + 1) * tm]
            if c0 < lru_cols:
                zl_ref[:, b * lru_cols + c0:b * lru_cols + c0 + tn] = rows
            else:
                zr_ref[b, :, c0 - lru_cols:c0 - lru_cols + tn] = rows.astype(BF16)


def _inproj(x, w_in_b, lru_cols):
    B, T, D = x.shape
    n_cols = w_in_b.shape[1]
    ret_cols = n_cols - lru_cols
    tm = min(ROWS_PER_STEP, T)
    bb = min(B, ROWS_PER_STEP // tm)
    body = functools.partial(_inproj_body, bb=bb, tm=tm, lru_cols=lru_cols, tn=512)
    return pl.pallas_call(
        body,
        grid=(B // bb, T // tm),
        in_specs=[pl.BlockSpec((bb, tm, D), lambda b, t: (b, t, 0)),
                  _const_spec((D, n_cols))],
        out_specs=[pl.BlockSpec((tm, bb * lru_cols), lambda b, t: (t, b)),
                   pl.BlockSpec((bb, tm, ret_cols), lambda b, t: (b, t, 0))],
        out_shape=[jax.ShapeDtypeStruct((T, B * lru_cols), F32),
                   jax.ShapeDtypeStruct((B, T, ret_cols), BF16)],
        compiler_params=_params(2),
        name="inproj",
    )(x, w_in_b)


def _lru_body(xl_ref, gl_ref, conv0_ref, h0_ref, cw_ref, cb_ref, wa_ref, ba_ref, wx_ref, bx_ref, lam_ref,
              y_ref, conv_out_ref, h_out_ref, xp_s, a_s, b_s, h_s, *, tt, rows):
    i = pl.program_id(0)
    B, W = h0_ref.shape
    nblk = wa_ref.shape[0]
    blk = W // nblk

    @pl.when(i == 0)
    def _():
        xp_s[0:CONV_WIDTH - 1] = conv0_ref[...]
        h_s[...] = h0_ref[...]

    xp_s[CONV_WIDTH - 1:] = xl_ref[...]

    lam = lam_ref[...]
    neg = -lam
    softplus = jnp.maximum(neg, 0.0) + jnp.log1p(jnp.exp(-jnp.abs(neg)))
    decay = (-LRU_C) * softplus

    def gates(c, carry):
        t0 = pl.multiple_of(c * rows, rows)
        xc = cb_ref[...].reshape(1, 1, W)
        for j in range(CONV_WIDTH):
            xc = xc + xp_s[pl.ds(t0 + j, rows)] * cw_ref[j:j + 1].reshape(1, 1, W)
        xc2 = xc.reshape(rows * B, W)
        xcb = xc2.astype(BF16)
        r_parts, i_parts = [], []
        for n in range(nblk):
            xb = xcb[:, n * blk:(n + 1) * blk]
            r_parts.append(jnp.dot(xb, wa_ref[n], preferred_element_type=F32))
            i_parts.append(jnp.dot(xb, wx_ref[n], preferred_element_type=F32))
        r = jax.nn.sigmoid(jnp.concatenate(r_parts, axis=1) + ba_ref[...])
        ig = jax.nn.sigmoid(jnp.concatenate(i_parts, axis=1) + bx_ref[...])
        a = jnp.exp(decay * r)
        bterm = jnp.sqrt(1.0 - a * a) * (ig * xc2)
        a_s[pl.ds(t0, rows)] = a.reshape(rows, B, W)
        b_s[pl.ds(t0, rows)] = bterm.reshape(rows, B, W)
        return carry

    lax.fori_loop(0, tt // rows, gates, 0)

    def step(t, h):
        hn = a_s[t] * h + b_s[t]
        y_ref[t] = hn * jax.nn.gelu(gl_ref[t])
        return hn

    h_last = lax.fori_loop(0, tt, step, h_s[...], unroll=8)
    h_s[...] = h_last
    tail = xp_s[tt:tt + CONV_WIDTH - 1]
    xp_s[0:CONV_WIDTH - 1] = tail
    conv_out_ref[...] = tail
    h_out_ref[...] = h_last


def _lru(z_lru3, conv0_tm, h0, conv_w, conv_b, wa_b, ba, wx_b, bx, lam):
    T, B, W2 = z_lru3.shape
    W = W2 // 2
    tt = min(LRU_TIME_TILE, T)
    rows = max(1, min(tt, 128 // B))
    body = functools.partial(_lru_body, tt=tt, rows=rows)
    nb = wa_b.shape[0]
    blk = W // nb
    return pl.pallas_call(
        body,
        grid=(T // tt,),
        in_specs=[pl.BlockSpec((tt, B, W), lambda t: (t, 0, 0)),
                  pl.BlockSpec((tt, B, W), lambda t: (t, 0, 1)),
                  _const_spec((CONV_WIDTH - 1, B, W)),
                  _const_spec((B, W)),
                  _const_spec((CONV_WIDTH, W)),
                  _const_spec((1, W)),
                  _const_spec((nb, blk, blk)),
                  _const_spec((1, W)),
                  _const_spec((nb, blk, blk)),
                  _const_spec((1, W)),
                  _const_spec((1, W))],
        out_specs=[pl.BlockSpec((tt, B, W), lambda t: (t, 0, 0)),
                   pl.BlockSpec((CONV_WIDTH - 1, B, W), lambda t: (0, 0, 0)),
                   pl.BlockSpec((B, W), lambda t: (0, 0))],
        out_shape=[jax.ShapeDtypeStruct((T, B, W), F32),
                   jax.ShapeDtypeStruct((CONV_WIDTH - 1, B, W), F32),
                   jax.ShapeDtypeStruct((B, W), F32)],
        scratch_shapes=[pltpu.VMEM((tt + CONV_WIDTH - 1, B, W), F32),
                        pltpu.VMEM((tt, B, W), F32),
                        pltpu.VMEM((tt, B, W), F32),
                        pltpu.VMEM((B, W), F32)],
        compiler_params=_params(1),
        name="rglru",
    )(z_lru3, z_lru3, conv0_tm, h0, conv_w, conv_b, wa_b, ba, wx_b, bx, lam)


def _ret_body(q_ref, k_ref, v_ref, g_ref, cos_ref, sin_ref, mask_ref, qdec_ref, kdec_ref, cdec_ref, gn_ref,
              s0_ref, y_ref, s_out_ref, *, rg):
    T = q_ref.shape[1]
    dh = q_ref.shape[2]
    scale = dh ** -0.5

    def rope(t, cos, sin):
        return t * cos + pltpu.roll(t, dh // 2, axis=1) * sin

    def group(c, s):
        r0 = pl.multiple_of(c * rg, rg)
        cos = cos_ref[pl.ds(r0, rg), :]
        sin = sin_ref[pl.ds(r0, rg), :]
        q = rope(q_ref[0, pl.ds(r0, rg), :].astype(F32), cos, sin)
        k = rope(k_ref[0, pl.ds(r0, rg), :].astype(F32), cos, sin) * scale
        v = v_ref[0, pl.ds(r0, rg), :]
        scores = lax.dot_general(q.astype(BF16), k.astype(BF16), (((1,), (1,)), ((), ())),
                                 preferred_element_type=F32)
        scores = scores * mask_ref[0]
        o = jnp.dot(scores.astype(BF16), v, preferred_element_type=F32)
        o = o + jnp.dot((q * qdec_ref[0]).astype(BF16), s.astype(BF16), preferred_element_type=F32)
        kd = (k * kdec_ref[0]).astype(BF16)
        kv = lax.dot_general(kd, v, (((0,), (0,)), ((), ())), preferred_element_type=F32)
        s_new = cdec_ref[0] * s + kv
        mu = jnp.mean(o, axis=-1, keepdims=True)
        oc = o - mu
        var = jnp.mean(oc * oc, axis=-1, keepdims=True)
        on = oc * lax.rsqrt(var + GN_EPS) * gn_ref[...]
        g = g_ref[0, pl.ds(r0, rg), :].astype(F32)
        y_ref[0, pl.ds(r0, rg), :] = (g * jax.nn.sigmoid(g) * on).astype(y_ref.dtype)
        return s_new

    s_last = lax.fori_loop(0, T // rg, group, s0_ref[0, 0])
    s_out_ref[0, 0] = s_last


def _retention_tables(T, pos0, chunk, rg, dh):
    half = dh // 2
    inv = ROPE_BASE ** (-jnp.arange(half, dtype=F32) / half)
    pos = pos0 + jnp.arange(T)
    ang = pos.astype(F32)[:, None] * inv[None, :]
    cos, sin = jnp.cos(ang), jnp.sin(ang)
    cos2 = jnp.concatenate([cos, cos], axis=1)
    sin2 = jnp.concatenate([-sin, sin], axis=1)
    log_g = jnp.log1p(-jnp.exp2(-5.0 - jnp.arange(RET_HEADS, dtype=F32)))[:, None, None]
    idx = jnp.arange(rg, dtype=F32)
    ci = jnp.floor(idx / chunk)
    diff = idx[:, None] - idx[None, :]
    same = ci[:, None] == ci[None, :]
    earlier = ci[None, :] < ci[:, None]
    dist = jnp.where(same, jnp.abs(diff), diff)
    mask = jnp.where(same | earlier, jnp.exp(dist[None] * log_g), 0.0)
    ones = jnp.ones((1, 1, dh), F32)
    qdec = jnp.exp((idx + 1.0)[None, :, None] * log_g) * ones
    kdec = jnp.exp((rg - 1.0 - idx)[None, :, None] * log_g) * ones
    cdec = jnp.exp(rg * log_g) * ones
    return cos2, sin2, mask, qdec, kdec, cdec


def _retention(z_ret, s0, gn, pos0, chunk):
    B, T, C4 = z_ret.shape
    H = RET_HEADS
    dh = C4 // (4 * H)
    rg = min(T, max(chunk, (RET_GROUP_ROWS // chunk) * chunk))
    cos2, sin2, mask, qdec, kdec, cdec = _retention_tables(T, pos0, chunk, rg, dh)
    body = functools.partial(_ret_body, rg=rg)
    col = lambda off: (lambda b, h: (b, 0, off * H + h))
    return pl.pallas_call(
        body,
        grid=(B, H),
        in_specs=[pl.BlockSpec((1, T, dh), col(0)),
                  pl.BlockSpec((1, T, dh), col(1)),
                  pl.BlockSpec((1, T, dh), col(2)),
                  pl.BlockSpec((1, T, dh), col(3)),
                  _const_spec((T, dh)),
                  _const_spec((T, dh)),
                  pl.BlockSpec((1, rg, rg), lambda b, h: (h, 0, 0)),
                  pl.BlockSpec((1, rg, dh), lambda b, h: (h, 0, 0)),
                  pl.BlockSpec((1, rg, dh), lambda b, h: (h, 0, 0)),
                  pl.BlockSpec((1, 1, dh), lambda b, h: (h, 0, 0)),
                  pl.BlockSpec((1, dh), lambda b, h: (0, h)),
                  pl.BlockSpec((1, 1, dh, dh), lambda b, h: (b, h, 0, 0))],
        out_specs=[pl.BlockSpec((1, T, dh), lambda b, h: (b, 0, h)),
                   pl.BlockSpec((1, 1, dh, dh), lambda b, h: (b, h, 0, 0))],
        out_shape=[jax.ShapeDtypeStruct((B, T, H * dh), BF16),
                   jax.ShapeDtypeStruct((B, H, dh, dh), F32)],
        compiler_params=_params(2),
        name="retention",
    )(z_ret, z_ret, z_ret, z_ret, cos2, sin2, mask, qdec, kdec, cdec, gn, s0)


def _seg_allreduce(v, lane, op):
    for s in (1, 2, 4):
        up = pltpu.roll(v, LANES - s, axis=1)
        dn = pltpu.roll(v, s, axis=1)
        v = op(v, jnp.where((lane & s) == 0, up, dn))
    return v


def _mix_body(yl_ref, yr_ref, x_ref, wo_ref, g1_ref, b1_ref, rwh_ref, rwl_ref, rb_ref, sg_ref, su_ref, sd_ref,
              tri_ref, base_ref, xpk_ref, eidx_ref, pos_ref, wts_ref, cnt_ref, carry_s, *, bb, tm, alpha):
    first = (pl.program_id(0) == 0) & (pl.program_id(1) == 0)
    m = bb * tm
    d = x_ref.shape[-1]
    w = yl_ref.shape[1] // bb

    @pl.when(first)
    def _():
        carry_s[...] = jnp.zeros_like(carry_s)

    yl = jnp.concatenate([yl_ref[:, b * w:(b + 1) * w] for b in range(bb)], axis=0).astype(BF16)
    yr = yr_ref[...].reshape(m, yr_ref.shape[-1])
    mix = jnp.dot(yl, wo_ref[:w], preferred_element_type=F32)
    mix = mix + jnp.dot(yr, wo_ref[w:], preferred_element_type=F32)
    x1 = _layer_norm(alpha * x_ref[...].reshape(m, d) + mix, g1_ref[...], b1_ref[...])
    x1b = x1.astype(BF16)

    x1l = (x1 - x1b.astype(F32)).astype(BF16)
    logits = jnp.dot(x1b, rwh_ref[...], preferred_element_type=F32)
    logits = logits + jnp.dot(x1l, rwh_ref[...], preferred_element_type=F32)
    logits = logits + jnp.dot(x1b, rwl_ref[...], preferred_element_type=F32)
    s = jax.nn.sigmoid(logits)
    sb = s + rb_ref[...]
    lane = lax.broadcasted_iota(I32, (m, LANES), 1)
    e_id = lane & (N_EXPERTS - 1)
    e_f = e_id.astype(F32)
    grp = e_id >> 3
    big = jnp.float32(1e9)
    ninf = jnp.float32(-jnp.inf)

    m1 = _seg_allreduce(sb, lane, jnp.maximum)
    first_max = _seg_allreduce(jnp.where(sb == m1, e_f, big), lane, jnp.minimum)
    m2 = _seg_allreduce(jnp.where(e_f == first_max, ninf, sb), lane, jnp.maximum)
    gs = m1 + m2
    rank = jnp.zeros((m, LANES), F32)
    for dgrp in range(1, N_GROUPS):
        other = pltpu.roll(gs, 8 * dgrp, axis=1)
        beats = (other > gs) | ((other == gs) & (grp >= dgrp))
        rank = rank + jnp.where(beats, 1.0, 0.0)
    v = jnp.where(rank < TOPK_GROUPS, sb, ninf)

    idx_cols, w_cols = [], []
    sel = jnp.zeros((m, LANES), F32)
    for _ in range(TOP_K):
        mx = jnp.max(v, axis=1, keepdims=True)
        idx = jnp.min(jnp.where(v == mx, e_f, big), axis=1, keepdims=True)
        hit = e_f == idx
        w_cols.append(jnp.sum(jnp.where(hit & (lane < N_EXPERTS), s, 0.0), axis=1, keepdims=True))
        idx_cols.append(idx)
        v = jnp.where(hit, ninf, v)
        sel = jnp.where(hit & (lane < N_EXPERTS), 1.0, sel)

    cum = jnp.dot(tri_ref[...], sel.astype(BF16), preferred_element_type=F32) + carry_s[0:1, :]
    carry_s[0:1, :] = carry_s[0:1, :] + jnp.sum(sel, axis=0, keepdims=True)
    cnt_ref[...] = carry_s[0:1, :].astype(I32)

    wsum = w_cols[0]
    for c in w_cols[1:]:
        wsum = wsum + c
    e_out = jnp.zeros((m, LANES), F32)
    p_out = jnp.zeros((m, LANES), F32)
    w_out = jnp.zeros((m, LANES), F32)
    for kk in range(TOP_K):
        hit = e_f == idx_cols[kk]
        pk = jnp.sum(jnp.where(hit & (lane < N_EXPERTS), cum, 0.0), axis=1, keepdims=True)
        e_out = jnp.where(lane == kk, idx_cols[kk], e_out)
        p_out = jnp.where(lane == kk, pk, p_out)
        w_out = jnp.where(lane == kk, w_cols[kk] / wsum * ROUTED_SCALE, w_out)
    eidx_ref[...] = e_out[:, :TOP_K].astype(I32)
    pos_ref[...] = p_out[:, :TOP_K].astype(I32)
    wts_ref[...] = w_out[:, :TOP_K]

    hg = jnp.dot(x1b, sg_ref[...], preferred_element_type=F32)
    hu = jnp.dot(x1b, su_ref[...], preferred_element_type=F32)
    hs = (hg * jax.nn.sigmoid(hg) * hu).astype(BF16)
    base_ref[...] = alpha * x1 + jnp.dot(hs, sd_ref[...], preferred_element_type=F32)
    xpk_ref[...] = _pack_bf16_pair(x1)


def _mix(y_lru2, y_ret, x, wo_b, g1, b1, rwh, rwl, rb2, sg_b, su_b, sd_b, alpha):
    B, T, D = x.shape
    W = y_ret.shape[-1]
    tm = min(MIX_ROWS, T)
    bb = min(B, MIX_ROWS // tm)
    m = bb * tm
    n = B * T
    hs = sg_b.shape[1]
    tri = (lax.broadcasted_iota(I32, (m, m), 1) < lax.broadcasted_iota(I32, (m, m), 0)).astype(BF16)
    body = functools.partial(_mix_body, bb=bb, tm=tm, alpha=alpha)
    nt = T // tm
    row = lambda b, t: (b * nt + t, 0)
    return pl.pallas_call(
        body,
        grid=(B // bb, nt),
        in_specs=[pl.BlockSpec((tm, bb * W), lambda b, t: (t, b)),
                  pl.BlockSpec((bb, tm, W), lambda b, t: (b, t, 0)),
                  pl.BlockSpec((bb, tm, D), lambda b, t: (b, t, 0)),
                  _const_spec((2 * W, D)),
                  _const_spec((1, D)),
                  _const_spec((1, D)),
                  _const_spec((D, LANES)),
                  _const_spec((D, LANES)),
                  _const_spec((1, LANES)),
                  _const_spec((D, hs)),
                  _const_spec((D, hs)),
                  _const_spec((hs, D)),
                  _const_spec((m, m))],
        out_specs=[pl.BlockSpec((m, D), row),
                   pl.BlockSpec((m, D // 2), row),
                   pl.BlockSpec((m, TOP_K), row),
                   pl.BlockSpec((m, TOP_K), row),
                   pl.BlockSpec((m, TOP_K), row),
                   pl.BlockSpec((1, LANES), lambda b, t: (0, 0))],
        out_shape=[jax.ShapeDtypeStruct((n, D), F32),
                   jax.ShapeDtypeStruct((n, D // 2), U32),
                   jax.ShapeDtypeStruct((n, TOP_K), I32),
                   jax.ShapeDtypeStruct((n, TOP_K), I32),
                   jax.ShapeDtypeStruct((n, TOP_K), F32),
                   jax.ShapeDtypeStruct((1, LANES), I32)],
        scratch_shapes=[pltpu.VMEM((8, LANES), F32)],
        compiler_params=_params(2),
        name="mix_router",
    )(y_lru2, y_ret, x, wo_b, g1, b1, rwh, rwl, rb2, sg_b, su_b, sd_b, tri)


def _dispatch_body(dest_ref, x_ref, xs_ref, sem, *, m):
    def row_copy(n, d):
        return pltpu.make_async_copy(x_ref.at[pl.ds(n, 1)], xs_ref.at[pl.ds(d, 1)], sem)

    def issue(n, c):
        for kk in range(TOP_K):
            row_copy(n, dest_ref[n * TOP_K + kk]).start()
        return c

    lax.fori_loop(0, m, issue, 0)

    def drain(n, c):
        for kk in range(TOP_K):
            row_copy(0, 0).wait()
        return c

    lax.fori_loop(0, m, drain, 0)


def _dispatch(xpk, dest_flat):
    n, c = xpk.shape
    m = min(MIX_ROWS, n)
    body = functools.partial(_dispatch_body, m=m)
    return pl.pallas_call(
        body,
        grid=(n // m,),
        in_specs=[pl.BlockSpec((m * TOP_K,), lambda i: (i,), memory_space=pltpu.SMEM),
                  pl.BlockSpec((m, c), lambda i: (i, 0))],
        out_specs=pl.BlockSpec(memory_space=pl.ANY),
        out_shape=jax.ShapeDtypeStruct((n * TOP_K, c), xpk.dtype),
        scratch_shapes=[pltpu.SemaphoreType.DMA(())],
        compiler_params=_params(1),
        name="dispatch",
    )(dest_flat, xpk)


def _gmm_body(ge_ref, gr_ref, glo_ref, ghi_ref, gfirst_ref, xs_ref, wg_ref, wu_ref, wd_ref, ys_ref,
              wgu_s, wd_s, *, tm):
    i = pl.program_id(0)
    hid = wg_ref.shape[2]
    e = ge_ref[i]
    e_prev = ge_ref[jnp.maximum(i - 1, 0)]

    @pl.when((i == 0) | (e != e_prev))
    def _():
        wgu_s[:, :hid] = wg_ref[0].astype(BF16)
        wgu_s[:, hid:] = wu_ref[0].astype(BF16)
        wd_s[...] = wd_ref[0].astype(BF16)

    lo = glo_ref[i]
    hi = ghi_ref[i]

    @pl.when(hi > lo)
    def _():
        xa, xb = _unpack_bf16_pair(xs_ref[...])
        x = jnp.concatenate([xa.astype(BF16), xb.astype(BF16)], axis=1)
        h2 = jnp.dot(x, wgu_s[...], preferred_element_type=F32)
        hg = h2[:, :hid]
        h = (hg * jax.nn.sigmoid(hg) * h2[:, hid:]).astype(BF16)
        y = _pack_bf16_pair(jnp.dot(h, wd_s[...], preferred_element_type=F32))
        rows = gr_ref[i] * tm + lax.broadcasted_iota(I32, y.shape, 0)
        valid = (rows >= lo) & (rows < hi)

        @pl.when(gfirst_ref[i] == 1)
        def _():
            ys_ref[...] = jnp.where(valid, y, jnp.uint32(0))

        @pl.when(gfirst_ref[i] != 1)
        def _():
            ys_ref[...] = jnp.where(valid, y, ys_ref[...])


def _gmm_metadata(counts, n_rows, tm):
    e = counts.shape[0]
    n_tiles = n_rows // tm
    n_items = n_tiles + e - 1
    ends = jnp.cumsum(counts)
    offs = ends - counts
    first_tile = offs // tm
    last_tile = jnp.maximum(ends - 1, 0) // tm
    per = jnp.where(counts > 0, last_tile - first_tile + 1, 0)
    item_end = jnp.cumsum(per)
    item_start = item_end - per
    total = item_end[-1]
    it = jnp.arange(n_items, dtype=I32)
    itc = jnp.minimum(it, total - 1)
    ge = jnp.searchsorted(item_end, itc, side="right").astype(I32)
    gr = (first_tile[ge] + (itc - item_start[ge])).astype(I32)
    live = it < total
    glo = jnp.where(live, jnp.maximum(offs[ge], gr * tm), 0).astype(I32)
    ghi = jnp.where(live, jnp.minimum(ends[ge], (gr + 1) * tm), 0).astype(I32)
    prev_r = jnp.concatenate([jnp.full((1,), -1, I32), gr[:-1]])
    gfirst = (live & (gr != prev_r)).astype(I32)
    return offs.astype(I32), (ge, gr, glo, ghi, gfirst)


def _gmm(xs, meta, wg, wu, wd):
    n_rows, c = xs.shape
    tm = min(GMM_ROWS, n_rows)
    e, d, hid = wg.shape
    n_items = n_rows // tm + e - 1
    body = functools.partial(_gmm_body, tm=tm)
    grid_spec = pltpu.PrefetchScalarGridSpec(
        num_scalar_prefetch=5,
        grid=(n_items,),
        in_specs=[pl.BlockSpec((tm, c), lambda i, ge, gr, glo, ghi, gf: (gr[i], 0)),
                  pl.BlockSpec((1, d, hid), lambda i, ge, gr, glo, ghi, gf: (ge[i], 0, 0)),
                  pl.BlockSpec((1, d, hid), lambda i, ge, gr, glo, ghi, gf: (ge[i], 0, 0)),
                  pl.BlockSpec((1, hid, d), lambda i, ge, gr, glo, ghi, gf: (ge[i], 0, 0))],
        out_specs=pl.BlockSpec((tm, c), lambda i, ge, gr, glo, ghi, gf: (gr[i], 0)),
        scratch_shapes=[pltpu.VMEM((d, 2 * hid), BF16), pltpu.VMEM((hid, d), BF16)],
    )
    return pl.pallas_call(
        body,
        grid_spec=grid_spec,
        out_shape=jax.ShapeDtypeStruct((n_rows, c), U32),
        compiler_params=_params(1),
        name="expert_gmm",
    )(*meta, xs, wg, wu, wd)


def _final_body(dest_ref, base_ref, wts_ref, p_ref, ys_ref, pg_ref, pb_ref, pp_ref, g2_ref, b2_ref,
                out_ref, rows_s, sem, *, bb, tm):
    m = bb * tm
    d = base_ref.shape[1]

    def row_copy(d_row, kk, n):
        return pltpu.make_async_copy(ys_ref.at[pl.ds(d_row, 1)], rows_s.at[kk, pl.ds(n, 1)], sem)

    def issue(n, c):
        for kk in range(TOP_K):
            row_copy(dest_ref[n * TOP_K + kk], kk, n).start()
        return c

    lax.fori_loop(0, m, issue, 0)

    def drain(n, c):
        for kk in range(TOP_K):
            row_copy(0, kk, 0).wait()
        return c

    lax.fori_loop(0, m, drain, 0)

    wts = wts_ref[...]
    r_hi = jnp.zeros((m, d // 2), F32)
    r_lo = jnp.zeros((m, d // 2), F32)
    for kk in range(TOP_K):
        a, b = _unpack_bf16_pair(rows_s[kk])
        wk = wts[:, kk:kk + 1]
        r_hi = r_hi + wk * a
        r_lo = r_lo + wk * b
    routed = jnp.concatenate([r_hi, r_lo], axis=1)
    x2 = _layer_norm(base_ref[...] + routed, g2_ref[...], b2_ref[...])
    gate = jax.nn.sigmoid(jnp.dot(x2.astype(BF16), pg_ref[...], preferred_element_type=F32) + pb_ref[...])
    proj = jnp.dot(p_ref[...].reshape(m, p_ref.shape[-1]).astype(BF16), pp_ref[...], preferred_element_type=F32)
    out_ref[...] = (x2 + gate * proj).reshape(bb, tm, d)


def _final(base, dest_flat, wts, p, ys, pg_b, pb, pp_b, g2, b2, B, T):
    n, D = base.shape
    tm = min(MIX_ROWS, T)
    bb = min(B, MIX_ROWS // tm)
    m = bb * tm
    nt = T // tm
    pd = p.shape[-1]
    row = lambda b, t: (b * nt + t, 0)
    body = functools.partial(_final_body, bb=bb, tm=tm)
    return pl.pallas_call(
        body,
        grid=(B // bb, nt),
        in_specs=[pl.BlockSpec((m * TOP_K,), lambda b, t: (b * nt + t,), memory_space=pltpu.SMEM),
                  pl.BlockSpec((m, D), row),
                  pl.BlockSpec((m, TOP_K), row),
                  pl.BlockSpec((bb, tm, pd), lambda b, t: (b, t, 0)),
                  pl.BlockSpec(memory_space=pl.ANY),
                  _const_spec((D, D)),
                  _const_spec((1, D)),
                  _const_spec((pd, D)),
                  _const_spec((1, D)),
                  _const_spec((1, D))],
        out_specs=pl.BlockSpec((bb, tm, D), lambda b, t: (b, t, 0)),
        out_shape=jax.ShapeDtypeStruct((B, T, D), F32),
        scratch_shapes=[pltpu.VMEM((TOP_K, m, D // 2), U32), pltpu.SemaphoreType.DMA(())],
        compiler_params=_params(2),
        name="combine_final",
    )(dest_flat, base, wts, p, ys, pg_b, pb, pp_b, g2, b2)


def _prep_layer(prm):
    (w_in, conv_w, conv_b, lru_wa, lru_ba, lru_wx, lru_bx, lru_lambda, ret_gn, w_out, ln1_g, ln1_b,
     router_w, router_b, e_gate, e_up, e_down, s_gate, s_up, s_down, ln2_g, ln2_b,
     ple_w_proj, ple_w_gate, ple_b_gate) = prm
    row = lambda v: v.reshape(1, -1)
    rw2 = jnp.concatenate([router_w, router_w], axis=1)
    rwh = rw2.astype(BF16)
    rwl = (rw2 - rwh.astype(F32)).astype(BF16)
    return dict(
        w_in=w_in.astype(BF16), conv_w=conv_w, conv_b=row(conv_b), wa=lru_wa.astype(BF16), ba=row(lru_ba),
        wx=lru_wx.astype(BF16), bx=row(lru_bx), lam=row(lru_lambda), gn=row(ret_gn), wo=w_out.astype(BF16),
        g1=row(ln1_g), b1=row(ln1_b), rwh=rwh, rwl=rwl, rb=row(jnp.concatenate([router_b, router_b])),
        e_gate=e_gate, e_up=e_up, e_down=e_down, sg=s_gate.astype(BF16), su=s_up.astype(BF16),
        sd=s_down.astype(BF16), g2=row(ln2_g), b2=row(ln2_b), pp=ple_w_proj.astype(BF16),
        pg=ple_w_gate.astype(BF16), pb=row(ple_b_gate))


def _layer(x, p, conv_st, lru_st, ret_st, pos0, chunk, w, alpha):
    B, T, D = x.shape
    W = conv_st.shape[-1]
    z_lru, z_ret = _inproj(x, w["w_in"], 2 * W)
    y_lru, conv_tm, new_lru = _lru(z_lru.reshape(T, B, 2 * W), jnp.transpose(conv_st, (1, 0, 2)), lru_st,
                                   w["conv_w"], w["conv_b"], w["wa"], w["ba"], w["wx"], w["bx"], w["lam"])
    y_ret, new_ret = _retention(z_ret, ret_st, w["gn"], pos0, chunk)
    base, xpk, eidx, pos, wts, counts = _mix(y_lru.reshape(T, B * W), y_ret, x, w["wo"], w["g1"], w["b1"],
                                             w["rwh"], w["rwl"], w["rb"], w["sg"], w["su"], w["sd"], alpha)
    n = B * T
    n_rows = n * TOP_K
    offs, meta = _gmm_metadata(counts[0, :N_EXPERTS], n_rows, min(GMM_ROWS, n_rows))
    dest = (offs[eidx] + pos).reshape(n_rows)
    xs = _dispatch(xpk, dest)
    ys = _gmm(xs, meta, w["e_gate"], w["e_up"], w["e_down"])
    out = _final(base, dest, wts, p, ys, w["pg"], w["pb"], w["pp"], w["g2"], w["b2"], B, T)
    return out, jnp.transpose(conv_tm, (1, 0, 2)), new_lru, new_ret


def kernel(x_prompt, x_sample, p_prompt, p_sample, state_conv, state_lru, state_ret, w_in, conv_w, conv_b,
           lru_wa, lru_ba, lru_wx, lru_bx, lru_lambda, ret_gn, w_out, ln1_g, ln1_b, router_w, router_b,
           exp_w_gate, exp_w_up, exp_w_down, sh_w_gate, sh_w_up, sh_w_down, ln2_g, ln2_b,
           ple_w_proj, ple_w_gate, ple_b_gate):
    depth = w_in.shape[0]
    alpha = (2 * depth) ** 0.25
    b_p = x_prompt.shape[0]
    t_s = x_sample.shape[1]
    W = state_conv.shape[-1]
    H, dh = state_ret.shape[2], state_ret.shape[3]
    hp, hs = x_prompt, x_sample
    outs = [[] for _ in range(6)]
    for i in range(depth):
        prm = (w_in[i], conv_w[i], conv_b[i], lru_wa[i], lru_ba[i], lru_wx[i], lru_bx[i], lru_lambda[i],
               ret_gn[i], w_out[i], ln1_g[i], ln1_b[i], router_w[i], router_b[i], exp_w_gate[i], exp_w_up[i],
               exp_w_down[i], sh_w_gate[i], sh_w_up[i], sh_w_down[i], ln2_g[i], ln2_b[i],
               ple_w_proj[i], ple_w_gate[i], ple_b_gate[i])
        w = _prep_layer(prm)
        zc = jnp.zeros((b_p, CONV_WIDTH - 1, W), x_prompt.dtype)
        zl = jnp.zeros((b_p, W), F32)
        zr = jnp.zeros((b_p, H, dh, dh), F32)
        hp, c, l, r = _layer(hp, p_prompt[i], zc, zl, zr, 0, CHUNK, w, alpha)
        outs[0].append(c), outs[1].append(l), outs[2].append(r)
        hs, c, l, r = _layer(hs, p_sample[i], state_conv[i], state_lru[i], state_ret[i], PAST_LEN, t_s, w, alpha)
        outs[3].append(c), outs[4].append(l), outs[5].append(r)
    return (hp, hs) + tuple(jnp.stack(o) for o in outs)
```

```python
import functools

import jax
import jax.numpy as jnp
from jax import lax
from jax.experimental import pallas as pl
from jax.experimental.pallas import tpu as pltpu

F32 = jnp.float32
BF16 = jnp.bfloat16
U32 = jnp.uint32
I32 = jnp.int32

CHUNK = 64
PAST_LEN = 1024
CONV_WIDTH = 4
LRU_C = 8.0
LRU_BLOCKS = 8
RET_HEADS = 8
ROPE_BASE = 10000.0
N_EXPERTS = 64
TOP_K = 8
N_GROUPS = 8
TOPK_GROUPS = 4
ROUTED_SCALE = 2.5
LN_EPS = 1e-5
GN_EPS = 1e-6

LANES = 128
ROWS_PER_STEP = 512
MIX_ROWS = 256
GMM_ROWS = 512
RET_GROUP_ROWS = 256
LRU_TIME_TILE = 64
VMEM_LIMIT = 56 * 1024 * 1024


def _const_spec(shape):
    zeros = (0,) * len(shape)
    return pl.BlockSpec(shape, lambda *_: zeros, pipeline_mode=pl.Buffered(1))


def _params(n_axes):
    return pltpu.CompilerParams(dimension_semantics=("arbitrary",) * n_axes,
                                vmem_limit_bytes=VMEM_LIMIT)


def _layer_norm(x, g, b):
    mu = jnp.mean(x, axis=-1, keepdims=True)
    xc = x - mu
    var = jnp.mean(xc * xc, axis=-1, keepdims=True)
    return xc * lax.rsqrt(var + LN_EPS) * g + b


def _pack_bf16_pair(x):
    c = x.shape[1] // 2
    xb = x.astype(BF16).astype(F32)
    hi = pltpu.bitcast(xb[:, :c], U32)
    lo = pltpu.bitcast(xb[:, c:], U32)
    return hi | (lo >> 16)


def _unpack_bf16_pair(pk):
    hi = pltpu.bitcast(pk & jnp.uint32(0xFFFF0000), F32)
    lo = pltpu.bitcast(pk << 16, F32)
    return hi, lo


def _inproj_body(x_ref, w_ref, zl_ref, zr_ref, *, bb, tm, lru_cols, tn):
    d = x_ref.shape[-1]
    x = x_ref[...].reshape(bb * tm, d).astype(BF16)
    for j in range(w_ref.shape[1] // tn):
        c0 = j * tn
        acc = jnp.dot(x, w_ref[:, c0:c0 + tn], preferred_element_type=F32)
        for b in range(bb):
            rows = acc[b * tm:(b + 1) * tm]
            if c0 < lru_cols:
                zl_ref[:, b * lru_cols + c0:b * lru_cols + c0 + tn] = rows
            else:
                zr_ref[b, :, c0 - lru_cols:c0 - lru_cols + tn] = rows.astype(BF16)


def _inproj(x, w_in_b, lru_cols):
    B, T, D = x.shape
    n_cols = w_in_b.shape[1]
    ret_cols = n_cols - lru_cols
    tm = min(ROWS_PER_STEP, T)
    bb = min(B, ROWS_PER_STEP // tm)
    body = functools.partial(_inproj_body, bb=bb, tm=tm, lru_cols=lru_cols, tn=512)
    return pl.pallas_call(
        body,
        grid=(B // bb, T // tm),
        in_specs=[pl.BlockSpec((bb, tm, D), lambda b, t: (b, t, 0)),
                  _const_spec((D, n_cols))],
        out_specs=[pl.BlockSpec((tm, bb * lru_cols), lambda b, t: (t, b)),
                   pl.BlockSpec((bb, tm, ret_cols), lambda b, t: (b, t, 0))],
        out_shape=[jax.ShapeDtypeStruct((T, B * lru_cols), F32),
                   jax.ShapeDtypeStruct((B, T, ret_cols), BF16)],
        compiler_params=_params(2),
        name="inproj",
    )(x, w_in_b)


def _lru_body(xl_ref, gl_ref, conv0_ref, h0_ref, cw_ref, cb_ref, wa_ref, ba_ref, wx_ref, bx_ref, lam_ref,
              y_ref, conv_out_ref, h_out_ref, xp_s, a_s, b_s, h_s, *, tt, rows):
    i = pl.program_id(0)
    B, W = h0_ref.shape
    nblk = wa_ref.shape[0]
    blk = W // nblk

    @pl.when(i == 0)
    def _():
        xp_s[0:CONV_WIDTH - 1] = conv0_ref[...]
        h_s[...] = h0_ref[...]

    xp_s[CONV_WIDTH - 1:] = xl_ref[...]

    lam = lam_ref[...]
    neg = -lam
    softplus = jnp.maximum(neg, 0.0) + jnp.log1p(jnp.exp(-jnp.abs(neg)))
    decay = (-LRU_C) * softplus

    def gates(c, carry):
        t0 = pl.multiple_of(c * rows, rows)
        xc = cb_ref[...].reshape(1, 1, W)
        for j in range(CONV_WIDTH):
            xc = xc + xp_s[pl.ds(t0 + j, rows)] * cw_ref[j:j + 1].reshape(1, 1, W)
        xc2 = xc.reshape(rows * B, W)
        xcb = xc2.astype(BF16)
        r_parts, i_parts = [], []
        for n in range(nblk):
            xb = xcb[:, n * blk:(n + 1) * blk]
            r_parts.append(jnp.dot(xb, wa_ref[n], preferred_element_type=F32))
            i_parts.append(jnp.dot(xb, wx_ref[n], preferred_element_type=F32))
        r = jax.nn.sigmoid(jnp.concatenate(r_parts, axis=1) + ba_ref[...])
        ig = jax.nn.sigmoid(jnp.concatenate(i_parts, axis=1) + bx_ref[...])
        a = jnp.exp(decay * r)
        bterm = jnp.sqrt(1.0 - a * a) * (ig * xc2)
        a_s[pl.ds(t0, rows)] = a.reshape(rows, B, W)
        b_s[pl.ds(t0, rows)] = bterm.reshape(rows, B, W)
        return carry

    lax.fori_loop(0, tt // rows, gates, 0)

    def step(t, h):
        hn = a_s[t] * h + b_s[t]
        y_ref[t] = hn * jax.nn.gelu(gl_ref[t])
        return hn

    h_last = lax.fori_loop(0, tt, step, h_s[...], unroll=8)
    h_s[...] = h_last
    tail = xp_s[tt:tt + CONV_WIDTH - 1]
    xp_s[0:CONV_WIDTH - 1] = tail
    conv_out_ref[...] = tail
    h_out_ref[...] = h_last


def _lru(z_lru3, conv0_tm, h0, conv_w, conv_b, wa_b, ba, wx_b, bx, lam):
    T, B, W2 = z_lru3.shape
    W = W2 // 2
    tt = min(LRU_TIME_TILE, T)
    rows = max(1, min(tt, 128 // B))
    body = functools.partial(_lru_body, tt=tt, rows=rows)
    nb = wa_b.shape[0]
    blk = W // nb
    return pl.pallas_call(
        body,
        grid=(T // tt,),
        in_specs=[pl.BlockSpec((tt, B, W), lambda t: (t, 0, 0)),
                  pl.BlockSpec((tt, B, W), lambda t: (t, 0, 1)),
                  _const_spec((CONV_WIDTH - 1, B, W)),
                  _const_spec((B, W)),
                  _const_spec((CONV_WIDTH, W)),
                  _const_spec((1, W)),
                  _const_spec((nb, blk, blk)),
                  _const_spec((1, W)),
                  _const_spec((nb, blk, blk)),
                  _const_spec((1, W)),
                  _const_spec((1, W))],
        out_specs=[pl.BlockSpec((tt, B, W), lambda t: (t, 0, 0)),
                   pl.BlockSpec((CONV_WIDTH - 1, B, W), lambda t: (0, 0, 0)),
                   pl.BlockSpec((B, W), lambda t: (0, 0))],
        out_shape=[jax.ShapeDtypeStruct((T, B, W), F32),
                   jax.ShapeDtypeStruct((CONV_WIDTH - 1, B, W), F32),
                   jax.ShapeDtypeStruct((B, W), F32)],
        scratch_shapes=[pltpu.VMEM((tt + CONV_WIDTH - 1, B, W), F32),
                        pltpu.VMEM((tt, B, W), F32),
                        pltpu.VMEM((tt, B, W), F32),
                        pltpu.VMEM((B, W), F32)],
        compiler_params=_params(1),
        name="rglru",
    )(z_lru3, z_lru3, conv0_tm, h0, conv_w, conv_b, wa_b, ba, wx_b, bx, lam)


def _ret_body(q_ref, k_ref, v_ref, g_ref, cos_ref, sin_ref, mask_ref, qdec_ref, kdec_ref, cdec_ref, gn_ref,
              s0_ref, y_ref, s_out_ref, *, rg, hb, dh):
    T = q_ref.shape[1]
    scale = dh ** -0.5

    def rope(t, cos, sin):
        return t * cos + pltpu.roll(t, dh // 2, axis=1) * sin

    for hh in range(hb):
        cols = slice(hh * dh, (hh + 1) * dh)

        def group(c, s, hh=hh, cols=cols):
            r0 = pl.multiple_of(c * rg, rg)
            rws = pl.ds(r0, rg)
            cos = cos_ref[rws, :]
            sin = sin_ref[rws, :]
            q = rope(q_ref[0, rws, cols].astype(F32), cos, sin)
            k = rope(k_ref[0, rws, cols].astype(F32), cos, sin) * scale
            v = v_ref[0, rws, cols]
            scores = lax.dot_general(q.astype(BF16), k.astype(BF16), (((1,), (1,)), ((), ())),
                                     preferred_element_type=F32)
            scores = scores * mask_ref[hh]
            o = jnp.dot(scores.astype(BF16), v, preferred_element_type=F32)
            o = o + jnp.dot((q * qdec_ref[hh]).astype(BF16), s.astype(BF16), preferred_element_type=F32)
            kd = (k * kdec_ref[hh]).astype(BF16)
            kv = lax.dot_general(kd, v, (((0,), (0,)), ((), ())), preferred_element_type=F32)
            s_new = cdec_ref[hh] * s + kv
            mu = jnp.mean(o, axis=-1, keepdims=True)
            oc = o - mu
            var = jnp.mean(oc * oc, axis=-1, keepdims=True)
            on = oc * lax.rsqrt(var + GN_EPS) * gn_ref[:, cols]
            g = g_ref[0, rws, cols].astype(F32)
            y_ref[0, rws, cols] = (g * jax.nn.sigmoid(g) * on).astype(y_ref.dtype)
            return s_new

        s_out_ref[0, hh] = lax.fori_loop(0, T // rg, group, s0_ref[0, hh])


def _retention_tables(T, pos0, chunk, rg, dh):
    half = dh // 2
    inv = ROPE_BASE ** (-jnp.arange(half, dtype=F32) / half)
    pos = pos0 + jnp.arange(T)
    ang = pos.astype(F32)[:, None] * inv[None, :]
    cos, sin = jnp.cos(ang), jnp.sin(ang)
    cos2 = jnp.concatenate([cos, cos], axis=1)
    sin2 = jnp.concatenate([-sin, sin], axis=1)
    log_g = jnp.log1p(-jnp.exp2(-5.0 - jnp.arange(RET_HEADS, dtype=F32)))[:, None, None]
    idx = jnp.arange(rg, dtype=F32)
    ci = jnp.floor(idx / chunk)
    diff = idx[:, None] - idx[None, :]
    same = ci[:, None] == ci[None, :]
    earlier = ci[None, :] < ci[:, None]
    dist = jnp.where(same, jnp.abs(diff), diff)
    mask = jnp.where(same | earlier, jnp.exp(dist[None] * log_g), 0.0)
    ones = jnp.ones((1, 1, dh), F32)
    qdec = jnp.exp((idx + 1.0)[None, :, None] * log_g) * ones
    kdec = jnp.exp((rg - 1.0 - idx)[None, :, None] * log_g) * ones
    cdec = jnp.exp(rg * log_g) * ones
    return cos2, sin2, mask, qdec, kdec, cdec


def _retention(z_ret, s0, gn, pos0, chunk):
    B, T, C4 = z_ret.shape
    H = RET_HEADS
    dh = C4 // (4 * H)
    rg = min(T, max(chunk, (RET_GROUP_ROWS // chunk) * chunk))
    hb = H if T * H * dh <= 64 * 1024 else 1
    nh = H // hb
    cos2, sin2, mask, qdec, kdec, cdec = _retention_tables(T, pos0, chunk, rg, dh)
    body = functools.partial(_ret_body, rg=rg, hb=hb, dh=dh)
    col = lambda off: (lambda b, h: (b, 0, off * nh + h))
    return pl.pallas_call(
        body,
        grid=(B, nh),
        in_specs=[pl.BlockSpec((1, T, hb * dh), col(0)),
                  pl.BlockSpec((1, T, hb * dh), col(1)),
                  pl.BlockSpec((1, T, hb * dh), col(2)),
                  pl.BlockSpec((1, T, hb * dh), col(3)),
                  _const_spec((T, dh)),
                  _const_spec((T, dh)),
                  pl.BlockSpec((hb, rg, rg), lambda b, h: (h, 0, 0)),
                  pl.BlockSpec((hb, rg, dh), lambda b, h: (h, 0, 0)),
                  pl.BlockSpec((hb, rg, dh), lambda b, h: (h, 0, 0)),
                  pl.BlockSpec((hb, 1, dh), lambda b, h: (h, 0, 0)),
                  pl.BlockSpec((1, hb * dh), lambda b, h: (0, h)),
                  pl.BlockSpec((1, hb, dh, dh), lambda b, h: (b, h, 0, 0))],
        out_specs=[pl.BlockSpec((1, T, hb * dh), lambda b, h: (b, 0, h)),
                   pl.BlockSpec((1, hb, dh, dh), lambda b, h: (b, h, 0, 0))],
        out_shape=[jax.ShapeDtypeStruct((B, T, H * dh), BF16),
                   jax.ShapeDtypeStruct((B, H, dh, dh), F32)],
        compiler_params=_params(2),
        name="retention",
    )(z_ret, z_ret, z_ret, z_ret, cos2, sin2, mask, qdec, kdec, cdec, gn, s0)


def _seg_allreduce(v, lane, op):
    for s in (1, 2, 4):
        up = pltpu.roll(v, LANES - s, axis=1)
        dn = pltpu.roll(v, s, axis=1)
        v = op(v, jnp.where((lane & s) == 0, up, dn))
    return v


def _mix_body(*refs, bb, tm, alpha, cap, n_steps, aliased):
    (yl_ref, yr_ref, x_ref, wo_ref, g1_ref, b1_ref, rwh_ref, rwl_ref, rb_ref, sg_ref, su_ref, sd_ref,
     tri_ref, cnt_in_ref) = refs[:14]
    refs = refs[15:] if aliased else refs[14:]
    base_ref, dest_ref, wts_ref, cnt_ref, xs_ref, carry_s, xpk_s, dv_s, ds_s, row_sems, idx_sem = refs
    i = pl.program_id(0)
    m = bb * tm
    d = x_ref.shape[-1]
    w = yl_ref.shape[1] // bb
    slot = i % 2

    @pl.when(i == 0)
    def _():
        carry_s[...] = jnp.zeros_like(carry_s)
        carry_s[0:1, :] = cnt_in_ref[...].astype(F32)

    yl = jnp.concatenate([yl_ref[:, b * w:(b + 1) * w] for b in range(bb)], axis=0).astype(BF16)
    yr = yr_ref[...].reshape(m, yr_ref.shape[-1])
    mix = jnp.dot(yl, wo_ref[:w], preferred_element_type=F32)
    mix = mix + jnp.dot(yr, wo_ref[w:], preferred_element_type=F32)
    x1 = _layer_norm(alpha * x_ref[...].reshape(m, d) + mix, g1_ref[...], b1_ref[...])
    x1b = x1.astype(BF16)

    x1l = (x1 - x1b.astype(F32)).astype(BF16)
    logits = jnp.dot(x1b, rwh_ref[...], preferred_element_type=F32)
    logits = logits + jnp.dot(x1l, rwh_ref[...], preferred_element_type=F32)
    logits = logits + jnp.dot(x1b, rwl_ref[...], preferred_element_type=F32)
    s = jax.nn.sigmoid(logits)
    sb = s + rb_ref[...]
    lane = lax.broadcasted_iota(I32, (m, LANES), 1)
    e_id = lane & (N_EXPERTS - 1)
    e_f = e_id.astype(F32)
    grp = e_id >> 3
    low = lane < N_EXPERTS
    big = jnp.float32(1e9)
    ninf = jnp.float32(-jnp.inf)

    m1 = _seg_allreduce(sb, lane, jnp.maximum)
    first_max = _seg_allreduce(jnp.where(sb == m1, e_f, big), lane, jnp.minimum)
    m2 = _seg_allreduce(jnp.where(e_f == first_max, ninf, sb), lane, jnp.maximum)
    gs = m1 + m2
    rank = jnp.zeros((m, LANES), F32)
    for dgrp in range(1, N_GROUPS):
        other = pltpu.roll(gs, 8 * dgrp, axis=1)
        beats = (other > gs) | ((other == gs) & (grp >= dgrp))
        rank = rank + jnp.where(beats, 1.0, 0.0)
    v = jnp.where(rank < TOPK_GROUPS, sb, ninf)

    idx_cols, w_cols = [], []
    sel = jnp.zeros((m, LANES), F32)
    for _ in range(TOP_K):
        mx = jnp.max(v, axis=1, keepdims=True)
        idx = jnp.min(jnp.where(v == mx, e_f, big), axis=1, keepdims=True)
        hit = e_f == idx
        w_cols.append(jnp.sum(jnp.where(hit & low, s, 0.0), axis=1, keepdims=True))
        idx_cols.append(idx)
        v = jnp.where(hit, ninf, v)
        sel = jnp.where(hit & low, 1.0, sel)

    cum = jnp.dot(tri_ref[...], sel.astype(BF16), preferred_element_type=F32) + carry_s[0:1, :]
    carry_s[0:1, :] = carry_s[0:1, :] + jnp.sum(sel, axis=0, keepdims=True)
    cnt_ref[...] = carry_s[0:1, :].astype(I32)

    wsum = w_cols[0]
    for c in w_cols[1:]:
        wsum = wsum + c
    d_out = jnp.zeros((m, LANES), F32)
    w_out = jnp.zeros((m, LANES), F32)
    for kk in range(TOP_K):
        hit = e_f == idx_cols[kk]
        pk = jnp.sum(jnp.where(hit & low, cum, 0.0), axis=1, keepdims=True)
        d_out = jnp.where(lane == kk, idx_cols[kk] * float(cap) + pk, d_out)
        w_out = jnp.where(lane == kk, w_cols[kk] / wsum * ROUTED_SCALE, w_out)
    wts_ref[...] = w_out[:, :TOP_K]
    dest_t = jnp.transpose(d_out)[:TOP_K].astype(I32)
    dest_ref[0] = dest_t
    dv_s[...] = dest_t
    to_smem = pltpu.make_async_copy(dv_s, ds_s, idx_sem)
    to_smem.start()

    hg = jnp.dot(x1b, sg_ref[...], preferred_element_type=F32)
    hu = jnp.dot(x1b, su_ref[...], preferred_element_type=F32)
    hs = (hg * jax.nn.sigmoid(hg) * hu).astype(BF16)
    base_ref[...] = alpha * x1 + jnp.dot(hs, sd_ref[...], preferred_element_type=F32)
    xpk_s[slot] = _pack_bf16_pair(x1)
    to_smem.wait()

    def row_copy(sl, n, dst):
        return pltpu.make_async_copy(xpk_s.at[sl, pl.ds(n, 1)], xs_ref.at[pl.ds(dst, 1)], row_sems.at[sl])

    def issue(n, c):
        for kk in range(TOP_K):
            row_copy(slot, n, ds_s[kk, n]).start()
        return c

    lax.fori_loop(0, m, issue, 0)

    def drain(sl):
        def one(n, c):
            for kk in range(TOP_K):
                row_copy(sl, 0, 0).wait()
            return c
        lax.fori_loop(0, m, one, 0)

    @pl.when(i > 0)
    def _():
        drain(1 - slot)

    @pl.when(i == n_steps - 1)
    def _():
        drain(slot)


def _mix(y_lru2, y_ret, x, w, alpha, cnt_in, xs, cap):
    B, T, D = x.shape
    W = y_ret.shape[-1]
    tm = min(MIX_ROWS, T)
    bb = min(B, MIX_ROWS // tm)
    m = bb * tm
    n = B * T
    nt = T // tm
    n_steps = (B // bb) * nt
    hs = w["sg"].shape[1]
    aliased = xs is not None
    tri = (lax.broadcasted_iota(I32, (m, m), 1) < lax.broadcasted_iota(I32, (m, m), 0)).astype(BF16)
    body = functools.partial(_mix_body, bb=bb, tm=tm, alpha=alpha, cap=cap, n_steps=n_steps, aliased=aliased)
    in_specs = [pl.BlockSpec((tm, bb * W), lambda i: (i % nt, i // nt)),
                pl.BlockSpec((bb, tm, W), lambda i: (i // nt, i % nt, 0)),
                pl.BlockSpec((bb, tm, D), lambda i: (i // nt, i % nt, 0)),
                _const_spec((2 * W, D)),
                _const_spec((1, D)),
                _const_spec((1, D)),
                _const_spec((D, LANES)),
                _const_spec((D, LANES)),
                _const_spec((1, LANES)),
                _const_spec((D, hs)),
                _const_spec((D, hs)),
                _const_spec((hs, D)),
                _const_spec((m, m)),
                _const_spec((1, LANES))]
    args = [y_lru2, y_ret, x, w["wo"], w["g1"], w["b1"], w["rwh"], w["rwl"], w["rb"], w["sg"], w["su"], w["sd"],
            tri, cnt_in]
    if aliased:
        in_specs.append(pl.BlockSpec(memory_space=pl.ANY))
        args.append(xs)
    return pl.pallas_call(
        body,
        grid=(n_steps,),
        in_specs=in_specs,
        out_specs=[pl.BlockSpec((m, D), lambda i: (i, 0)),
                   pl.BlockSpec((1, TOP_K, m), lambda i: (i, 0, 0)),
                   pl.BlockSpec((m, TOP_K), lambda i: (i, 0)),
                   pl.BlockSpec((1, LANES), lambda i: (0, 0)),
                   pl.BlockSpec(memory_space=pl.ANY)],
        out_shape=[jax.ShapeDtypeStruct((n, D), F32),
                   jax.ShapeDtypeStruct((n_steps, TOP_K, m), I32),
                   jax.ShapeDtypeStruct((n, TOP_K), F32),
                   jax.ShapeDtypeStruct((1, LANES), I32),
                   jax.ShapeDtypeStruct((N_EXPERTS * cap, D // 2), U32)],
        scratch_shapes=[pltpu.VMEM((8, LANES), F32),
                        pltpu.VMEM((2, m, D // 2), U32),
                        pltpu.VMEM((TOP_K, m), I32),
                        pltpu.SMEM((TOP_K, m), I32),
                        pltpu.SemaphoreType.DMA((2,)),
                        pltpu.SemaphoreType.DMA(())],
        input_output_aliases={14: 4} if aliased else {},
        compiler_params=_params(1),
        name="mix_router",
    )(*args)


def _gmm_body(ge_ref, gr_ref, gn_ref, xs_ref, wg_ref, wu_ref, wd_ref, ys_ref, wgu_s, wd_s):
    i = pl.program_id(0)
    hid = wg_ref.shape[2]
    e = ge_ref[i]
    e_prev = ge_ref[jnp.maximum(i - 1, 0)]

    @pl.when((i == 0) | (e != e_prev))
    def _():
        wgu_s[:, :hid] = wg_ref[0].astype(BF16)
        wgu_s[:, hid:] = wu_ref[0].astype(BF16)
        wd_s[...] = wd_ref[0].astype(BF16)

    n_valid = gn_ref[i]

    @pl.when(n_valid > 0)
    def _():
        pk = xs_ref[...]
        valid = lax.broadcasted_iota(I32, pk.shape, 0) < n_valid
        xa, xb = _unpack_bf16_pair(jnp.where(valid, pk, jnp.uint32(0)))
        x = jnp.concatenate([xa.astype(BF16), xb.astype(BF16)], axis=1)
        h2 = jnp.dot(x, wgu_s[...], preferred_element_type=F32)
        hg = h2[:, :hid]
        h = (hg * jax.nn.sigmoid(hg) * h2[:, hid:]).astype(BF16)
        ys_ref[...] = _pack_bf16_pair(jnp.dot(h, wd_s[...], preferred_element_type=F32))


def _gmm_metadata(counts, cap, tm, n_items):
    e = counts.shape[0]
    tiles = (counts + tm - 1) // tm
    item_end = jnp.cumsum(tiles)
    total = item_end[-1]
    it = jnp.arange(n_items, dtype=I32)
    itc = jnp.minimum(it, total - 1)
    ge = jnp.sum((item_end[None, :] <= itc[:, None]).astype(I32), axis=1)
    onehot = ge[:, None] == jnp.arange(e, dtype=I32)[None, :]
    start = jnp.sum(jnp.where(onehot, (item_end - tiles)[None, :], 0), axis=1)
    cnt = jnp.sum(jnp.where(onehot, counts[None, :], 0), axis=1)
    j = itc - start
    gr = ge * (cap // tm) + j
    gn = jnp.where(it < total, jnp.clip(cnt - j * tm, 0, tm), 0)
    return ge.astype(I32), gr.astype(I32), gn.astype(I32)


def _gmm(xs, counts, cap, n_tokens, wg, wu, wd):
    c = xs.shape[1]
    tm = GMM_ROWS
    e, d, hid = wg.shape
    n_items = (n_tokens * TOP_K) // tm + e
    meta = _gmm_metadata(counts, cap, tm, n_items)
    grid_spec = pltpu.PrefetchScalarGridSpec(
        num_scalar_prefetch=3,
        grid=(n_items,),
        in_specs=[pl.BlockSpec((tm, c), lambda i, ge, gr, gn: (gr[i], 0)),
                  pl.BlockSpec((1, d, hid), lambda i, ge, gr, gn: (ge[i], 0, 0)),
                  pl.BlockSpec((1, d, hid), lambda i, ge, gr, gn: (ge[i], 0, 0)),
                  pl.BlockSpec((1, hid, d), lambda i, ge, gr, gn: (ge[i], 0, 0))],
        out_specs=pl.BlockSpec((tm, c), lambda i, ge, gr, gn: (gr[i], 0)),
        scratch_shapes=[pltpu.VMEM((d, 2 * hid), BF16), pltpu.VMEM((hid, d), BF16)],
    )
    return pl.pallas_call(
        _gmm_body,
        grid_spec=grid_spec,
        out_shape=jax.ShapeDtypeStruct(xs.shape, U32),
        compiler_params=_params(1),
        name="expert_gmm",
    )(*meta, xs, wg, wu, wd)


def _final_body(d0_ref, dn_ref, base_ref, wts_ref, p_ref, ys_ref, pg_ref, pb_ref, pp_ref, g2_ref, b2_ref,
                out_ref, rows_s, sems, *, bb, tm, n_steps):
    i = pl.program_id(0)
    m = bb * tm
    d = base_ref.shape[1]
    slot = i % 2

    def row_copy(sl, d_row, kk, n):
        return pltpu.make_async_copy(ys_ref.at[pl.ds(d_row, 1)], rows_s.at[sl, kk, pl.ds(n, 1)], sems.at[sl])

    def issue(sl, dref):
        def one(n, c):
            for kk in range(TOP_K):
                row_copy(sl, dref[0, kk, n], kk, n).start()
            return c
        lax.fori_loop(0, m, one, 0)

    @pl.when(i == 0)
    def _():
        issue(0, d0_ref)

    @pl.when(i + 1 < n_steps)
    def _():
        issue(1 - slot, dn_ref)

    def drain(n, c):
        for kk in range(TOP_K):
            row_copy(slot, 0, kk, 0).wait()
        return c

    lax.fori_loop(0, m, drain, 0)

    wts = wts_ref[...]
    r_hi = jnp.zeros((m, d // 2), F32)
    r_lo = jnp.zeros((m, d // 2), F32)
    for kk in range(TOP_K):
        a, b = _unpack_bf16_pair(rows_s[slot, kk])
        wk = wts[:, kk:kk + 1]
        r_hi = r_hi + wk * a
        r_lo = r_lo + wk * b
    routed = jnp.concatenate([r_hi, r_lo], axis=1)
    x2 = _layer_norm(base_ref[...] + routed, g2_ref[...], b2_ref[...])
    gate = jax.nn.sigmoid(jnp.dot(x2.astype(BF16), pg_ref[...], preferred_element_type=F32) + pb_ref[...])
    proj = jnp.dot(p_ref[...].reshape(m, p_ref.shape[-1]).astype(BF16), pp_ref[...], preferred_element_type=F32)
    out_ref[...] = (x2 + gate * proj).reshape(bb, tm, d)


def _final(base, dest, wts, p, ys, w, B, T):
    n, D = base.shape
    tm = min(MIX_ROWS, T)
    bb = min(B, MIX_ROWS // tm)
    m = bb * tm
    nt = T // tm
    n_steps = (B // bb) * nt
    pd = p.shape[-1]
    body = functools.partial(_final_body, bb=bb, tm=tm, n_steps=n_steps)
    return pl.pallas_call(
        body,
        grid=(n_steps,),
        in_specs=[pl.BlockSpec((1, TOP_K, m), lambda i: (0, 0, 0), memory_space=pltpu.SMEM),
                  pl.BlockSpec((1, TOP_K, m), lambda i: (jnp.minimum(i + 1, n_steps - 1), 0, 0),
                               memory_space=pltpu.SMEM),
                  pl.BlockSpec((m, D), lambda i: (i, 0)),
                  pl.BlockSpec((m, TOP_K), lambda i: (i, 0)),
                  pl.BlockSpec((bb, tm, pd), lambda i: (i // nt, i % nt, 0)),
                  pl.BlockSpec(memory_space=pl.ANY),
                  _const_spec((D, D)),
                  _const_spec((1, D)),
                  _const_spec((pd, D)),
                  _const_spec((1, D)),
                  _const_spec((1, D))],
        out_specs=pl.BlockSpec((bb, tm, D), lambda i: (i // nt, i % nt, 0)),
        out_shape=jax.ShapeDtypeStruct((B, T, D), F32),
        scratch_shapes=[pltpu.VMEM((2, TOP_K, m, D // 2), U32), pltpu.SemaphoreType.DMA((2,))],
        compiler_params=_params(1),
        name="combine_final",
    )(dest, dest, base, wts, p, ys, w["pg"], w["pb"], w["pp"], w["g2"], w["b2"])


def _prep_layer(prm):
    (w_in, conv_w, conv_b, lru_wa, lru_ba, lru_wx, lru_bx, lru_lambda, ret_gn, w_out, ln1_g, ln1_b,
     router_w, router_b, e_gate, e_up, e_down, s_gate, s_up, s_down, ln2_g, ln2_b,
     ple_w_proj, ple_w_gate, ple_b_gate) = prm
    row = lambda v: v.reshape(1, -1)
    rw2 = jnp.concatenate([router_w, router_w], axis=1)
    rwh = rw2.astype(BF16)
    rwl = (rw2 - rwh.astype(F32)).astype(BF16)
    return dict(
        w_in=w_in.astype(BF16), conv_w=conv_w, conv_b=row(conv_b), wa=lru_wa.astype(BF16), ba=row(lru_ba),
        wx=lru_wx.astype(BF16), bx=row(lru_bx), lam=row(lru_lambda), gn=row(ret_gn), wo=w_out.astype(BF16),
        g1=row(ln1_g), b1=row(ln1_b), rwh=rwh, rwl=rwl, rb=row(jnp.concatenate([router_b, router_b])),
        e_gate=e_gate, e_up=e_up, e_down=e_down, sg=s_gate.astype(BF16), su=s_up.astype(BF16),
        sd=s_down.astype(BF16), g2=row(ln2_g), b2=row(ln2_b), pp=ple_w_proj.astype(BF16),
        pg=ple_w_gate.astype(BF16), pb=row(ple_b_gate))


def _mixers(x, conv_st, lru_st, ret_st, pos0, chunk, w):
    B, T, D = x.shape
    W = conv_st.shape[-1]
    z_lru, z_ret = _inproj(x, w["w_in"], 2 * W)
    y_lru, conv_tm, new_lru = _lru(z_lru.reshape(T, B, 2 * W), jnp.transpose(conv_st, (1, 0, 2)), lru_st,
                                   w["conv_w"], w["conv_b"], w["wa"], w["ba"], w["wx"], w["bx"], w["lam"])
    y_ret, new_ret = _retention(z_ret, ret_st, w["gn"], pos0, chunk)
    return y_lru.reshape(T, B * W), y_ret, jnp.transpose(conv_tm, (1, 0, 2)), new_lru, new_ret


def kernel(x_prompt, x_sample, p_prompt, p_sample, state_conv, state_lru, state_ret, w_in, conv_w, conv_b,
           lru_wa, lru_ba, lru_wx, lru_bx, lru_lambda, ret_gn, w_out, ln1_g, ln1_b, router_w, router_b,
           exp_w_gate, exp_w_up, exp_w_down, sh_w_gate, sh_w_up, sh_w_down, ln2_g, ln2_b,
           ple_w_proj, ple_w_gate, ple_b_gate):
    depth = w_in.shape[0]
    alpha = (2 * depth) ** 0.25
    b_p, t_p, _ = x_prompt.shape
    b_s, t_s, _ = x_sample.shape
    W = state_conv.shape[-1]
    H, dh = state_ret.shape[2], state_ret.shape[3]
    n_tokens = b_p * t_p + b_s * t_s
    cap = -(-n_tokens // GMM_ROWS) * GMM_ROWS
    hp, hs = x_prompt, x_sample
    outs = [[] for _ in range(6)]
    for i in range(depth):
        prm = (w_in[i], conv_w[i], conv_b[i], lru_wa[i], lru_ba[i], lru_wx[i], lru_bx[i], lru_lambda[i],
               ret_gn[i], w_out[i], ln1_g[i], ln1_b[i], router_w[i], router_b[i], exp_w_gate[i], exp_w_up[i],
               exp_w_down[i], sh_w_gate[i], sh_w_up[i], sh_w_down[i], ln2_g[i], ln2_b[i],
               ple_w_proj[i], ple_w_gate[i], ple_b_gate[i])
        w = _prep_layer(prm)
        zc = jnp.zeros((b_p, CONV_WIDTH - 1, W), x_prompt.dtype)
        zl = jnp.zeros((b_p, W), F32)
        zr = jnp.zeros((b_p, H, dh, dh), F32)
        yl_p, yr_p, c_p, l_p, r_p = _mixers(hp, zc, zl, zr, 0, CHUNK, w)
        yl_s, yr_s, c_s, l_s, r_s = _mixers(hs, state_conv[i], state_lru[i], state_ret[i], PAST_LEN, t_s, w)
        for o, val in zip(outs, (c_p, l_p, r_p, c_s, l_s, r_s)):
            o.append(val)
        base_p, dest_p, wts_p, cnt_p, xs = _mix(yl_p, yr_p, hp, w, alpha, jnp.zeros((1, LANES), I32), None, cap)
        base_s, dest_s, wts_s, cnt_all, xs = _mix(yl_s, yr_s, hs, w, alpha, cnt_p, xs, cap)
        ys = _gmm(xs, cnt_all[0, :N_EXPERTS], cap, n_tokens, w["e_gate"], w["e_up"], w["e_down"])
        hp = _final(base_p, dest_p, wts_p, p_prompt[i], ys, w, b_p, t_p)
        hs = _final(base_s, dest_s, wts_s, p_sample[i], ys, w, b_s, t_s)
    return (hp, hs) + tuple(jnp.stack(o) for o in outs)
```

```python
import functools

import jax
import jax.numpy as jnp
from jax import lax
from jax.experimental import pallas as pl
from jax.experimental.pallas import tpu as pltpu

F32 = jnp.float32
BF16 = jnp.bfloat16
U32 = jnp.uint32
I32 = jnp.int32

CHUNK = 64
PAST_LEN = 1024
CONV_WIDTH = 4
LRU_C = 8.0
LRU_BLOCKS = 8
RET_HEADS = 8
ROPE_BASE = 10000.0
N_EXPERTS = 64
TOP_K = 8
N_GROUPS = 8
TOPK_GROUPS = 4
ROUTED_SCALE = 2.5
LN_EPS = 1e-5
GN_EPS = 1e-6

LANES = 128
ROW_SUBLANES = 8
DRAIN_UNROLL = 8
ROWS_PER_STEP = 512
MIX_ROWS = 256
GMM_ROWS = 512
RET_GROUP_ROWS = 256
LRU_TIME_TILE = 64
VMEM_LIMIT = 56 * 1024 * 1024


def _const_spec(shape):
    zeros = (0,) * len(shape)
    return pl.BlockSpec(shape, lambda *_: zeros, pipeline_mode=pl.Buffered(1))


def _params(n_axes):
    return pltpu.CompilerParams(dimension_semantics=("arbitrary",) * n_axes,
                                vmem_limit_bytes=VMEM_LIMIT)


def _layer_norm(x, g, b):
    mu = jnp.mean(x, axis=-1, keepdims=True)
    xc = x - mu
    var = jnp.mean(xc * xc, axis=-1, keepdims=True)
    return xc * lax.rsqrt(var + LN_EPS) * g + b


def _pack_bf16_pair(x):
    c = x.shape[1] // 2
    xb = x.astype(BF16).astype(F32)
    hi = pltpu.bitcast(xb[:, :c], U32)
    lo = pltpu.bitcast(xb[:, c:], U32)
    return hi | (lo >> 16)


def _unpack_bf16_pair(pk):
    hi = pltpu.bitcast(pk & jnp.uint32(0xFFFF0000), F32)
    lo = pltpu.bitcast(pk << 16, F32)
    return hi, lo


def _inproj_body(x_ref, w_ref, zl_ref, zr_ref, *, bb, tm, lru_cols, tn):
    d = x_ref.shape[-1]
    x = x_ref[...].reshape(bb * tm, d).astype(BF16)
    for j in range(w_ref.shape[1] // tn):
        c0 = j * tn
        acc = jnp.dot(x, w_ref[:, c0:c0 + tn], preferred_element_type=F32)
        for b in range(bb):
            rows = acc[b * tm:(b + 1) * tm]
            if c0 < lru_cols:
                zl_ref[:, b * lru_cols + c0:b * lru_cols + c0 + tn] = rows
            else:
                zr_ref[b, :, c0 - lru_cols:c0 - lru_cols + tn] = rows.astype(BF16)


def _inproj(x, w_in_b, lru_cols):
    B, T, D = x.shape
    n_cols = w_in_b.shape[1]
    ret_cols = n_cols - lru_cols
    tm = min(ROWS_PER_STEP, T)
    bb = min(B, ROWS_PER_STEP // tm)
    body = functools.partial(_inproj_body, bb=bb, tm=tm, lru_cols=lru_cols, tn=512)
    return pl.pallas_call(
        body,
        grid=(B // bb, T // tm),
        in_specs=[pl.BlockSpec((bb, tm, D), lambda b, t: (b, t, 0)),
                  _const_spec((D, n_cols))],
        out_specs=[pl.BlockSpec((tm, bb * lru_cols), lambda b, t: (t, b)),
                   pl.BlockSpec((bb, tm, ret_cols), lambda b, t: (b, t, 0))],
        out_shape=[jax.ShapeDtypeStruct((T, B * lru_cols), F32),
                   jax.ShapeDtypeStruct((B, T, ret_cols), BF16)],
        compiler_params=_params(2),
        name="inproj",
    )(x, w_in_b)


def _lru_body(xl_ref, gl_ref, conv0_ref, h0_ref, cw_ref, cb_ref, wa_ref, ba_ref, wx_ref, bx_ref, lam_ref,
              y_ref, conv_out_ref, h_out_ref, xp_s, a_s, b_s, h_s, *, tt, rows):
    i = pl.program_id(0)
    B, W = h0_ref.shape
    nblk = wa_ref.shape[0]
    blk = W // nblk

    @pl.when(i == 0)
    def _():
        xp_s[0:CONV_WIDTH - 1] = conv0_ref[...]
        h_s[...] = h0_ref[...]

    xp_s[CONV_WIDTH - 1:] = xl_ref[...]

    lam = lam_ref[...]
    neg = -lam
    softplus = jnp.maximum(neg, 0.0) + jnp.log1p(jnp.exp(-jnp.abs(neg)))
    decay = (-LRU_C) * softplus

    def gates(c, carry):
        t0 = pl.multiple_of(c * rows, rows)
        xc = cb_ref[...].reshape(1, 1, W)
        for j in range(CONV_WIDTH):
            xc = xc + xp_s[pl.ds(t0 + j, rows)] * cw_ref[j:j + 1].reshape(1, 1, W)
        xc2 = xc.reshape(rows * B, W)
        xcb = xc2.astype(BF16)
        r_parts, i_parts = [], []
        for n in range(nblk):
            xb = xcb[:, n * blk:(n + 1) * blk]
            r_parts.append(jnp.dot(xb, wa_ref[n], preferred_element_type=F32))
            i_parts.append(jnp.dot(xb, wx_ref[n], preferred_element_type=F32))
        r = jax.nn.sigmoid(jnp.concatenate(r_parts, axis=1) + ba_ref[...])
        ig = jax.nn.sigmoid(jnp.concatenate(i_parts, axis=1) + bx_ref[...])
        a = jnp.exp(decay * r)
        bterm = jnp.sqrt(1.0 - a * a) * (ig * xc2)
        a_s[pl.ds(t0, rows)] = a.reshape(rows, B, W)
        b_s[pl.ds(t0, rows)] = bterm.reshape(rows, B, W)
        return carry

    lax.fori_loop(0, tt // rows, gates, 0)

    def step(t, h):
        hn = a_s[t] * h + b_s[t]
        y_ref[t] = hn * jax.nn.gelu(gl_ref[t])
        return hn

    h_last = lax.fori_loop(0, tt, step, h_s[...], unroll=8)
    h_s[...] = h_last
    tail = xp_s[tt:tt + CONV_WIDTH - 1]
    xp_s[0:CONV_WIDTH - 1] = tail
    conv_out_ref[...] = tail
    h_out_ref[...] = h_last


def _lru(z_lru3, conv0_tm, h0, conv_w, conv_b, wa_b, ba, wx_b, bx, lam):
    T, B, W2 = z_lru3.shape
    W = W2 // 2
    tt = min(LRU_TIME_TILE, T)
    rows = max(1, min(tt, 128 // B))
    body = functools.partial(_lru_body, tt=tt, rows=rows)
    nb = wa_b.shape[0]
    blk = W // nb
    return pl.pallas_call(
        body,
        grid=(T // tt,),
        in_specs=[pl.BlockSpec((tt, B, W), lambda t: (t, 0, 0)),
                  pl.BlockSpec((tt, B, W), lambda t: (t, 0, 1)),
                  _const_spec((CONV_WIDTH - 1, B, W)),
                  _const_spec((B, W)),
                  _const_spec((CONV_WIDTH, W)),
                  _const_spec((1, W)),
                  _const_spec((nb, blk, blk)),
                  _const_spec((1, W)),
                  _const_spec((nb, blk, blk)),
                  _const_spec((1, W)),
                  _const_spec((1, W))],
        out_specs=[pl.BlockSpec((tt, B, W), lambda t: (t, 0, 0)),
                   pl.BlockSpec((CONV_WIDTH - 1, B, W), lambda t: (0, 0, 0)),
                   pl.BlockSpec((B, W), lambda t: (0, 0))],
        out_shape=[jax.ShapeDtypeStruct((T, B, W), F32),
                   jax.ShapeDtypeStruct((CONV_WIDTH - 1, B, W), F32),
                   jax.ShapeDtypeStruct((B, W), F32)],
        scratch_shapes=[pltpu.VMEM((tt + CONV_WIDTH - 1, B, W), F32),
                        pltpu.VMEM((tt, B, W), F32),
                        pltpu.VMEM((tt, B, W), F32),
                        pltpu.VMEM((B, W), F32)],
        compiler_params=_params(1),
        name="rglru",
    )(z_lru3, z_lru3, conv0_tm, h0, conv_w, conv_b, wa_b, ba, wx_b, bx, lam)


def _ret_body(q_ref, k_ref, v_ref, g_ref, cos_ref, sin_ref, mask_ref, qdec_ref, kdec_ref, cdec_ref, gn_ref,
              s0_ref, y_ref, s_out_ref, *, rg, hb, dh):
    T = q_ref.shape[1]
    scale = dh ** -0.5

    def rope(t, cos, sin):
        return t * cos + pltpu.roll(t, dh // 2, axis=1) * sin

    for hh in range(hb):
        cols = slice(hh * dh, (hh + 1) * dh)

        def group(c, s, hh=hh, cols=cols):
            r0 = pl.multiple_of(c * rg, rg)
            rws = pl.ds(r0, rg)
            cos = cos_ref[rws, :]
            sin = sin_ref[rws, :]
            q = rope(q_ref[0, rws, cols].astype(F32), cos, sin)
            k = rope(k_ref[0, rws, cols].astype(F32), cos, sin) * scale
            v = v_ref[0, rws, cols]
            scores = lax.dot_general(q.astype(BF16), k.astype(BF16), (((1,), (1,)), ((), ())),
                                     preferred_element_type=F32)
            scores = scores * mask_ref[hh]
            o = jnp.dot(scores.astype(BF16), v, preferred_element_type=F32)
            o = o + jnp.dot((q * qdec_ref[hh]).astype(BF16), s.astype(BF16), preferred_element_type=F32)
            kd = (k * kdec_ref[hh]).astype(BF16)
            kv = lax.dot_general(kd, v, (((0,), (0,)), ((), ())), preferred_element_type=F32)
            s_new = cdec_ref[hh] * s + kv
            mu = jnp.mean(o, axis=-1, keepdims=True)
            oc = o - mu
            var = jnp.mean(oc * oc, axis=-1, keepdims=True)
            on = oc * lax.rsqrt(var + GN_EPS) * gn_ref[:, cols]
            g = g_ref[0, rws, cols].astype(F32)
            y_ref[0, rws, cols] = (g * jax.nn.sigmoid(g) * on).astype(y_ref.dtype)
            return s_new

        s_out_ref[0, hh] = lax.fori_loop(0, T // rg, group, s0_ref[0, hh])


def _retention_tables(T, pos0, chunk, rg, dh):
    half = dh // 2
    inv = ROPE_BASE ** (-jnp.arange(half, dtype=F32) / half)
    pos = pos0 + jnp.arange(T)
    ang = pos.astype(F32)[:, None] * inv[None, :]
    cos, sin = jnp.cos(ang), jnp.sin(ang)
    cos2 = jnp.concatenate([cos, cos], axis=1)
    sin2 = jnp.concatenate([-sin, sin], axis=1)
    log_g = jnp.log1p(-jnp.exp2(-5.0 - jnp.arange(RET_HEADS, dtype=F32)))[:, None, None]
    idx = jnp.arange(rg, dtype=F32)
    ci = jnp.floor(idx / chunk)
    diff = idx[:, None] - idx[None, :]
    same = ci[:, None] == ci[None, :]
    earlier = ci[None, :] < ci[:, None]
    dist = jnp.where(same, jnp.abs(diff), diff)
    mask = jnp.where(same | earlier, jnp.exp(dist[None] * log_g), 0.0)
    ones = jnp.ones((1, 1, dh), F32)
    qdec = jnp.exp((idx + 1.0)[None, :, None] * log_g) * ones
    kdec = jnp.exp((rg - 1.0 - idx)[None, :, None] * log_g) * ones
    cdec = jnp.exp(rg * log_g) * ones
    return cos2, sin2, mask, qdec, kdec, cdec


def _retention(z_ret, s0, gn, pos0, chunk):
    B, T, C4 = z_ret.shape
    H = RET_HEADS
    dh = C4 // (4 * H)
    rg = min(T, max(chunk, (RET_GROUP_ROWS // chunk) * chunk))
    hb = H if T * H * dh <= 64 * 1024 else 1
    nh = H // hb
    cos2, sin2, mask, qdec, kdec, cdec = _retention_tables(T, pos0, chunk, rg, dh)
    body = functools.partial(_ret_body, rg=rg, hb=hb, dh=dh)
    col = lambda off: (lambda b, h: (b, 0, off * nh + h))
    return pl.pallas_call(
        body,
        grid=(B, nh),
        in_specs=[pl.BlockSpec((1, T, hb * dh), col(0)),
                  pl.BlockSpec((1, T, hb * dh), col(1)),
                  pl.BlockSpec((1, T, hb * dh), col(2)),
                  pl.BlockSpec((1, T, hb * dh), col(3)),
                  _const_spec((T, dh)),
                  _const_spec((T, dh)),
                  pl.BlockSpec((hb, rg, rg), lambda b, h: (h, 0, 0)),
                  pl.BlockSpec((hb, rg, dh), lambda b, h: (h, 0, 0)),
                  pl.BlockSpec((hb, rg, dh), lambda b, h: (h, 0, 0)),
                  pl.BlockSpec((hb, 1, dh), lambda b, h: (h, 0, 0)),
                  pl.BlockSpec((1, hb * dh), lambda b, h: (0, h)),
                  pl.BlockSpec((1, hb, dh, dh), lambda b, h: (b, h, 0, 0))],
        out_specs=[pl.BlockSpec((1, T, hb * dh), lambda b, h: (b, 0, h)),
                   pl.BlockSpec((1, hb, dh, dh), lambda b, h: (b, h, 0, 0))],
        out_shape=[jax.ShapeDtypeStruct((B, T, H * dh), BF16),
                   jax.ShapeDtypeStruct((B, H, dh, dh), F32)],
        compiler_params=_params(2),
        name="retention",
    )(z_ret, z_ret, z_ret, z_ret, cos2, sin2, mask, qdec, kdec, cdec, gn, s0)


def _seg_allreduce(v, lane, op):
    for s in (1, 2, 4):
        up = pltpu.roll(v, LANES - s, axis=1)
        dn = pltpu.roll(v, s, axis=1)
        v = op(v, jnp.where((lane & s) == 0, up, dn))
    return v


def _mix_body(*refs, bb, tm, alpha, cap, n_steps, aliased):
    (yl_ref, yr_ref, x_ref, wo_ref, g1_ref, b1_ref, rwh_ref, rwl_ref, rb_ref, sg_ref, su_ref, sd_ref,
     tri_ref, cnt_in_ref) = refs[:14]
    refs = refs[15:] if aliased else refs[14:]
    base_ref, dest_ref, wts_ref, cnt_ref, xs_ref, carry_s, xpk_s, dv_s, ds_s, row_sems, idx_sem = refs
    i = pl.program_id(0)
    m = bb * tm
    d = x_ref.shape[-1]
    w = yl_ref.shape[1] // bb
    slot = i % 2

    @pl.when(i == 0)
    def _():
        carry_s[...] = jnp.zeros_like(carry_s)
        carry_s[0:1, :] = cnt_in_ref[...].astype(F32)

    yl = jnp.concatenate([yl_ref[:, b * w:(b + 1) * w] for b in range(bb)], axis=0).astype(BF16)
    yr = yr_ref[...].reshape(m, yr_ref.shape[-1])
    mix = jnp.dot(yl, wo_ref[:w], preferred_element_type=F32)
    mix = mix + jnp.dot(yr, wo_ref[w:], preferred_element_type=F32)
    x1 = _layer_norm(alpha * x_ref[...].reshape(m, d) + mix, g1_ref[...], b1_ref[...])
    x1b = x1.astype(BF16)

    x1l = (x1 - x1b.astype(F32)).astype(BF16)
    logits = jnp.dot(x1b, rwh_ref[...], preferred_element_type=F32)
    logits = logits + jnp.dot(x1l, rwh_ref[...], preferred_element_type=F32)
    logits = logits + jnp.dot(x1b, rwl_ref[...], preferred_element_type=F32)
    s = jax.nn.sigmoid(logits)
    sb = s + rb_ref[...]
    lane = lax.broadcasted_iota(I32, (m, LANES), 1)
    e_id = lane & (N_EXPERTS - 1)
    e_f = e_id.astype(F32)
    grp = e_id >> 3
    low = lane < N_EXPERTS
    big = jnp.float32(1e9)
    ninf = jnp.float32(-jnp.inf)

    m1 = _seg_allreduce(sb, lane, jnp.maximum)
    first_max = _seg_allreduce(jnp.where(sb == m1, e_f, big), lane, jnp.minimum)
    m2 = _seg_allreduce(jnp.where(e_f == first_max, ninf, sb), lane, jnp.maximum)
    gs = m1 + m2
    rank = jnp.zeros((m, LANES), F32)
    for dgrp in range(1, N_GROUPS):
        other = pltpu.roll(gs, 8 * dgrp, axis=1)
        beats = (other > gs) | ((other == gs) & (grp >= dgrp))
        rank = rank + jnp.where(beats, 1.0, 0.0)
    v = jnp.where(rank < TOPK_GROUPS, sb, ninf)

    idx_cols, w_cols = [], []
    sel = jnp.zeros((m, LANES), F32)
    for _ in range(TOP_K):
        mx = jnp.max(v, axis=1, keepdims=True)
        idx = jnp.min(jnp.where(v == mx, e_f, big), axis=1, keepdims=True)
        hit = e_f == idx
        w_cols.append(jnp.sum(jnp.where(hit & low, s, 0.0), axis=1, keepdims=True))
        idx_cols.append(idx)
        v = jnp.where(hit, ninf, v)
        sel = jnp.where(hit & low, 1.0, sel)

    cum = jnp.dot(tri_ref[...], sel.astype(BF16), preferred_element_type=F32) + carry_s[0:1, :]
    carry_s[0:1, :] = carry_s[0:1, :] + jnp.sum(sel, axis=0, keepdims=True)
    cnt_ref[...] = carry_s[0:1, :].astype(I32)

    wsum = w_cols[0]
    for c in w_cols[1:]:
        wsum = wsum + c
    d_out = jnp.zeros((m, LANES), F32)
    w_out = jnp.zeros((m, LANES), F32)
    for kk in range(TOP_K):
        hit = e_f == idx_cols[kk]
        pk = jnp.sum(jnp.where(hit & low, cum, 0.0), axis=1, keepdims=True)
        d_out = jnp.where(lane == kk, idx_cols[kk] * float(cap) + pk, d_out)
        w_out = jnp.where(lane == kk, w_cols[kk] / wsum * ROUTED_SCALE, w_out)
    wts_ref[0] = jnp.transpose(w_out)[:TOP_K]
    dest_t = jnp.transpose(d_out)[:TOP_K].astype(I32)
    dest_ref[0] = dest_t
    dv_s[...] = dest_t
    to_smem = pltpu.make_async_copy(dv_s, ds_s, idx_sem)
    to_smem.start()

    hg = jnp.dot(x1b, sg_ref[...], preferred_element_type=F32)
    hu = jnp.dot(x1b, su_ref[...], preferred_element_type=F32)
    hs = (hg * jax.nn.sigmoid(hg) * hu).astype(BF16)
    base_ref[...] = alpha * x1 + jnp.dot(hs, sd_ref[...], preferred_element_type=F32)
    pk = _pack_bf16_pair(x1)
    for sub in range(ROW_SUBLANES):
        xpk_s[slot, :, sub, :] = pk[:, sub * LANES:(sub + 1) * LANES]
    to_smem.wait()

    def row_copy(sl, n, dst):
        return pltpu.make_async_copy(xpk_s.at[sl, n], xs_ref.at[dst], row_sems.at[sl])

    def issue(n, c):
        for kk in range(TOP_K):
            row_copy(slot, n, ds_s[kk, n]).start()
        return c

    lax.fori_loop(0, m, issue, 0)

    def drain(sl):
        def one(n, c):
            for _ in range(DRAIN_UNROLL * TOP_K):
                row_copy(sl, 0, 0).wait()
            return c
        lax.fori_loop(0, m // DRAIN_UNROLL, one, 0)

    @pl.when(i > 0)
    def _():
        drain(1 - slot)

    @pl.when(i == n_steps - 1)
    def _():
        drain(slot)


def _mix(y_lru2, y_ret, x, w, alpha, cnt_in, xs, cap):
    B, T, D = x.shape
    assert D == 2 * ROW_SUBLANES * LANES, "a packed token row must fill exactly one (8, 128) tile"
    W = y_ret.shape[-1]
    tm = min(MIX_ROWS, T)
    bb = min(B, MIX_ROWS // tm)
    m = bb * tm
    n = B * T
    nt = T // tm
    n_steps = (B // bb) * nt
    hs = w["sg"].shape[1]
    aliased = xs is not None
    tri = (lax.broadcasted_iota(I32, (m, m), 1) < lax.broadcasted_iota(I32, (m, m), 0)).astype(BF16)
    body = functools.partial(_mix_body, bb=bb, tm=tm, alpha=alpha, cap=cap, n_steps=n_steps, aliased=aliased)
    in_specs = [pl.BlockSpec((tm, bb * W), lambda i: (i % nt, i // nt)),
                pl.BlockSpec((bb, tm, W), lambda i: (i // nt, i % nt, 0)),
                pl.BlockSpec((bb, tm, D), lambda i: (i // nt, i % nt, 0)),
                _const_spec((2 * W, D)),
                _const_spec((1, D)),
                _const_spec((1, D)),
                _const_spec((D, LANES)),
                _const_spec((D, LANES)),
                _const_spec((1, LANES)),
                _const_spec((D, hs)),
                _const_spec((D, hs)),
                _const_spec((hs, D)),
                _const_spec((m, m)),
                _const_spec((1, LANES))]
    args = [y_lru2, y_ret, x, w["wo"], w["g1"], w["b1"], w["rwh"], w["rwl"], w["rb"], w["sg"], w["su"], w["sd"],
            tri, cnt_in]
    if aliased:
        in_specs.append(pl.BlockSpec(memory_space=pl.ANY))
        args.append(xs)
    return pl.pallas_call(
        body,
        grid=(n_steps,),
        in_specs=in_specs,
        out_specs=[pl.BlockSpec((m, D), lambda i: (i, 0)),
                   pl.BlockSpec((1, TOP_K, m), lambda i: (i, 0, 0)),
                   pl.BlockSpec((1, TOP_K, m), lambda i: (i, 0, 0)),
                   pl.BlockSpec((1, LANES), lambda i: (0, 0)),
                   pl.BlockSpec(memory_space=pl.ANY)],
        out_shape=[jax.ShapeDtypeStruct((n, D), F32),
                   jax.ShapeDtypeStruct((n_steps, TOP_K, m), I32),
                   jax.ShapeDtypeStruct((n_steps, TOP_K, m), F32),
                   jax.ShapeDtypeStruct((1, LANES), I32),
                   jax.ShapeDtypeStruct((N_EXPERTS * cap, ROW_SUBLANES, LANES), U32)],
        scratch_shapes=[pltpu.VMEM((8, LANES), F32),
                        pltpu.VMEM((2, m, ROW_SUBLANES, LANES), U32),
                        pltpu.VMEM((TOP_K, m), I32),
                        pltpu.SMEM((TOP_K, m), I32),
                        pltpu.SemaphoreType.DMA((2,)),
                        pltpu.SemaphoreType.DMA(())],
        input_output_aliases={14: 4} if aliased else {},
        compiler_params=_params(1),
        name="mix_router",
    )(*args)


def _gmm_body(ge_ref, gr_ref, gn_ref, gt_ref, xs_ref, wg_ref, wu_ref, wd_ref, ys_ref,
              wgu_s, wd_s, x_s, y_s, in_sems, out_sems, *, tm, n_items):
    i = pl.program_id(0)
    hid = wg_ref.shape[2]
    slot = i % 2

    def tile_copies(to_vmem, item, sl):
        r0 = pl.multiple_of(gr_ref[item] * tm, tm)
        out = []
        for sub in range(ROW_SUBLANES):
            cols = pl.ds(sub * LANES, LANES)
            if to_vmem:
                out.append(pltpu.make_async_copy(xs_ref.at[pl.ds(r0, tm), sub, :], x_s.at[sl, :, cols],
                                                 in_sems.at[sl]))
            else:
                out.append(pltpu.make_async_copy(y_s.at[sl, :, cols], ys_ref.at[pl.ds(r0, tm), sub, :],
                                                 out_sems.at[sl]))
        return out

    @pl.when(i == 0)
    def _():
        for cp in tile_copies(True, 0, 0):
            cp.start()

    nxt = jnp.minimum(i + 1, n_items - 1)

    @pl.when((i + 1 < n_items) & (gn_ref[nxt] > 0))
    def _():
        for cp in tile_copies(True, nxt, 1 - slot):
            cp.start()

    e = ge_ref[i]
    e_prev = ge_ref[jnp.maximum(i - 1, 0)]

    @pl.when((i == 0) | (e != e_prev))
    def _():
        wgu_s[:, :hid] = wg_ref[0].astype(BF16)
        wgu_s[:, hid:] = wu_ref[0].astype(BF16)
        wd_s[...] = wd_ref[0].astype(BF16)

    n_valid = gn_ref[i]

    @pl.when(n_valid > 0)
    def _():
        for cp in tile_copies(True, i, slot):
            cp.wait()

        @pl.when(i >= 2)
        def _():
            for cp in tile_copies(False, i, slot):
                cp.wait()

        pk = x_s[slot]
        valid = lax.broadcasted_iota(I32, pk.shape, 0) < n_valid
        xa, xb = _unpack_bf16_pair(jnp.where(valid, pk, jnp.uint32(0)))
        x = jnp.concatenate([xa.astype(BF16), xb.astype(BF16)], axis=1)
        h2 = jnp.dot(x, wgu_s[...], preferred_element_type=F32)
        hg = h2[:, :hid]
        h = (hg * jax.nn.sigmoid(hg) * h2[:, hid:]).astype(BF16)
        y_s[slot] = _pack_bf16_pair(jnp.dot(h, wd_s[...], preferred_element_type=F32))
        for cp in tile_copies(False, i, slot):
            cp.start()

    @pl.when(i == n_items - 1)
    def _():
        total = gt_ref[0]

        @pl.when(total >= 2)
        def _():
            for cp in tile_copies(False, 0, total % 2):
                cp.wait()

        for cp in tile_copies(False, 0, (total + 1) % 2):
            cp.wait()


def _gmm_metadata(counts, cap, tm, n_items):
    e = counts.shape[0]
    tiles = (counts + tm - 1) // tm
    item_end = jnp.cumsum(tiles)
    total = item_end[-1]
    it = jnp.arange(n_items, dtype=I32)
    itc = jnp.minimum(it, total - 1)
    ge = jnp.sum((item_end[None, :] <= itc[:, None]).astype(I32), axis=1)
    onehot = ge[:, None] == jnp.arange(e, dtype=I32)[None, :]
    start = jnp.sum(jnp.where(onehot, (item_end - tiles)[None, :], 0), axis=1)
    cnt = jnp.sum(jnp.where(onehot, counts[None, :], 0), axis=1)
    j = itc - start
    gr = ge * (cap // tm) + j
    gn = jnp.where(it < total, jnp.clip(cnt - j * tm, 0, tm), 0)
    return ge.astype(I32), gr.astype(I32), gn.astype(I32), total.reshape(1).astype(I32)


def _gmm(xs, counts, cap, n_tokens, wg, wu, wd):
    tm = GMM_ROWS
    e, d, hid = wg.shape
    n_items = (n_tokens * TOP_K) // tm + e
    meta = _gmm_metadata(counts, cap, tm, n_items)
    body = functools.partial(_gmm_body, tm=tm, n_items=n_items)
    grid_spec = pltpu.PrefetchScalarGridSpec(
        num_scalar_prefetch=4,
        grid=(n_items,),
        in_specs=[pl.BlockSpec(memory_space=pl.ANY),
                  pl.BlockSpec((1, d, hid), lambda i, ge, gr, gn, gt: (ge[i], 0, 0)),
                  pl.BlockSpec((1, d, hid), lambda i, ge, gr, gn, gt: (ge[i], 0, 0)),
                  pl.BlockSpec((1, hid, d), lambda i, ge, gr, gn, gt: (ge[i], 0, 0))],
        out_specs=pl.BlockSpec(memory_space=pl.ANY),
        scratch_shapes=[pltpu.VMEM((d, 2 * hid), BF16), pltpu.VMEM((hid, d), BF16),
                        pltpu.VMEM((2, tm, d // 2), U32), pltpu.VMEM((2, tm, d // 2), U32),
                        pltpu.SemaphoreType.DMA((2,)), pltpu.SemaphoreType.DMA((2,))],
    )
    return pl.pallas_call(
        body,
        grid_spec=grid_spec,
        out_shape=jax.ShapeDtypeStruct(xs.shape, U32),
        compiler_params=_params(1),
        name="expert_gmm",
    )(*meta, xs, wg, wu, wd)


def _final_body(d0_ref, dn_ref, wts_ref, base_ref, p_ref, ys_ref, pg_ref, pb_ref, pp_ref, g2_ref, b2_ref,
                out_ref, rows_s, hi_s, lo_s, sems, *, bb, tm, n_steps):
    i = pl.program_id(0)
    m = bb * tm
    d = base_ref.shape[1]
    slot = i % 2

    def row_copy(sl, d_row, kk, n):
        return pltpu.make_async_copy(ys_ref.at[d_row], rows_s.at[sl, kk, n], sems.at[sl])

    def issue(sl, dref):
        def one(n, c):
            for kk in range(TOP_K):
                row_copy(sl, dref[0, kk, n], kk, n).start()
            return c
        lax.fori_loop(0, m, one, 0)

    @pl.when(i == 0)
    def _():
        issue(0, d0_ref)

    @pl.when(i + 1 < n_steps)
    def _():
        issue(1 - slot, dn_ref)

    def drain(n, c):
        for _ in range(DRAIN_UNROLL * TOP_K):
            row_copy(slot, 0, 0, 0).wait()
        return c

    lax.fori_loop(0, m // DRAIN_UNROLL, drain, 0)

    def combine(n, c):
        hi = jnp.zeros((ROW_SUBLANES, LANES), F32)
        lo = jnp.zeros((ROW_SUBLANES, LANES), F32)
        for kk in range(TOP_K):
            a, b = _unpack_bf16_pair(rows_s[slot, kk, n])
            wk = wts_ref[0, kk, n]
            hi = hi + wk * a
            lo = lo + wk * b
        hi_s[n] = hi
        lo_s[n] = lo
        return c

    lax.fori_loop(0, m, combine, 0, unroll=2)
    routed = jnp.concatenate([hi_s[:, sub, :] for sub in range(ROW_SUBLANES)]
                             + [lo_s[:, sub, :] for sub in range(ROW_SUBLANES)], axis=1)
    x2 = _layer_norm(base_ref[...] + routed, g2_ref[...], b2_ref[...])
    gate = jax.nn.sigmoid(jnp.dot(x2.astype(BF16), pg_ref[...], preferred_element_type=F32) + pb_ref[...])
    proj = jnp.dot(p_ref[...].reshape(m, p_ref.shape[-1]).astype(BF16), pp_ref[...], preferred_element_type=F32)
    out_ref[...] = (x2 + gate * proj).reshape(bb, tm, d)


def _final(base, dest, wts, p, ys, w, B, T):
    n, D = base.shape
    tm = min(MIX_ROWS, T)
    bb = min(B, MIX_ROWS // tm)
    m = bb * tm
    nt = T // tm
    n_steps = (B // bb) * nt
    pd = p.shape[-1]
    body = functools.partial(_final_body, bb=bb, tm=tm, n_steps=n_steps)
    return pl.pallas_call(
        body,
        grid=(n_steps,),
        in_specs=[pl.BlockSpec((1, TOP_K, m), lambda i: (0, 0, 0), memory_space=pltpu.SMEM),
                  pl.BlockSpec((1, TOP_K, m), lambda i: (jnp.minimum(i + 1, n_steps - 1), 0, 0),
                               memory_space=pltpu.SMEM),
                  pl.BlockSpec((1, TOP_K, m), lambda i: (i, 0, 0), memory_space=pltpu.SMEM),
                  pl.BlockSpec((m, D), lambda i: (i, 0)),
                  pl.BlockSpec((bb, tm, pd), lambda i: (i // nt, i % nt, 0)),
                  pl.BlockSpec(memory_space=pl.ANY),
                  _const_spec((D, D)),
                  _const_spec((1, D)),
                  _const_spec((pd, D)),
                  _const_spec((1, D)),
                  _const_spec((1, D))],
        out_specs=pl.BlockSpec((bb, tm, D), lambda i: (i // nt, i % nt, 0)),
        out_shape=jax.ShapeDtypeStruct((B, T, D), F32),
        scratch_shapes=[pltpu.VMEM((2, TOP_K, m, ROW_SUBLANES, LANES), U32),
                        pltpu.VMEM((m, ROW_SUBLANES, LANES), F32),
                        pltpu.VMEM((m, ROW_SUBLANES, LANES), F32),
                        pltpu.SemaphoreType.DMA((2,))],
        compiler_params=_params(1),
        name="combine_final",
    )(dest, dest, wts, base, p, ys, w["pg"], w["pb"], w["pp"], w["g2"], w["b2"])


def _prep_layer(prm):
    (w_in, conv_w, conv_b, lru_wa, lru_ba, lru_wx, lru_bx, lru_lambda, ret_gn, w_out, ln1_g, ln1_b,
     router_w, router_b, e_gate, e_up, e_down, s_gate, s_up, s_down, ln2_g, ln2_b,
     ple_w_proj, ple_w_gate, ple_b_gate) = prm
    row = lambda v: v.reshape(1, -1)
    rw2 = jnp.concatenate([router_w, router_w], axis=1)
    rwh = rw2.astype(BF16)
    rwl = (rw2 - rwh.astype(F32)).astype(BF16)
    return dict(
        w_in=w_in.astype(BF16), conv_w=conv_w, conv_b=row(conv_b), wa=lru_wa.astype(BF16), ba=row(lru_ba),
        wx=lru_wx.astype(BF16), bx=row(lru_bx), lam=row(lru_lambda), gn=row(ret_gn), wo=w_out.astype(BF16),
        g1=row(ln1_g), b1=row(ln1_b), rwh=rwh, rwl=rwl, rb=row(jnp.concatenate([router_b, router_b])),
        e_gate=e_gate, e_up=e_up, e_down=e_down, sg=s_gate.astype(BF16), su=s_up.astype(BF16),
        sd=s_down.astype(BF16), g2=row(ln2_g), b2=row(ln2_b), pp=ple_w_proj.astype(BF16),
        pg=ple_w_gate.astype(BF16), pb=row(ple_b_gate))


def _mixers(x, conv_st, lru_st, ret_st, pos0, chunk, w):
    B, T, D = x.shape
    W = conv_st.shape[-1]
    z_lru, z_ret = _inproj(x, w["w_in"], 2 * W)
    y_lru, conv_tm, new_lru = _lru(z_lru.reshape(T, B, 2 * W), jnp.transpose(conv_st, (1, 0, 2)), lru_st,
                                   w["conv_w"], w["conv_b"], w["wa"], w["ba"], w["wx"], w["bx"], w["lam"])
    y_ret, new_ret = _retention(z_ret, ret_st, w["gn"], pos0, chunk)
    return y_lru.reshape(T, B * W), y_ret, jnp.transpose(conv_tm, (1, 0, 2)), new_lru, new_ret


def kernel(x_prompt, x_sample, p_prompt, p_sample, state_conv, state_lru, state_ret, w_in, conv_w, conv_b,
           lru_wa, lru_ba, lru_wx, lru_bx, lru_lambda, ret_gn, w_out, ln1_g, ln1_b, router_w, router_b,
           exp_w_gate, exp_w_up, exp_w_down, sh_w_gate, sh_w_up, sh_w_down, ln2_g, ln2_b,
           ple_w_proj, ple_w_gate, ple_b_gate):
    depth = w_in.shape[0]
    alpha = (2 * depth) ** 0.25
    b_p, t_p, _ = x_prompt.shape
    b_s, t_s, _ = x_sample.shape
    W = state_conv.shape[-1]
    H, dh = state_ret.shape[2], state_ret.shape[3]
    n_tokens = b_p * t_p + b_s * t_s
    cap = -(-n_tokens // GMM_ROWS) * GMM_ROWS
    hp, hs = x_prompt, x_sample
    outs = [[] for _ in range(6)]
    for i in range(depth):
        prm = (w_in[i], conv_w[i], conv_b[i], lru_wa[i], lru_ba[i], lru_wx[i], lru_bx[i], lru_lambda[i],
               ret_gn[i], w_out[i], ln1_g[i], ln1_b[i], router_w[i], router_b[i], exp_w_gate[i], exp_w_up[i],
               exp_w_down[i], sh_w_gate[i], sh_w_up[i], sh_w_down[i], ln2_g[i], ln2_b[i],
               ple_w_proj[i], ple_w_gate[i], ple_b_gate[i])
        w = _prep_layer(prm)
        zc = jnp.zeros((b_p, CONV_WIDTH - 1, W), x_prompt.dtype)
        zl = jnp.zeros((b_p, W), F32)
        zr = jnp.zeros((b_p, H, dh, dh), F32)
        yl_p, yr_p, c_p, l_p, r_p = _mixers(hp, zc, zl, zr, 0, CHUNK, w)
        yl_s, yr_s, c_s, l_s, r_s = _mixers(hs, state_conv[i], state_lru[i], state_ret[i], PAST_LEN, t_s, w)
        for o, val in zip(outs, (c_p, l_p, r_p, c_s, l_s, r_s)):
            o.append(val)
        base_p, dest_p, wts_p, cnt_p, xs = _mix(yl_p, yr_p, hp, w, alpha, jnp.zeros((1, LANES), I32), None, cap)
        base_s, dest_s, wts_s, cnt_all, xs = _mix(yl_s, yr_s, hs, w, alpha, cnt_p, xs, cap)
        ys = _gmm(xs, cnt_all[0, :N_EXPERTS], cap, n_tokens, w["e_gate"], w["e_up"], w["e_down"])
        hp = _final(base_p, dest_p, wts_p, p_prompt[i], ys, w, b_p, t_p)
        hs = _final(base_s, dest_s, wts_s, p_sample[i], ys, w, b_s, t_s)
    return (hp, hs) + tuple(jnp.stack(o) for o in outs)
```

```python
import functools

import jax
import jax.numpy as jnp
from jax import lax
from jax.experimental import pallas as pl
from jax.experimental.pallas import tpu as pltpu

F32 = jnp.float32
BF16 = jnp.bfloat16
U32 = jnp.uint32
I32 = jnp.int32

CHUNK = 64
PAST_LEN = 1024
CONV_WIDTH = 4
LRU_C = 8.0
LRU_BLOCKS = 8
RET_HEADS = 8
ROPE_BASE = 10000.0
N_EXPERTS = 64
TOP_K = 8
N_GROUPS = 8
TOPK_GROUPS = 4
ROUTED_SCALE = 2.5
LN_EPS = 1e-5
GN_EPS = 1e-6

LANES = 128
ROW_SUBLANES = 8
DRAIN_UNROLL = 8
ROWS_PER_STEP = 512
MIX_ROWS = 256
GMM_ROWS = 512
RET_GROUP_ROWS = 256
LRU_TIME_TILE = 64
VMEM_LIMIT = 56 * 1024 * 1024


def _const_spec(shape):
    zeros = (0,) * len(shape)
    return pl.BlockSpec(shape, lambda *_: zeros, pipeline_mode=pl.Buffered(1))


def _params(n_axes):
    return pltpu.CompilerParams(dimension_semantics=("arbitrary",) * n_axes,
                                vmem_limit_bytes=VMEM_LIMIT)


def _layer_norm(x, g, b):
    mu = jnp.mean(x, axis=-1, keepdims=True)
    xc = x - mu
    var = jnp.mean(xc * xc, axis=-1, keepdims=True)
    return xc * lax.rsqrt(var + LN_EPS) * g + b


def _pack_bf16_pair(x):
    c = x.shape[1] // 2
    xb = x.astype(BF16).astype(F32)
    hi = pltpu.bitcast(xb[:, :c], U32)
    lo = pltpu.bitcast(xb[:, c:], U32)
    return hi | (lo >> 16)


def _unpack_bf16_pair(pk):
    hi = pltpu.bitcast(pk & jnp.uint32(0xFFFF0000), F32)
    lo = pltpu.bitcast(pk << 16, F32)
    return hi, lo


def _inproj_body(x_ref, w_ref, zl_ref, zr_ref, *, bb, tm, lru_cols, tn):
    d = x_ref.shape[-1]
    x = x_ref[...].reshape(bb * tm, d).astype(BF16)
    for j in range(w_ref.shape[1] // tn):
        c0 = j * tn
        acc = jnp.dot(x, w_ref[:, c0:c0 + tn], preferred_element_type=F32)
        for b in range(bb):
            rows = acc[b * tm:(b + 1) * tm]
            if c0 < lru_cols:
                zl_ref[:, b * lru_cols + c0:b * lru_cols + c0 + tn] = rows
            else:
                zr_ref[b, :, c0 - lru_cols:c0 - lru_cols + tn] = rows.astype(BF16)


def _inproj(x, w_in_b, lru_cols):
    B, T, D = x.shape
    n_cols = w_in_b.shape[1]
    ret_cols = n_cols - lru_cols
    tm = min(ROWS_PER_STEP, T)
    bb = min(B, ROWS_PER_STEP // tm)
    body = functools.partial(_inproj_body, bb=bb, tm=tm, lru_cols=lru_cols, tn=512)
    return pl.pallas_call(
        body,
        grid=(B // bb, T // tm),
        in_specs=[pl.BlockSpec((bb, tm, D), lambda b, t: (b, t, 0)),
                  _const_spec((D, n_cols))],
        out_specs=[pl.BlockSpec((tm, bb * lru_cols), lambda b, t: (t, b)),
                   pl.BlockSpec((bb, tm, ret_cols), lambda b, t: (b, t, 0))],
        out_shape=[jax.ShapeDtypeStruct((T, B * lru_cols), F32),
                   jax.ShapeDtypeStruct((B, T, ret_cols), BF16)],
        compiler_params=_params(2),
        name="inproj",
    )(x, w_in_b)


def _lru_body(xl_ref, gl_ref, conv0_ref, h0_ref, cw_ref, cb_ref, wa_ref, ba_ref, wx_ref, bx_ref, lam_ref,
              y_ref, conv_out_ref, h_out_ref, xp_s, a_s, b_s, h_s, *, tt, rows):
    i = pl.program_id(0)
    B, W = h0_ref.shape
    nblk = wa_ref.shape[0]
    blk = W // nblk

    @pl.when(i == 0)
    def _():
        xp_s[0:CONV_WIDTH - 1] = conv0_ref[...]
        h_s[...] = h0_ref[...]

    xp_s[CONV_WIDTH - 1:] = xl_ref[...]

    lam = lam_ref[...]
    neg = -lam
    softplus = jnp.maximum(neg, 0.0) + jnp.log1p(jnp.exp(-jnp.abs(neg)))
    decay = (-LRU_C) * softplus

    def gates(c, carry):
        t0 = pl.multiple_of(c * rows, rows)
        xc = cb_ref[...].reshape(1, 1, W)
        for j in range(CONV_WIDTH):
            xc = xc + xp_s[pl.ds(t0 + j, rows)] * cw_ref[j:j + 1].reshape(1, 1, W)
        xc2 = xc.reshape(rows * B, W)
        xcb = xc2.astype(BF16)
        r_parts, i_parts = [], []
        for n in range(nblk):
            xb = xcb[:, n * blk:(n + 1) * blk]
            r_parts.append(jnp.dot(xb, wa_ref[n], preferred_element_type=F32))
            i_parts.append(jnp.dot(xb, wx_ref[n], preferred_element_type=F32))
        r = jax.nn.sigmoid(jnp.concatenate(r_parts, axis=1) + ba_ref[...])
        ig = jax.nn.sigmoid(jnp.concatenate(i_parts, axis=1) + bx_ref[...])
        a = jnp.exp(decay * r)
        bterm = jnp.sqrt(1.0 - a * a) * (ig * xc2)
        a_s[pl.ds(t0, rows)] = a.reshape(rows, B, W)
        b_s[pl.ds(t0, rows)] = bterm.reshape(rows, B, W)
        return carry

    lax.fori_loop(0, tt // rows, gates, 0)

    def step(t, h):
        hn = a_s[t] * h + b_s[t]
        y_ref[t] = hn * jax.nn.gelu(gl_ref[t])
        return hn

    h_last = lax.fori_loop(0, tt, step, h_s[...], unroll=8)
    h_s[...] = h_last
    tail = xp_s[tt:tt + CONV_WIDTH - 1]
    xp_s[0:CONV_WIDTH - 1] = tail
    conv_out_ref[...] = tail
    h_out_ref[...] = h_last


def _lru(z_lru3, conv0_tm, h0, conv_w, conv_b, wa_b, ba, wx_b, bx, lam):
    T, B, W2 = z_lru3.shape
    W = W2 // 2
    tt = min(LRU_TIME_TILE, T)
    rows = max(1, min(tt, 128 // B))
    body = functools.partial(_lru_body, tt=tt, rows=rows)
    nb = wa_b.shape[0]
    blk = W // nb
    return pl.pallas_call(
        body,
        grid=(T // tt,),
        in_specs=[pl.BlockSpec((tt, B, W), lambda t: (t, 0, 0)),
                  pl.BlockSpec((tt, B, W), lambda t: (t, 0, 1)),
                  _const_spec((CONV_WIDTH - 1, B, W)),
                  _const_spec((B, W)),
                  _const_spec((CONV_WIDTH, W)),
                  _const_spec((1, W)),
                  _const_spec((nb, blk, blk)),
                  _const_spec((1, W)),
                  _const_spec((nb, blk, blk)),
                  _const_spec((1, W)),
                  _const_spec((1, W))],
        out_specs=[pl.BlockSpec((tt, B, W), lambda t: (t, 0, 0)),
                   pl.BlockSpec((CONV_WIDTH - 1, B, W), lambda t: (0, 0, 0)),
                   pl.BlockSpec((B, W), lambda t: (0, 0))],
        out_shape=[jax.ShapeDtypeStruct((T, B, W), F32),
                   jax.ShapeDtypeStruct((CONV_WIDTH - 1, B, W), F32),
                   jax.ShapeDtypeStruct((B, W), F32)],
        scratch_shapes=[pltpu.VMEM((tt + CONV_WIDTH - 1, B, W), F32),
                        pltpu.VMEM((tt, B, W), F32),
                        pltpu.VMEM((tt, B, W), F32),
                        pltpu.VMEM((B, W), F32)],
        compiler_params=_params(1),
        name="rglru",
    )(z_lru3, z_lru3, conv0_tm, h0, conv_w, conv_b, wa_b, ba, wx_b, bx, lam)


def _ret_body(q_ref, k_ref, v_ref, g_ref, cos_ref, sin_ref, mask_ref, qdec_ref, kdec_ref, cdec_ref, gn_ref,
              s0_ref, y_ref, s_out_ref, *, rg, hb, dh):
    T = q_ref.shape[1]
    scale = dh ** -0.5

    def rope(t, cos, sin):
        return t * cos + pltpu.roll(t, dh // 2, axis=1) * sin

    for hh in range(hb):
        cols = slice(hh * dh, (hh + 1) * dh)

        def group(c, s, hh=hh, cols=cols):
            r0 = pl.multiple_of(c * rg, rg)
            rws = pl.ds(r0, rg)
            cos = cos_ref[rws, :]
            sin = sin_ref[rws, :]
            q = rope(q_ref[0, rws, cols].astype(F32), cos, sin)
            k = rope(k_ref[0, rws, cols].astype(F32), cos, sin) * scale
            v = v_ref[0, rws, cols]
            scores = lax.dot_general(q.astype(BF16), k.astype(BF16), (((1,), (1,)), ((), ())),
                                     preferred_element_type=F32)
            scores = scores * mask_ref[hh]
            o = jnp.dot(scores.astype(BF16), v, preferred_element_type=F32)
            o = o + jnp.dot((q * qdec_ref[hh]).astype(BF16), s.astype(BF16), preferred_element_type=F32)
            kd = (k * kdec_ref[hh]).astype(BF16)
            kv = lax.dot_general(kd, v, (((0,), (0,)), ((), ())), preferred_element_type=F32)
            s_new = cdec_ref[hh] * s + kv
            mu = jnp.mean(o, axis=-1, keepdims=True)
            oc = o - mu
            var = jnp.mean(oc * oc, axis=-1, keepdims=True)
            on = oc * lax.rsqrt(var + GN_EPS) * gn_ref[:, cols]
            g = g_ref[0, rws, cols].astype(F32)
            y_ref[0, rws, cols] = (g * jax.nn.sigmoid(g) * on).astype(y_ref.dtype)
            return s_new

        n_groups = T // rg
        s_out_ref[0, hh] = lax.fori_loop(0, n_groups, group, s0_ref[0, hh],
                                         unroll=2 if n_groups % 2 == 0 else 1)


def _retention_tables(T, pos0, chunk, rg, dh):
    half = dh // 2
    inv = ROPE_BASE ** (-jnp.arange(half, dtype=F32) / half)
    pos = pos0 + jnp.arange(T)
    ang = pos.astype(F32)[:, None] * inv[None, :]
    cos, sin = jnp.cos(ang), jnp.sin(ang)
    cos2 = jnp.concatenate([cos, cos], axis=1)
    sin2 = jnp.concatenate([-sin, sin], axis=1)
    log_g = jnp.log1p(-jnp.exp2(-5.0 - jnp.arange(RET_HEADS, dtype=F32)))[:, None, None]
    idx = jnp.arange(rg, dtype=F32)
    ci = jnp.floor(idx / chunk)
    diff = idx[:, None] - idx[None, :]
    same = ci[:, None] == ci[None, :]
    earlier = ci[None, :] < ci[:, None]
    dist = jnp.where(same, jnp.abs(diff), diff)
    mask = jnp.where(same | earlier, jnp.exp(dist[None] * log_g), 0.0)
    ones = jnp.ones((1, 1, dh), F32)
    qdec = jnp.exp((idx + 1.0)[None, :, None] * log_g) * ones
    kdec = jnp.exp((rg - 1.0 - idx)[None, :, None] * log_g) * ones
    cdec = jnp.exp(rg * log_g) * ones
    return cos2, sin2, mask, qdec, kdec, cdec


def _retention(z_ret, s0, gn, pos0, chunk):
    B, T, C4 = z_ret.shape
    H = RET_HEADS
    dh = C4 // (4 * H)
    rg = min(T, max(chunk, (RET_GROUP_ROWS // chunk) * chunk))
    hb = H if T * H * dh <= 64 * 1024 else 1
    nh = H // hb
    cos2, sin2, mask, qdec, kdec, cdec = _retention_tables(T, pos0, chunk, rg, dh)
    body = functools.partial(_ret_body, rg=rg, hb=hb, dh=dh)
    col = lambda off: (lambda b, h: (b, 0, off * nh + h))
    return pl.pallas_call(
        body,
        grid=(B, nh),
        in_specs=[pl.BlockSpec((1, T, hb * dh), col(0)),
                  pl.BlockSpec((1, T, hb * dh), col(1)),
                  pl.BlockSpec((1, T, hb * dh), col(2)),
                  pl.BlockSpec((1, T, hb * dh), col(3)),
                  _const_spec((T, dh)),
                  _const_spec((T, dh)),
                  pl.BlockSpec((hb, rg, rg), lambda b, h: (h, 0, 0)),
                  pl.BlockSpec((hb, rg, dh), lambda b, h: (h, 0, 0)),
                  pl.BlockSpec((hb, rg, dh), lambda b, h: (h, 0, 0)),
                  pl.BlockSpec((hb, 1, dh), lambda b, h: (h, 0, 0)),
                  pl.BlockSpec((1, hb * dh), lambda b, h: (0, h)),
                  pl.BlockSpec((1, hb, dh, dh), lambda b, h: (b, h, 0, 0))],
        out_specs=[pl.BlockSpec((1, T, hb * dh), lambda b, h: (b, 0, h)),
                   pl.BlockSpec((1, hb, dh, dh), lambda b, h: (b, h, 0, 0))],
        out_shape=[jax.ShapeDtypeStruct((B, T, H * dh), BF16),
                   jax.ShapeDtypeStruct((B, H, dh, dh), F32)],
        compiler_params=_params(2),
        name="retention",
    )(z_ret, z_ret, z_ret, z_ret, cos2, sin2, mask, qdec, kdec, cdec, gn, s0)


def _seg_allreduce(v, lane, op):
    for s in (1, 2, 4):
        up = pltpu.roll(v, LANES - s, axis=1)
        dn = pltpu.roll(v, s, axis=1)
        v = op(v, jnp.where((lane & s) == 0, up, dn))
    return v


def _mix_body(*refs, bb, tm, alpha, cap, n_steps, aliased):
    (yl_ref, yr_ref, x_ref, wo_ref, g1_ref, b1_ref, rwh_ref, rwl_ref, rb_ref, sg_ref, su_ref, sd_ref,
     tri_ref, cnt_in_ref) = refs[:14]
    refs = refs[15:] if aliased else refs[14:]
    base_ref, dest_ref, wts_ref, cnt_ref, xs_ref, carry_s, xpk_s, dv_s, ds_s, row_sems, idx_sem = refs
    i = pl.program_id(0)
    m = bb * tm
    d = x_ref.shape[-1]
    w = yl_ref.shape[1] // bb
    slot = i % 2

    @pl.when(i == 0)
    def _():
        carry_s[...] = jnp.zeros_like(carry_s)
        carry_s[0:1, :] = cnt_in_ref[...].astype(F32)

    yl = jnp.concatenate([yl_ref[:, b * w:(b + 1) * w] for b in range(bb)], axis=0).astype(BF16)
    yr = yr_ref[...].reshape(m, yr_ref.shape[-1])
    mix = jnp.dot(yl, wo_ref[:w], preferred_element_type=F32)
    mix = mix + jnp.dot(yr, wo_ref[w:], preferred_element_type=F32)
    x1 = _layer_norm(alpha * x_ref[...].reshape(m, d) + mix, g1_ref[...], b1_ref[...])
    x1b = x1.astype(BF16)

    x1l = (x1 - x1b.astype(F32)).astype(BF16)
    logits = jnp.dot(x1b, rwh_ref[...], preferred_element_type=F32)
    logits = logits + jnp.dot(x1l, rwh_ref[...], preferred_element_type=F32)
    logits = logits + jnp.dot(x1b, rwl_ref[...], preferred_element_type=F32)
    s = jax.nn.sigmoid(logits)
    sb = s + rb_ref[...]
    lane = lax.broadcasted_iota(I32, (m, LANES), 1)
    e_id = lane & (N_EXPERTS - 1)
    e_f = e_id.astype(F32)
    low = lane < N_EXPERTS
    e_low = jnp.where(low, e_f, -1.0)
    grp = e_id >> 3
    big = jnp.float32(1e9)
    ninf = jnp.float32(-jnp.inf)

    m1 = _seg_allreduce(sb, lane, jnp.maximum)
    first_max = _seg_allreduce(jnp.where(sb == m1, e_f, big), lane, jnp.minimum)
    m2 = _seg_allreduce(jnp.where(e_f == first_max, ninf, sb), lane, jnp.maximum)
    gs = m1 + m2
    rank = jnp.zeros((m, LANES), F32)
    for dgrp in range(1, N_GROUPS):
        other = pltpu.roll(gs, 8 * dgrp, axis=1)
        tie = jnp.where(grp >= dgrp, 1.0, 0.0)
        rank = rank + jnp.where(other > gs, 1.0, jnp.where(other == gs, tie, 0.0))
    v = jnp.where(rank < TOPK_GROUPS, jnp.where(low, sb, ninf), ninf)

    idx_cols, w_cols = [], []
    sel = jnp.zeros((m, LANES), F32)
    for _ in range(TOP_K):
        mx = jnp.max(v, axis=1, keepdims=True)
        idx = jnp.min(jnp.where(v == mx, e_f, big), axis=1, keepdims=True)
        hit = e_low == idx
        w_cols.append(jnp.sum(jnp.where(hit, s, 0.0), axis=1, keepdims=True))
        idx_cols.append(idx)
        v = jnp.where(hit, ninf, v)
        sel = jnp.where(hit, 1.0, sel)

    cum = jnp.dot(tri_ref[...], sel.astype(BF16), preferred_element_type=F32) + carry_s[0:1, :]
    carry_s[0:1, :] = carry_s[0:1, :] + jnp.sum(sel, axis=0, keepdims=True)
    cnt_ref[...] = carry_s[0:1, :].astype(I32)

    wsum = w_cols[0]
    for c in w_cols[1:]:
        wsum = wsum + c
    d_out = jnp.zeros((m, LANES), F32)
    w_out = jnp.zeros((m, LANES), F32)
    for kk in range(TOP_K):
        hit = e_low == idx_cols[kk]
        pk = jnp.sum(jnp.where(hit, cum, 0.0), axis=1, keepdims=True)
        d_out = jnp.where(lane == kk, idx_cols[kk] * float(cap) + pk, d_out)
        w_out = jnp.where(lane == kk, w_cols[kk] / wsum * ROUTED_SCALE, w_out)
    wts_ref[...] = w_out[:, :TOP_K]
    dest_t = jnp.transpose(d_out)[:TOP_K].astype(I32)
    dest_ref[0] = dest_t
    dv_s[...] = dest_t
    to_smem = pltpu.make_async_copy(dv_s, ds_s, idx_sem)
    to_smem.start()

    hg = jnp.dot(x1b, sg_ref[...], preferred_element_type=F32)
    hu = jnp.dot(x1b, su_ref[...], preferred_element_type=F32)
    hs = (hg * jax.nn.sigmoid(hg) * hu).astype(BF16)
    base_ref[...] = alpha * x1 + jnp.dot(hs, sd_ref[...], preferred_element_type=F32)
    pk = _pack_bf16_pair(x1)
    for sub in range(ROW_SUBLANES):
        xpk_s[slot, :, sub, :] = pk[:, sub * LANES:(sub + 1) * LANES]
    to_smem.wait()

    def row_copy(sl, n, dst):
        return pltpu.make_async_copy(xpk_s.at[sl, n], xs_ref.at[dst], row_sems.at[sl])

    def issue(n, c):
        for kk in range(TOP_K):
            row_copy(slot, n, ds_s[kk, n]).start()
        return c

    lax.fori_loop(0, m, issue, 0)

    def drain(sl):
        def one(n, c):
            for _ in range(DRAIN_UNROLL * TOP_K):
                row_copy(sl, 0, 0).wait()
            return c
        lax.fori_loop(0, m // DRAIN_UNROLL, one, 0)

    @pl.when(i > 0)
    def _():
        drain(1 - slot)

    @pl.when(i == n_steps - 1)
    def _():
        drain(slot)


def _mix(y_lru2, y_ret, x, w, alpha, cnt_in, xs, cap):
    B, T, D = x.shape
    assert D == 2 * ROW_SUBLANES * LANES, "a packed token row must fill exactly one (8, 128) tile"
    W = y_ret.shape[-1]
    tm = min(MIX_ROWS, T)
    bb = min(B, MIX_ROWS // tm)
    m = bb * tm
    n = B * T
    nt = T // tm
    n_steps = (B // bb) * nt
    hs = w["sg"].shape[1]
    aliased = xs is not None
    tri = (lax.broadcasted_iota(I32, (m, m), 1) < lax.broadcasted_iota(I32, (m, m), 0)).astype(BF16)
    body = functools.partial(_mix_body, bb=bb, tm=tm, alpha=alpha, cap=cap, n_steps=n_steps, aliased=aliased)
    in_specs = [pl.BlockSpec((tm, bb * W), lambda i: (i % nt, i // nt)),
                pl.BlockSpec((bb, tm, W), lambda i: (i // nt, i % nt, 0)),
                pl.BlockSpec((bb, tm, D), lambda i: (i // nt, i % nt, 0)),
                _const_spec((2 * W, D)),
                _const_spec((1, D)),
                _const_spec((1, D)),
                _const_spec((D, LANES)),
                _const_spec((D, LANES)),
                _const_spec((1, LANES)),
                _const_spec((D, hs)),
                _const_spec((D, hs)),
                _const_spec((hs, D)),
                _const_spec((m, m)),
                _const_spec((1, LANES))]
    args = [y_lru2, y_ret, x, w["wo"], w["g1"], w["b1"], w["rwh"], w["rwl"], w["rb"], w["sg"], w["su"], w["sd"],
            tri, cnt_in]
    if aliased:
        in_specs.append(pl.BlockSpec(memory_space=pl.ANY))
        args.append(xs)
    return pl.pallas_call(
        body,
        grid=(n_steps,),
        in_specs=in_specs,
        out_specs=[pl.BlockSpec((m, D), lambda i: (i, 0)),
                   pl.BlockSpec((1, TOP_K, m), lambda i: (i, 0, 0)),
                   pl.BlockSpec((m, TOP_K), lambda i: (i, 0)),
                   pl.BlockSpec((1, LANES), lambda i: (0, 0)),
                   pl.BlockSpec(memory_space=pl.ANY)],
        out_shape=[jax.ShapeDtypeStruct((n, D), F32),
                   jax.ShapeDtypeStruct((n_steps, TOP_K, m), I32),
                   jax.ShapeDtypeStruct((n, TOP_K), F32),
                   jax.ShapeDtypeStruct((1, LANES), I32),
                   jax.ShapeDtypeStruct((N_EXPERTS * cap, ROW_SUBLANES, LANES), U32)],
        scratch_shapes=[pltpu.VMEM((8, LANES), F32),
                        pltpu.VMEM((2, m, ROW_SUBLANES, LANES), U32),
                        pltpu.VMEM((TOP_K, m), I32),
                        pltpu.SMEM((TOP_K, m), I32),
                        pltpu.SemaphoreType.DMA((2,)),
                        pltpu.SemaphoreType.DMA(())],
        input_output_aliases={14: 4} if aliased else {},
        compiler_params=_params(1),
        name="mix_router",
    )(*args)


def _gmm_body(ge_ref, gr_ref, gn_ref, gt_ref, gx_ref, gs_ref, xs_ref, wg_ref, wu_ref, wd_ref, ys_ref,
              wgf_s, wuf_s, wdf_s, wgu_s, wd_s, x_s, y_s, w_sems, in_sems, out_sems, *, tm, n_items):
    i = pl.program_id(0)
    hid = wg_ref.shape[2]
    slot = i % 2

    def weight_copies(expert, sl):
        return [pltpu.make_async_copy(wg_ref.at[expert], wgf_s.at[sl], w_sems.at[sl]),
                pltpu.make_async_copy(wu_ref.at[expert], wuf_s.at[sl], w_sems.at[sl]),
                pltpu.make_async_copy(wd_ref.at[expert], wdf_s.at[sl], w_sems.at[sl])]

    def tile_copies(to_vmem, item, sl):
        r0 = pl.multiple_of(gr_ref[item] * tm, tm)
        out = []
        for sub in range(ROW_SUBLANES):
            cols = pl.ds(sub * LANES, LANES)
            if to_vmem:
                out.append(pltpu.make_async_copy(xs_ref.at[pl.ds(r0, tm), sub, :], x_s.at[sl, :, cols],
                                                 in_sems.at[sl]))
            else:
                out.append(pltpu.make_async_copy(y_s.at[sl, :, cols], ys_ref.at[pl.ds(r0, tm), sub, :],
                                                 out_sems.at[sl]))
        return out

    @pl.when(i == 0)
    def _():
        for cp in weight_copies(ge_ref[0], 0):
            cp.start()
        for cp in tile_copies(True, 0, 0):
            cp.start()
        y_s[...] = jnp.zeros_like(y_s)

    nxt = jnp.minimum(i + 1, n_items - 1)

    @pl.when((i + 1 < n_items) & (gn_ref[nxt] > 0))
    def _():
        for cp in tile_copies(True, nxt, 1 - slot):
            cp.start()

    e = ge_ref[i]
    e_prev = ge_ref[jnp.maximum(i - 1, 0)]

    @pl.when((i == 0) | (e != e_prev))
    def _():
        wsl = gs_ref[i]
        nxt_e = gx_ref[i]

        @pl.when(nxt_e >= 0)
        def _():
            for cp in weight_copies(nxt_e, 1 - wsl):
                cp.start()

        for cp in weight_copies(e, wsl):
            cp.wait()
        wgu_s[:, :hid] = wgf_s[wsl].astype(BF16)
        wgu_s[:, hid:] = wuf_s[wsl].astype(BF16)
        wd_s[...] = wdf_s[wsl].astype(BF16)

    n_valid = gn_ref[i]

    @pl.when(n_valid > 0)
    def _():
        for cp in tile_copies(True, i, slot):
            cp.wait()

        @pl.when(i >= 2)
        def _():
            for cp in tile_copies(False, i, slot):
                cp.wait()

        def expert_mlp(rows):
            pk = x_s[slot, :rows]
            valid = lax.broadcasted_iota(I32, pk.shape, 0) < n_valid
            xa, xb = _unpack_bf16_pair(jnp.where(valid, pk, jnp.uint32(0)))
            x = jnp.concatenate([xa.astype(BF16), xb.astype(BF16)], axis=1)
            h2 = jnp.dot(x, wgu_s[...], preferred_element_type=F32)
            hg = h2[:, :hid]
            h = (hg * jax.nn.sigmoid(hg) * h2[:, hid:]).astype(BF16)
            y_s[slot, :rows] = _pack_bf16_pair(jnp.dot(h, wd_s[...], preferred_element_type=F32))

        @pl.when(n_valid > tm // 2)
        def _():
            expert_mlp(tm)

        @pl.when(n_valid <= tm // 2)
        def _():
            expert_mlp(tm // 2)

        for cp in tile_copies(False, i, slot):
            cp.start()

    @pl.when(i == n_items - 1)
    def _():
        total = gt_ref[0]

        @pl.when(total >= 2)
        def _():
            for cp in tile_copies(False, 0, total % 2):
                cp.wait()

        for cp in tile_copies(False, 0, (total + 1) % 2):
            cp.wait()


def _gmm_metadata(counts, cap, tm, n_items):
    e = counts.shape[0]
    tiles = (counts + tm - 1) // tm
    item_end = jnp.cumsum(tiles)
    total = item_end[-1]
    it = jnp.arange(n_items, dtype=I32)
    itc = jnp.minimum(it, total - 1)
    ge = jnp.sum((item_end[None, :] <= itc[:, None]).astype(I32), axis=1)
    onehot = ge[:, None] == jnp.arange(e, dtype=I32)[None, :]
    start = jnp.sum(jnp.where(onehot, (item_end - tiles)[None, :], 0), axis=1)
    cnt = jnp.sum(jnp.where(onehot, counts[None, :], 0), axis=1)
    j = itc - start
    gr = ge * (cap // tm) + j
    gn = jnp.where(it < total, jnp.clip(cnt - j * tm, 0, tm), 0)
    ids = jnp.arange(e, dtype=I32)
    live = tiles > 0
    later = live[None, :] & (ids[None, :] > ids[:, None])
    next_e = jnp.min(jnp.where(later, ids[None, :], e), axis=1)
    next_e = jnp.where(next_e < e, next_e, -1)
    wslot = (jnp.cumsum(live.astype(I32)) - 1) % 2
    gx = jnp.sum(jnp.where(onehot, next_e[None, :], 0), axis=1)
    gs = jnp.sum(jnp.where(onehot, wslot[None, :], 0), axis=1)
    return tuple(a.astype(I32) for a in (ge, gr, gn, total.reshape(1), gx, gs))


def _gmm(xs, counts, cap, n_tokens, wg, wu, wd):
    tm = GMM_ROWS
    e, d, hid = wg.shape
    n_items = (n_tokens * TOP_K) // tm + e
    meta = _gmm_metadata(counts, cap, tm, n_items)
    body = functools.partial(_gmm_body, tm=tm, n_items=n_items)
    grid_spec = pltpu.PrefetchScalarGridSpec(
        num_scalar_prefetch=6,
        grid=(n_items,),
        in_specs=[pl.BlockSpec(memory_space=pl.ANY)] * 4,
        out_specs=pl.BlockSpec(memory_space=pl.ANY),
        scratch_shapes=[pltpu.VMEM((2, d, hid), F32), pltpu.VMEM((2, d, hid), F32),
                        pltpu.VMEM((2, hid, d), F32),
                        pltpu.VMEM((d, 2 * hid), BF16), pltpu.VMEM((hid, d), BF16),
                        pltpu.VMEM((2, tm, d // 2), U32), pltpu.VMEM((2, tm, d // 2), U32),
                        pltpu.SemaphoreType.DMA((2,)), pltpu.SemaphoreType.DMA((2,)),
                        pltpu.SemaphoreType.DMA((2,))],
    )
    return pl.pallas_call(
        body,
        grid_spec=grid_spec,
        out_shape=jax.ShapeDtypeStruct(xs.shape, U32),
        compiler_params=_params(1),
        name="expert_gmm",
    )(*meta, xs, wg, wu, wd)


def _final_body(d0_ref, dn_ref, wts_ref, base_ref, p_ref, ys_ref, pg_ref, pb_ref, pp_ref, g2_ref, b2_ref,
                out_ref, rows_s, hi_s, lo_s, wrep_s, sems, *, bb, tm, n_steps):
    i = pl.program_id(0)
    m = bb * tm
    d = base_ref.shape[1]
    slot = i % 2

    def row_copy(sl, d_row, kk, n):
        return pltpu.make_async_copy(ys_ref.at[d_row], rows_s.at[sl, kk, n], sems.at[sl])

    def issue(sl, dref):
        def one(n, c):
            for kk in range(TOP_K):
                row_copy(sl, dref[0, kk, n], kk, n).start()
            return c
        lax.fori_loop(0, m, one, 0)

    @pl.when(i == 0)
    def _():
        issue(0, d0_ref)

    @pl.when(i + 1 < n_steps)
    def _():
        issue(1 - slot, dn_ref)

    def drain(n, c):
        for _ in range(DRAIN_UNROLL * TOP_K):
            row_copy(slot, 0, 0, 0).wait()
        return c

    lax.fori_loop(0, m // DRAIN_UNROLL, drain, 0)

    wts = wts_ref[...]
    for kk in range(TOP_K):
        wrep_s[kk] = jnp.broadcast_to(wts[:, kk:kk + 1], (m, LANES))

    def combine(n, c):
        hi = jnp.zeros((ROW_SUBLANES, LANES), F32)
        lo = jnp.zeros((ROW_SUBLANES, LANES), F32)
        for kk in range(TOP_K):
            a, b = _unpack_bf16_pair(rows_s[slot, kk, n])
            wk = wrep_s[kk, pl.ds(n, 1), :]
            hi = hi + wk * a
            lo = lo + wk * b
        hi_s[n] = hi
        lo_s[n] = lo
        return c

    lax.fori_loop(0, m, combine, 0, unroll=2)
    routed = jnp.concatenate([hi_s[:, sub, :] for sub in range(ROW_SUBLANES)]
                             + [lo_s[:, sub, :] for sub in range(ROW_SUBLANES)], axis=1)
    x2 = _layer_norm(base_ref[...] + routed, g2_ref[...], b2_ref[...])
    gate = jax.nn.sigmoid(jnp.dot(x2.astype(BF16), pg_ref[...], preferred_element_type=F32) + pb_ref[...])
    proj = jnp.dot(p_ref[...].reshape(m, p_ref.shape[-1]).astype(BF16), pp_ref[...], preferred_element_type=F32)
    out_ref[...] = (x2 + gate * proj).reshape(bb, tm, d)


def _final(base, dest, wts, p, ys, w, B, T):
    n, D = base.shape
    tm = min(MIX_ROWS, T)
    bb = min(B, MIX_ROWS // tm)
    m = bb * tm
    nt = T // tm
    n_steps = (B // bb) * nt
    pd = p.shape[-1]
    body = functools.partial(_final_body, bb=bb, tm=tm, n_steps=n_steps)
    return pl.pallas_call(
        body,
        grid=(n_steps,),
        in_specs=[pl.BlockSpec((1, TOP_K, m), lambda i: (0, 0, 0), memory_space=pltpu.SMEM),
                  pl.BlockSpec((1, TOP_K, m), lambda i: (jnp.minimum(i + 1, n_steps - 1), 0, 0),
                               memory_space=pltpu.SMEM),
                  pl.BlockSpec((m, TOP_K), lambda i: (i, 0)),
                  pl.BlockSpec((m, D), lambda i: (i, 0)),
                  pl.BlockSpec((bb, tm, pd), lambda i: (i // nt, i % nt, 0)),
                  pl.BlockSpec(memory_space=pl.ANY),
                  _const_spec((D, D)),
                  _const_spec((1, D)),
                  _const_spec((pd, D)),
                  _const_spec((1, D)),
                  _const_spec((1, D))],
        out_specs=pl.BlockSpec((bb, tm, D), lambda i: (i // nt, i % nt, 0)),
        out_shape=jax.ShapeDtypeStruct((B, T, D), F32),
        scratch_shapes=[pltpu.VMEM((2, TOP_K, m, ROW_SUBLANES, LANES), U32),
                        pltpu.VMEM((m, ROW_SUBLANES, LANES), F32),
                        pltpu.VMEM((m, ROW_SUBLANES, LANES), F32),
                        pltpu.VMEM((TOP_K, m, LANES), F32),
                        pltpu.SemaphoreType.DMA((2,))],
        compiler_params=_params(1),
        name="combine_final",
    )(dest, dest, wts, base, p, ys, w["pg"], w["pb"], w["pp"], w["g2"], w["b2"])


def _prep_layer(prm):
    (w_in, conv_w, conv_b, lru_wa, lru_ba, lru_wx, lru_bx, lru_lambda, ret_gn, w_out, ln1_g, ln1_b,
     router_w, router_b, e_gate, e_up, e_down, s_gate, s_up, s_down, ln2_g, ln2_b,
     ple_w_proj, ple_w_gate, ple_b_gate) = prm
    row = lambda v: v.reshape(1, -1)
    rw2 = jnp.concatenate([router_w, router_w], axis=1)
    rwh = rw2.astype(BF16)
    rwl = (rw2 - rwh.astype(F32)).astype(BF16)
    return dict(
        w_in=w_in.astype(BF16), conv_w=conv_w, conv_b=row(conv_b), wa=lru_wa.astype(BF16), ba=row(lru_ba),
        wx=lru_wx.astype(BF16), bx=row(lru_bx), lam=row(lru_lambda), gn=row(ret_gn), wo=w_out.astype(BF16),
        g1=row(ln1_g), b1=row(ln1_b), rwh=rwh, rwl=rwl, rb=row(jnp.concatenate([router_b, router_b])),
        e_gate=e_gate, e_up=e_up, e_down=e_down, sg=s_gate.astype(BF16), su=s_up.astype(BF16),
        sd=s_down.astype(BF16), g2=row(ln2_g), b2=row(ln2_b), pp=ple_w_proj.astype(BF16),
        pg=ple_w_gate.astype(BF16), pb=row(ple_b_gate))


def _mixers(x, conv_st, lru_st, ret_st, pos0, chunk, w):
    B, T, D = x.shape
    W = conv_st.shape[-1]
    z_lru, z_ret = _inproj(x, w["w_in"], 2 * W)
    y_lru, conv_tm, new_lru = _lru(z_lru.reshape(T, B, 2 * W), jnp.transpose(conv_st, (1, 0, 2)), lru_st,
                                   w["conv_w"], w["conv_b"], w["wa"], w["ba"], w["wx"], w["bx"], w["lam"])
    y_ret, new_ret = _retention(z_ret, ret_st, w["gn"], pos0, chunk)
    return y_lru.reshape(T, B * W), y_ret, jnp.transpose(conv_tm, (1, 0, 2)), new_lru, new_ret


def kernel(x_prompt, x_sample, p_prompt, p_sample, state_conv, state_lru, state_ret, w_in, conv_w, conv_b,
           lru_wa, lru_ba, lru_wx, lru_bx, lru_lambda, ret_gn, w_out, ln1_g, ln1_b, router_w, router_b,
           exp_w_gate, exp_w_up, exp_w_down, sh_w_gate, sh_w_up, sh_w_down, ln2_g, ln2_b,
           ple_w_proj, ple_w_gate, ple_b_gate):
    depth = w_in.shape[0]
    alpha = (2 * depth) ** 0.25
    b_p, t_p, _ = x_prompt.shape
    b_s, t_s, _ = x_sample.shape
    W = state_conv.shape[-1]
    H, dh = state_ret.shape[2], state_ret.shape[3]
    n_tokens = b_p * t_p + b_s * t_s
    cap = -(-n_tokens // GMM_ROWS) * GMM_ROWS
    hp, hs = x_prompt, x_sample
    outs = [[] for _ in range(6)]
    for i in range(depth):
        prm = (w_in[i], conv_w[i], conv_b[i], lru_wa[i], lru_ba[i], lru_wx[i], lru_bx[i], lru_lambda[i],
               ret_gn[i], w_out[i], ln1_g[i], ln1_b[i], router_w[i], router_b[i], exp_w_gate[i], exp_w_up[i],
               exp_w_down[i], sh_w_gate[i], sh_w_up[i], sh_w_down[i], ln2_g[i], ln2_b[i],
               ple_w_proj[i], ple_w_gate[i], ple_b_gate[i])
        w = _prep_layer(prm)
        zc = jnp.zeros((b_p, CONV_WIDTH - 1, W), x_prompt.dtype)
        zl = jnp.zeros((b_p, W), F32)
        zr = jnp.zeros((b_p, H, dh, dh), F32)
        yl_p, yr_p, c_p, l_p, r_p = _mixers(hp, zc, zl, zr, 0, CHUNK, w)
        yl_s, yr_s, c_s, l_s, r_s = _mixers(hs, state_conv[i], state_lru[i], state_ret[i], PAST_LEN, t_s, w)
        for o, val in zip(outs, (c_p, l_p, r_p, c_s, l_s, r_s)):
            o.append(val)
        base_p, dest_p, wts_p, cnt_p, xs = _mix(yl_p, yr_p, hp, w, alpha, jnp.zeros((1, LANES), I32), None, cap)
        base_s, dest_s, wts_s, cnt_all, xs = _mix(yl_s, yr_s, hs, w, alpha, cnt_p, xs, cap)
        ys = _gmm(xs, cnt_all[0, :N_EXPERTS], cap, n_tokens, w["e_gate"], w["e_up"], w["e_down"])
        hp = _final(base_p, dest_p, wts_p, p_prompt[i], ys, w, b_p, t_p)
        hs = _final(base_s, dest_s, wts_s, p_sample[i], ys, w, b_s, t_s)
    return (hp, hs) + tuple(jnp.stack(o) for o in outs)
```

```python
import functools

import jax
import jax.numpy as jnp
from jax import lax
from jax.experimental import pallas as pl
from jax.experimental.pallas import tpu as pltpu

F32 = jnp.float32
BF16 = jnp.bfloat16
U32 = jnp.uint32
I32 = jnp.int32

CHUNK = 64
PAST_LEN = 1024
CONV_WIDTH = 4
LRU_C = 8.0
LRU_BLOCKS = 8
RET_HEADS = 8
ROPE_BASE = 10000.0
N_EXPERTS = 64
TOP_K = 8
N_GROUPS = 8
TOPK_GROUPS = 4
ROUTED_SCALE = 2.5
LN_EPS = 1e-5
GN_EPS = 1e-6

LANES = 128
ROW_SUBLANES = 8
DRAIN_UNROLL = 8
ISSUE_CHUNKS = 4
ROWS_PER_STEP = 512
MIX_ROWS = 256
GMM_ROWS = 512
RET_GROUP_ROWS = 256
LRU_TIME_TILE = 64
VMEM_LIMIT = 56 * 1024 * 1024


def _const_spec(shape):
    zeros = (0,) * len(shape)
    return pl.BlockSpec(shape, lambda *_: zeros, pipeline_mode=pl.Buffered(1))


def _params(n_axes):
    return pltpu.CompilerParams(dimension_semantics=("arbitrary",) * n_axes,
                                vmem_limit_bytes=VMEM_LIMIT)


def _layer_norm(x, g, b):
    mu = jnp.mean(x, axis=-1, keepdims=True)
    xc = x - mu
    var = jnp.mean(xc * xc, axis=-1, keepdims=True)
    return xc * lax.rsqrt(var + LN_EPS) * g + b


def _pack_bf16_pair(x):
    c = x.shape[1] // 2
    xb = x.astype(BF16).astype(F32)
    hi = pltpu.bitcast(xb[:, :c], U32)
    lo = pltpu.bitcast(xb[:, c:], U32)
    return hi | (lo >> 16)


def _unpack_bf16_pair(pk):
    hi = pltpu.bitcast(pk & jnp.uint32(0xFFFF0000), F32)
    lo = pltpu.bitcast(pk << 16, F32)
    return hi, lo


def _inproj_body(x_ref, w_ref, zl_ref, zr_ref, *, bb, tm, lru_cols, tn):
    d = x_ref.shape[-1]
    x = x_ref[...].reshape(bb * tm, d).astype(BF16)
    for j in range(w_ref.shape[1] // tn):
        c0 = j * tn
        acc = jnp.dot(x, w_ref[:, c0:c0 + tn], preferred_element_type=F32)
        for b in range(bb):
            rows = acc[b * tm:(b + 1) * tm]
            if c0 < lru_cols:
                zl_ref[:, b * lru_cols + c0:b * lru_cols + c0 + tn] = rows
            else:
                zr_ref[b, :, c0 - lru_cols:c0 - lru_cols + tn] = rows.astype(BF16)


def _inproj(x, w_in_b, lru_cols):
    B, T, D = x.shape
    n_cols = w_in_b.shape[1]
    ret_cols = n_cols - lru_cols
    tm = min(ROWS_PER_STEP, T)
    bb = min(B, ROWS_PER_STEP // tm)
    body = functools.partial(_inproj_body, bb=bb, tm=tm, lru_cols=lru_cols, tn=512)
    return pl.pallas_call(
        body,
        grid=(B // bb, T // tm),
        in_specs=[pl.BlockSpec((bb, tm, D), lambda b, t: (b, t, 0)),
                  _const_spec((D, n_cols))],
        out_specs=[pl.BlockSpec((tm, bb * lru_cols), lambda b, t: (t, b)),
                   pl.BlockSpec((bb, tm, ret_cols), lambda b, t: (b, t, 0))],
        out_shape=[jax.ShapeDtypeStruct((T, B * lru_cols), F32),
                   jax.ShapeDtypeStruct((B, T, ret_cols), BF16)],
        compiler_params=_params(2),
        name="inproj",
    )(x, w_in_b)


def _lru_body(xl_ref, gl_ref, conv0_ref, h0_ref, cw_ref, cb_ref, wa_ref, ba_ref, wx_ref, bx_ref, lam_ref,
              after_ref, y_ref, conv_out_ref, h_out_ref, xp_s, a_s, b_s, h_s, *, tt, rows):
    del after_ref
    i = pl.program_id(0)
    B, W = h0_ref.shape
    nblk = wa_ref.shape[0]
    blk = W // nblk

    @pl.when(i == 0)
    def _():
        xp_s[0:CONV_WIDTH - 1] = conv0_ref[...]
        h_s[...] = h0_ref[...]

    xp_s[CONV_WIDTH - 1:] = xl_ref[...]

    lam = lam_ref[...]
    neg = -lam
    softplus = jnp.maximum(neg, 0.0) + jnp.log1p(jnp.exp(-jnp.abs(neg)))
    decay = (-LRU_C) * softplus

    def gates(c, carry):
        t0 = pl.multiple_of(c * rows, rows)
        xc = cb_ref[...].reshape(1, 1, W)
        for j in range(CONV_WIDTH):
            xc = xc + xp_s[pl.ds(t0 + j, rows)] * cw_ref[j:j + 1].reshape(1, 1, W)
        xc2 = xc.reshape(rows * B, W)
        xcb = xc2.astype(BF16)
        r_parts, i_parts = [], []
        for n in range(nblk):
            xb = xcb[:, n * blk:(n + 1) * blk]
            r_parts.append(jnp.dot(xb, wa_ref[n], preferred_element_type=F32))
            i_parts.append(jnp.dot(xb, wx_ref[n], preferred_element_type=F32))
        r = jax.nn.sigmoid(jnp.concatenate(r_parts, axis=1) + ba_ref[...])
        ig = jax.nn.sigmoid(jnp.concatenate(i_parts, axis=1) + bx_ref[...])
        a = jnp.exp(decay * r)
        bterm = jnp.sqrt(1.0 - a * a) * (ig * xc2)
        a_s[pl.ds(t0, rows)] = a.reshape(rows, B, W)
        b_s[pl.ds(t0, rows)] = bterm.reshape(rows, B, W)
        return carry

    lax.fori_loop(0, tt // rows, gates, 0)

    def step(t, h):
        hn = a_s[t] * h + b_s[t]
        y_ref[t] = hn * jax.nn.gelu(gl_ref[t])
        return hn

    h_last = lax.fori_loop(0, tt, step, h_s[...], unroll=8)
    h_s[...] = h_last
    tail = xp_s[tt:tt + CONV_WIDTH - 1]
    xp_s[0:CONV_WIDTH - 1] = tail
    conv_out_ref[...] = tail
    h_out_ref[...] = h_last


def _lru(z_lru3, conv0_tm, h0, conv_w, conv_b, wa_b, ba, wx_b, bx, lam, after):
    T, B, W2 = z_lru3.shape
    W = W2 // 2
    tt = min(LRU_TIME_TILE, T)
    rows = max(1, min(tt, 128 // B))
    body = functools.partial(_lru_body, tt=tt, rows=rows)
    nb = wa_b.shape[0]
    blk = W // nb
    return pl.pallas_call(
        body,
        grid=(T // tt,),
        in_specs=[pl.BlockSpec((tt, B, W), lambda t: (t, 0, 0)),
                  pl.BlockSpec((tt, B, W), lambda t: (t, 0, 1)),
                  _const_spec((CONV_WIDTH - 1, B, W)),
                  _const_spec((B, W)),
                  _const_spec((CONV_WIDTH, W)),
                  _const_spec((1, W)),
                  _const_spec((nb, blk, blk)),
                  _const_spec((1, W)),
                  _const_spec((nb, blk, blk)),
                  _const_spec((1, W)),
                  _const_spec((1, W)),
                  pl.BlockSpec(memory_space=pl.ANY)],
        out_specs=[pl.BlockSpec((tt, B, W), lambda t: (t, 0, 0)),
                   pl.BlockSpec((CONV_WIDTH - 1, B, W), lambda t: (0, 0, 0)),
                   pl.BlockSpec((B, W), lambda t: (0, 0))],
        out_shape=[jax.ShapeDtypeStruct((T, B, W), F32),
                   jax.ShapeDtypeStruct((CONV_WIDTH - 1, B, W), F32),
                   jax.ShapeDtypeStruct((B, W), F32)],
        scratch_shapes=[pltpu.VMEM((tt + CONV_WIDTH - 1, B, W), F32),
                        pltpu.VMEM((tt, B, W), F32),
                        pltpu.VMEM((tt, B, W), F32),
                        pltpu.VMEM((B, W), F32)],
        compiler_params=_params(1),
        name="rglru",
    )(z_lru3, z_lru3, conv0_tm, h0, conv_w, conv_b, wa_b, ba, wx_b, bx, lam, after)


def _ret_body(q_ref, k_ref, v_ref, g_ref, cos_ref, sin_ref, mask_ref, qdec_ref, kdec_ref, cdec_ref, gn_ref,
              s0_ref, y_ref, s_out_ref, *, rg, hb, dh):
    T = q_ref.shape[1]
    scale = dh ** -0.5

    def rope(t, cos, sin):
        return t * cos + pltpu.roll(t, dh // 2, axis=1) * sin

    for hh in range(hb):
        cols = slice(hh * dh, (hh + 1) * dh)

        def group(c, s, hh=hh, cols=cols):
            r0 = pl.multiple_of(c * rg, rg)
            rws = pl.ds(r0, rg)
            cos = cos_ref[rws, :]
            sin = sin_ref[rws, :]
            q = rope(q_ref[0, rws, cols].astype(F32), cos, sin)
            k = rope(k_ref[0, rws, cols].astype(F32), cos, sin) * scale
            v = v_ref[0, rws, cols]
            scores = lax.dot_general(q.astype(BF16), k.astype(BF16), (((1,), (1,)), ((), ())),
                                     preferred_element_type=F32)
            scores = scores * mask_ref[hh]
            o = jnp.dot(scores.astype(BF16), v, preferred_element_type=F32)
            o = o + jnp.dot((q * qdec_ref[hh]).astype(BF16), s.astype(BF16), preferred_element_type=F32)
            kd = (k * kdec_ref[hh]).astype(BF16)
            kv = lax.dot_general(kd, v, (((0,), (0,)), ((), ())), preferred_element_type=F32)
            s_new = cdec_ref[hh] * s + kv
            mu = jnp.mean(o, axis=-1, keepdims=True)
            oc = o - mu
            var = jnp.mean(oc * oc, axis=-1, keepdims=True)
            on = oc * lax.rsqrt(var + GN_EPS) * gn_ref[:, cols]
            g = g_ref[0, rws, cols].astype(F32)
            y_ref[0, rws, cols] = (g * jax.nn.sigmoid(g) * on).astype(y_ref.dtype)
            return s_new

        n_groups = T // rg
        s_out_ref[0, hh] = lax.fori_loop(0, n_groups, group, s0_ref[0, hh],
                                         unroll=2 if n_groups % 2 == 0 else 1)


def _retention_tables(T, pos0, chunk, rg, dh):
    half = dh // 2
    inv = ROPE_BASE ** (-jnp.arange(half, dtype=F32) / half)
    pos = pos0 + jnp.arange(T)
    ang = pos.astype(F32)[:, None] * inv[None, :]
    cos, sin = jnp.cos(ang), jnp.sin(ang)
    cos2 = jnp.concatenate([cos, cos], axis=1)
    sin2 = jnp.concatenate([-sin, sin], axis=1)
    log_g = jnp.log1p(-jnp.exp2(-5.0 - jnp.arange(RET_HEADS, dtype=F32)))[:, None, None]
    idx = jnp.arange(rg, dtype=F32)
    ci = jnp.floor(idx / chunk)
    diff = idx[:, None] - idx[None, :]
    same = ci[:, None] == ci[None, :]
    earlier = ci[None, :] < ci[:, None]
    dist = jnp.where(same, jnp.abs(diff), diff)
    mask = jnp.where(same | earlier, jnp.exp(dist[None] * log_g), 0.0)
    ones = jnp.ones((1, 1, dh), F32)
    qdec = jnp.exp((idx + 1.0)[None, :, None] * log_g) * ones
    kdec = jnp.exp((rg - 1.0 - idx)[None, :, None] * log_g) * ones
    cdec = jnp.exp(rg * log_g) * ones
    return cos2, sin2, mask, qdec, kdec, cdec


def _retention(z_ret, s0, gn, pos0, chunk):
    B, T, C4 = z_ret.shape
    H = RET_HEADS
    dh = C4 // (4 * H)
    rg = min(T, max(chunk, (RET_GROUP_ROWS // chunk) * chunk))
    hb = H if T * H * dh <= 64 * 1024 else 1
    nh = H // hb
    cos2, sin2, mask, qdec, kdec, cdec = _retention_tables(T, pos0, chunk, rg, dh)
    body = functools.partial(_ret_body, rg=rg, hb=hb, dh=dh)
    col = lambda off: (lambda b, h: (b, 0, off * nh + h))
    return pl.pallas_call(
        body,
        grid=(B, nh),
        in_specs=[pl.BlockSpec((1, T, hb * dh), col(0)),
                  pl.BlockSpec((1, T, hb * dh), col(1)),
                  pl.BlockSpec((1, T, hb * dh), col(2)),
                  pl.BlockSpec((1, T, hb * dh), col(3)),
                  _const_spec((T, dh)),
                  _const_spec((T, dh)),
                  pl.BlockSpec((hb, rg, rg), lambda b, h: (h, 0, 0)),
                  pl.BlockSpec((hb, rg, dh), lambda b, h: (h, 0, 0)),
                  pl.BlockSpec((hb, rg, dh), lambda b, h: (h, 0, 0)),
                  pl.BlockSpec((hb, 1, dh), lambda b, h: (h, 0, 0)),
                  pl.BlockSpec((1, hb * dh), lambda b, h: (0, h)),
                  pl.BlockSpec((1, hb, dh, dh), lambda b, h: (b, h, 0, 0))],
        out_specs=[pl.BlockSpec((1, T, hb * dh), lambda b, h: (b, 0, h)),
                   pl.BlockSpec((1, hb, dh, dh), lambda b, h: (b, h, 0, 0))],
        out_shape=[jax.ShapeDtypeStruct((B, T, H * dh), BF16),
                   jax.ShapeDtypeStruct((B, H, dh, dh), F32)],
        compiler_params=_params(2),
        name="retention",
    )(z_ret, z_ret, z_ret, z_ret, cos2, sin2, mask, qdec, kdec, cdec, gn, s0)


def _seg_allreduce(v, lane, op):
    for s in (1, 2, 4):
        up = pltpu.roll(v, LANES - s, axis=1)
        dn = pltpu.roll(v, s, axis=1)
        v = op(v, jnp.where((lane & s) == 0, up, dn))
    return v


def _mix_body(*refs, bb, tm, alpha, cap, n_steps, aliased):
    (yl_ref, yr_ref, x_ref, wo_ref, g1_ref, b1_ref, rwh_ref, rwl_ref, rb_ref, sg_ref, su_ref, sd_ref,
     tri_ref, cnt_in_ref) = refs[:14]
    refs = refs[15:] if aliased else refs[14:]
    base_ref, dest_ref, wts_ref, cnt_ref, xs_ref, carry_s, xpk_s, dv_s, ds_s, row_sems, idx_sem = refs
    i = pl.program_id(0)
    m = bb * tm
    d = x_ref.shape[-1]
    w = yl_ref.shape[1] // bb
    slot = i % 2
    chunk = m // ISSUE_CHUNKS

    def row_copy(sl, n, dst):
        return pltpu.make_async_copy(xpk_s.at[sl, n], xs_ref.at[dst], row_sems.at[sl])

    def issue_rows(sl, lo, hi):
        def one(n, c):
            for kk in range(TOP_K):
                row_copy(sl, n, ds_s[sl, kk, n]).start()
            return c
        lax.fori_loop(lo, hi, one, 0)

    def issue_prev_chunk(c):
        @pl.when(i > 0)
        def _():
            issue_rows(1 - slot, c * chunk, (c + 1) * chunk)

    def drain(sl):
        def one(n, c):
            for _ in range(DRAIN_UNROLL * TOP_K):
                row_copy(sl, 0, 0).wait()
            return c
        lax.fori_loop(0, m // DRAIN_UNROLL, one, 0)

    @pl.when(i == 0)
    def _():
        carry_s[...] = jnp.zeros_like(carry_s)
        carry_s[0:1, :] = cnt_in_ref[...].astype(F32)

    issue_prev_chunk(0)
    yl = jnp.concatenate([yl_ref[:, b * w:(b + 1) * w] for b in range(bb)], axis=0).astype(BF16)
    yr = yr_ref[...].reshape(m, yr_ref.shape[-1])
    mix = jnp.dot(yl, wo_ref[:w], preferred_element_type=F32)
    mix = mix + jnp.dot(yr, wo_ref[w:], preferred_element_type=F32)
    x1 = _layer_norm(alpha * x_ref[...].reshape(m, d) + mix, g1_ref[...], b1_ref[...])
    x1b = x1.astype(BF16)
    issue_prev_chunk(1)

    x1l = (x1 - x1b.astype(F32)).astype(BF16)
    logits = jnp.dot(x1b, rwh_ref[...], preferred_element_type=F32)
    logits = logits + jnp.dot(x1l, rwh_ref[...], preferred_element_type=F32)
    logits = logits + jnp.dot(x1b, rwl_ref[...], preferred_element_type=F32)

    s = jax.nn.sigmoid(logits)
    sb = s + rb_ref[...]
    lane = lax.broadcasted_iota(I32, (m, LANES), 1)
    e_id = lane & (N_EXPERTS - 1)
    e_f = e_id.astype(F32)
    low = lane < N_EXPERTS
    e_low = jnp.where(low, e_f, -1.0)
    grp = e_id >> 3
    big = jnp.float32(1e9)
    ninf = jnp.float32(-jnp.inf)

    m1 = _seg_allreduce(sb, lane, jnp.maximum)
    first_max = _seg_allreduce(jnp.where(sb == m1, e_f, big), lane, jnp.minimum)
    m2 = _seg_allreduce(jnp.where(e_f == first_max, ninf, sb), lane, jnp.maximum)
    gs = m1 + m2
    hg = jnp.dot(x1b, sg_ref[...], preferred_element_type=F32)
    issue_prev_chunk(2)
    rank = jnp.zeros((m, LANES), F32)
    for dgrp in range(1, N_GROUPS):
        other = pltpu.roll(gs, 8 * dgrp, axis=1)
        tie = jnp.where(grp >= dgrp, 1.0, 0.0)
        rank = rank + jnp.where(other > gs, 1.0, jnp.where(other == gs, tie, 0.0))
    v = jnp.where(rank < TOPK_GROUPS, jnp.where(low, sb, ninf), ninf)
    hu = jnp.dot(x1b, su_ref[...], preferred_element_type=F32)
    hs = (hg * jax.nn.sigmoid(hg) * hu).astype(BF16)

    idx_cols, w_cols = [], []
    sel = jnp.zeros((m, LANES), F32)
    for rnd in range(TOP_K):
        mx = jnp.max(v, axis=1, keepdims=True)
        idx = jnp.min(jnp.where(v == mx, e_f, big), axis=1, keepdims=True)
        hit = e_low == idx
        w_cols.append(jnp.sum(jnp.where(hit, s, 0.0), axis=1, keepdims=True))
        idx_cols.append(idx)
        v = jnp.where(hit, ninf, v)
        sel = jnp.where(hit, 1.0, sel)
        if rnd == TOP_K // 2 - 1:
            base_ref[...] = alpha * x1 + jnp.dot(hs, sd_ref[...], preferred_element_type=F32)
            issue_prev_chunk(3)

    packed = _pack_bf16_pair(x1)
    for sub in range(ROW_SUBLANES):
        xpk_s[slot, :, sub, :] = packed[:, sub * LANES:(sub + 1) * LANES]

    cum = jnp.dot(tri_ref[...], sel.astype(BF16), preferred_element_type=F32) + carry_s[0:1, :]
    carry_s[0:1, :] = carry_s[0:1, :] + jnp.sum(sel, axis=0, keepdims=True)
    cnt_ref[...] = carry_s[0:1, :].astype(I32)

    wsum = w_cols[0]
    for c in w_cols[1:]:
        wsum = wsum + c
    d_out = jnp.zeros((m, LANES), F32)
    w_out = jnp.zeros((m, LANES), F32)
    for kk in range(TOP_K):
        hit = e_low == idx_cols[kk]
        pk = jnp.sum(jnp.where(hit, cum, 0.0), axis=1, keepdims=True)
        d_out = jnp.where(lane == kk, idx_cols[kk] * float(cap) + pk, d_out)
        w_out = jnp.where(lane == kk, w_cols[kk] / wsum * ROUTED_SCALE, w_out)
    wts_ref[...] = w_out[:, :TOP_K]
    dest_t = jnp.transpose(d_out)[:TOP_K].astype(I32)
    dest_ref[0] = dest_t
    dv_s[...] = dest_t
    to_smem = pltpu.make_async_copy(dv_s, ds_s.at[slot], idx_sem)
    to_smem.start()
    to_smem.wait()

    @pl.when(i > 0)
    def _():
        drain(1 - slot)

    @pl.when(i == n_steps - 1)
    def _():
        issue_rows(slot, 0, m)
        drain(slot)


def _mix(y_lru2, y_ret, x, w, alpha, cnt_in, xs, cap):
    B, T, D = x.shape
    assert D == 2 * ROW_SUBLANES * LANES, "a packed token row must fill exactly one (8, 128) tile"
    W = y_ret.shape[-1]
    tm = min(MIX_ROWS, T)
    bb = min(B, MIX_ROWS // tm)
    m = bb * tm
    n = B * T
    nt = T // tm
    n_steps = (B // bb) * nt
    hs = w["sg"].shape[1]
    aliased = xs is not None
    tri = (lax.broadcasted_iota(I32, (m, m), 1) < lax.broadcasted_iota(I32, (m, m), 0)).astype(BF16)
    body = functools.partial(_mix_body, bb=bb, tm=tm, alpha=alpha, cap=cap, n_steps=n_steps, aliased=aliased)
    in_specs = [pl.BlockSpec((tm, bb * W), lambda i: (i % nt, i // nt)),
                pl.BlockSpec((bb, tm, W), lambda i: (i // nt, i % nt, 0)),
                pl.BlockSpec((bb, tm, D), lambda i: (i // nt, i % nt, 0)),
                _const_spec((2 * W, D)),
                _const_spec((1, D)),
                _const_spec((1, D)),
                _const_spec((D, LANES)),
                _const_spec((D, LANES)),
                _const_spec((1, LANES)),
                _const_spec((D, hs)),
                _const_spec((D, hs)),
                _const_spec((hs, D)),
                _const_spec((m, m)),
                _const_spec((1, LANES))]
    args = [y_lru2, y_ret, x, w["wo"], w["g1"], w["b1"], w["rwh"], w["rwl"], w["rb"], w["sg"], w["su"], w["sd"],
            tri, cnt_in]
    if aliased:
        in_specs.append(pl.BlockSpec(memory_space=pl.ANY))
        args.append(xs)
    return pl.pallas_call(
        body,
        grid=(n_steps,),
        in_specs=in_specs,
        out_specs=[pl.BlockSpec((m, D), lambda i: (i, 0)),
                   pl.BlockSpec((1, TOP_K, m), lambda i: (i, 0, 0)),
                   pl.BlockSpec((m, TOP_K), lambda i: (i, 0)),
                   pl.BlockSpec((1, LANES), lambda i: (0, 0)),
                   pl.BlockSpec(memory_space=pl.ANY)],
        out_shape=[jax.ShapeDtypeStruct((n, D), F32),
                   jax.ShapeDtypeStruct((n_steps, TOP_K, m), I32),
                   jax.ShapeDtypeStruct((n, TOP_K), F32),
                   jax.ShapeDtypeStruct((1, LANES), I32),
                   jax.ShapeDtypeStruct((N_EXPERTS * cap, ROW_SUBLANES, LANES), U32)],
        scratch_shapes=[pltpu.VMEM((8, LANES), F32),
                        pltpu.VMEM((2, m, ROW_SUBLANES, LANES), U32),
                        pltpu.VMEM((TOP_K, m), I32),
                        pltpu.SMEM((2, TOP_K, m), I32),
                        pltpu.SemaphoreType.DMA((2,)),
                        pltpu.SemaphoreType.DMA(())],
        input_output_aliases={14: 4} if aliased else {},
        compiler_params=_params(1),
        name="mix_router",
    )(*args)


def _gmm_body(ge_ref, gr_ref, gn_ref, gt_ref, gx_ref, gs_ref, xs_ref, wg_ref, wu_ref, wd_ref, ys_ref,
              wgf_s, wuf_s, wdf_s, wgu_s, wd_s, x_s, y_s, w_sems, in_sems, out_sems, *, tm, n_items):
    i = pl.program_id(0)
    hid = wg_ref.shape[2]
    slot = i % 2

    def weight_copies(expert, sl):
        return [pltpu.make_async_copy(wg_ref.at[expert], wgf_s.at[sl], w_sems.at[sl]),
                pltpu.make_async_copy(wu_ref.at[expert], wuf_s.at[sl], w_sems.at[sl]),
                pltpu.make_async_copy(wd_ref.at[expert], wdf_s.at[sl], w_sems.at[sl])]

    def tile_copies(to_vmem, item, sl):
        r0 = pl.multiple_of(gr_ref[item] * tm, tm)
        out = []
        for sub in range(ROW_SUBLANES):
            cols = pl.ds(sub * LANES, LANES)
            if to_vmem:
                out.append(pltpu.make_async_copy(xs_ref.at[pl.ds(r0, tm), sub, :], x_s.at[sl, :, cols],
                                                 in_sems.at[sl]))
            else:
                out.append(pltpu.make_async_copy(y_s.at[sl, :, cols], ys_ref.at[pl.ds(r0, tm), sub, :],
                                                 out_sems.at[sl]))
        return out

    @pl.when(i == 0)
    def _():
        for cp in weight_copies(ge_ref[0], 0):
            cp.start()
        for cp in tile_copies(True, 0, 0):
            cp.start()
        y_s[...] = jnp.zeros_like(y_s)

    nxt = jnp.minimum(i + 1, n_items - 1)

    @pl.when((i + 1 < n_items) & (gn_ref[nxt] > 0))
    def _():
        for cp in tile_copies(True, nxt, 1 - slot):
            cp.start()

    e = ge_ref[i]
    e_prev = ge_ref[jnp.maximum(i - 1, 0)]

    @pl.when((i == 0) | (e != e_prev))
    def _():
        wsl = gs_ref[i]
        nxt_e = gx_ref[i]

        @pl.when(nxt_e >= 0)
        def _():
            for cp in weight_copies(nxt_e, 1 - wsl):
                cp.start()

        for cp in weight_copies(e, wsl):
            cp.wait()
        wgu_s[:, :hid] = wgf_s[wsl].astype(BF16)
        wgu_s[:, hid:] = wuf_s[wsl].astype(BF16)
        wd_s[...] = wdf_s[wsl].astype(BF16)

    n_valid = gn_ref[i]

    @pl.when(n_valid > 0)
    def _():
        for cp in tile_copies(True, i, slot):
            cp.wait()

        @pl.when(i >= 2)
        def _():
            for cp in tile_copies(False, i, slot):
                cp.wait()

        def expert_mlp(rows):
            pk = x_s[slot, :rows]
            valid = lax.broadcasted_iota(I32, pk.shape, 0) < n_valid
            xa, xb = _unpack_bf16_pair(jnp.where(valid, pk, jnp.uint32(0)))
            x = jnp.concatenate([xa.astype(BF16), xb.astype(BF16)], axis=1)
            h2 = jnp.dot(x, wgu_s[...], preferred_element_type=F32)
            hg = h2[:, :hid]
            h = (hg * jax.nn.sigmoid(hg) * h2[:, hid:]).astype(BF16)
            y_s[slot, :rows] = _pack_bf16_pair(jnp.dot(h, wd_s[...], preferred_element_type=F32))

        @pl.when(n_valid > tm // 2)
        def _():
            expert_mlp(tm)

        @pl.when(n_valid <= tm // 2)
        def _():
            expert_mlp(tm // 2)

        for cp in tile_copies(False, i, slot):
            cp.start()

    @pl.when(i == n_items - 1)
    def _():
        total = gt_ref[0]

        @pl.when(total >= 2)
        def _():
            for cp in tile_copies(False, 0, total % 2):
                cp.wait()

        for cp in tile_copies(False, 0, (total + 1) % 2):
            cp.wait()


def _gmm_metadata(counts, cap, tm, n_items):
    e = counts.shape[0]
    tiles = (counts + tm - 1) // tm
    item_end = jnp.cumsum(tiles)
    total = item_end[-1]
    it = jnp.arange(n_items, dtype=I32)
    itc = jnp.minimum(it, total - 1)
    ge = jnp.sum((item_end[None, :] <= itc[:, None]).astype(I32), axis=1)
    onehot = ge[:, None] == jnp.arange(e, dtype=I32)[None, :]
    start = jnp.sum(jnp.where(onehot, (item_end - tiles)[None, :], 0), axis=1)
    cnt = jnp.sum(jnp.where(onehot, counts[None, :], 0), axis=1)
    j = itc - start
    gr = ge * (cap // tm) + j
    gn = jnp.where(it < total, jnp.clip(cnt - j * tm, 0, tm), 0)
    ids = jnp.arange(e, dtype=I32)
    live = tiles > 0
    later = live[None, :] & (ids[None, :] > ids[:, None])
    next_e = jnp.min(jnp.where(later, ids[None, :], e), axis=1)
    next_e = jnp.where(next_e < e, next_e, -1)
    wslot = (jnp.cumsum(live.astype(I32)) - 1) % 2
    gx = jnp.sum(jnp.where(onehot, next_e[None, :], 0), axis=1)
    gs = jnp.sum(jnp.where(onehot, wslot[None, :], 0), axis=1)
    return tuple(a.astype(I32) for a in (ge, gr, gn, total.reshape(1), gx, gs))


def _gmm(xs, counts, cap, n_tokens, wg, wu, wd):
    tm = GMM_ROWS
    e, d, hid = wg.shape
    n_items = (n_tokens * TOP_K) // tm + e
    meta = _gmm_metadata(counts, cap, tm, n_items)
    body = functools.partial(_gmm_body, tm=tm, n_items=n_items)
    grid_spec = pltpu.PrefetchScalarGridSpec(
        num_scalar_prefetch=6,
        grid=(n_items,),
        in_specs=[pl.BlockSpec(memory_space=pl.ANY)] * 4,
        out_specs=pl.BlockSpec(memory_space=pl.ANY),
        scratch_shapes=[pltpu.VMEM((2, d, hid), F32), pltpu.VMEM((2, d, hid), F32),
                        pltpu.VMEM((2, hid, d), F32),
                        pltpu.VMEM((d, 2 * hid), BF16), pltpu.VMEM((hid, d), BF16),
                        pltpu.VMEM((2, tm, d // 2), U32), pltpu.VMEM((2, tm, d // 2), U32),
                        pltpu.SemaphoreType.DMA((2,)), pltpu.SemaphoreType.DMA((2,)),
                        pltpu.SemaphoreType.DMA((2,))],
    )
    return pl.pallas_call(
        body,
        grid_spec=grid_spec,
        out_shape=jax.ShapeDtypeStruct(xs.shape, U32),
        compiler_params=_params(1),
        name="expert_gmm",
    )(*meta, xs, wg, wu, wd)


def _final_body(d0_ref, dn_ref, wts_ref, base_ref, p_ref, ys_ref, pg_ref, pb_ref, pp_ref, g2_ref, b2_ref,
                out_ref, rows_s, hi_s, lo_s, wrep_s, sems, *, bb, tm, n_steps):
    i = pl.program_id(0)
    m = bb * tm
    d = base_ref.shape[1]
    slot = i % 2

    def row_copy(sl, d_row, kk, n):
        return pltpu.make_async_copy(ys_ref.at[d_row], rows_s.at[sl, kk, n], sems.at[sl])

    chunk = m // ISSUE_CHUNKS

    def issue(sl, dref, lo, hi):
        def one(n, c):
            for kk in range(TOP_K):
                row_copy(sl, dref[0, kk, n], kk, n).start()
            return c
        lax.fori_loop(lo, hi, one, 0)

    def issue_next_chunk(c):
        @pl.when(i + 1 < n_steps)
        def _():
            issue(1 - slot, dn_ref, c * chunk, (c + 1) * chunk)

    @pl.when(i == 0)
    def _():
        issue(0, d0_ref, 0, m)

    def drain(n, c):
        for _ in range(DRAIN_UNROLL * TOP_K):
            row_copy(slot, 0, 0, 0).wait()
        return c

    lax.fori_loop(0, m // DRAIN_UNROLL, drain, 0)

    wts = wts_ref[...]
    for kk in range(TOP_K):
        wrep_s[kk] = jnp.broadcast_to(wts[:, kk:kk + 1], (m, LANES))

    def combine(n, c):
        hi = jnp.zeros((ROW_SUBLANES, LANES), F32)
        lo = jnp.zeros((ROW_SUBLANES, LANES), F32)
        for kk in range(TOP_K):
            a, b = _unpack_bf16_pair(rows_s[slot, kk, n])
            wk = wrep_s[kk, pl.ds(n, 1), :]
            hi = hi + wk * a
            lo = lo + wk * b
        hi_s[n] = hi
        lo_s[n] = lo
        return c

    issue_next_chunk(0)
    lax.fori_loop(0, m // 2, combine, 0, unroll=2)
    issue_next_chunk(1)
    lax.fori_loop(m // 2, m, combine, 0, unroll=2)
    issue_next_chunk(2)
    routed = jnp.concatenate([hi_s[:, sub, :] for sub in range(ROW_SUBLANES)]
                             + [lo_s[:, sub, :] for sub in range(ROW_SUBLANES)], axis=1)
    x2 = _layer_norm(base_ref[...] + routed, g2_ref[...], b2_ref[...])
    issue_next_chunk(3)
    gate = jax.nn.sigmoid(jnp.dot(x2.astype(BF16), pg_ref[...], preferred_element_type=F32) + pb_ref[...])
    proj = jnp.dot(p_ref[...].reshape(m, p_ref.shape[-1]).astype(BF16), pp_ref[...], preferred_element_type=F32)
    out_ref[...] = (x2 + gate * proj).reshape(bb, tm, d)


def _final(base, dest, wts, p, ys, w, B, T):
    n, D = base.shape
    tm = min(MIX_ROWS, T)
    bb = min(B, MIX_ROWS // tm)
    m = bb * tm
    nt = T // tm
    n_steps = (B // bb) * nt
    pd = p.shape[-1]
    body = functools.partial(_final_body, bb=bb, tm=tm, n_steps=n_steps)
    return pl.pallas_call(
        body,
        grid=(n_steps,),
        in_specs=[pl.BlockSpec((1, TOP_K, m), lambda i: (0, 0, 0), memory_space=pltpu.SMEM),
                  pl.BlockSpec((1, TOP_K, m), lambda i: (jnp.minimum(i + 1, n_steps - 1), 0, 0),
                               memory_space=pltpu.SMEM),
                  pl.BlockSpec((m, TOP_K), lambda i: (i, 0)),
                  pl.BlockSpec((m, D), lambda i: (i, 0)),
                  pl.BlockSpec((bb, tm, pd), lambda i: (i // nt, i % nt, 0)),
                  pl.BlockSpec(memory_space=pl.ANY),
                  _const_spec((D, D)),
                  _const_spec((1, D)),
                  _const_spec((pd, D)),
                  _const_spec((1, D)),
                  _const_spec((1, D))],
        out_specs=pl.BlockSpec((bb, tm, D), lambda i: (i // nt, i % nt, 0)),
        out_shape=jax.ShapeDtypeStruct((B, T, D), F32),
        scratch_shapes=[pltpu.VMEM((2, TOP_K, m, ROW_SUBLANES, LANES), U32),
                        pltpu.VMEM((m, ROW_SUBLANES, LANES), F32),
                        pltpu.VMEM((m, ROW_SUBLANES, LANES), F32),
                        pltpu.VMEM((TOP_K, m, LANES), F32),
                        pltpu.SemaphoreType.DMA((2,))],
        compiler_params=_params(1),
        name="combine_final",
    )(dest, dest, wts, base, p, ys, w["pg"], w["pb"], w["pp"], w["g2"], w["b2"])


def _prep_layer(prm):
    (w_in, conv_w, conv_b, lru_wa, lru_ba, lru_wx, lru_bx, lru_lambda, ret_gn, w_out, ln1_g, ln1_b,
     router_w, router_b, e_gate, e_up, e_down, s_gate, s_up, s_down, ln2_g, ln2_b,
     ple_w_proj, ple_w_gate, ple_b_gate) = prm
    row = lambda v: v.reshape(1, -1)
    rw2 = jnp.concatenate([router_w, router_w], axis=1)
    rwh = rw2.astype(BF16)
    rwl = (rw2 - rwh.astype(F32)).astype(BF16)
    return dict(
        w_in=w_in.astype(BF16), conv_w=conv_w, conv_b=row(conv_b), wa=lru_wa.astype(BF16), ba=row(lru_ba),
        wx=lru_wx.astype(BF16), bx=row(lru_bx), lam=row(lru_lambda), gn=row(ret_gn), wo=w_out.astype(BF16),
        g1=row(ln1_g), b1=row(ln1_b), rwh=rwh, rwl=rwl, rb=row(jnp.concatenate([router_b, router_b])),
        e_gate=e_gate, e_up=e_up, e_down=e_down, sg=s_gate.astype(BF16), su=s_up.astype(BF16),
        sd=s_down.astype(BF16), g2=row(ln2_g), b2=row(ln2_b), pp=ple_w_proj.astype(BF16),
        pg=ple_w_gate.astype(BF16), pb=row(ple_b_gate))


def _mixers(x, conv_st, lru_st, ret_st, pos0, chunk, w):
    B, T, D = x.shape
    W = conv_st.shape[-1]
    z_lru, z_ret = _inproj(x, w["w_in"], 2 * W)
    y_ret, new_ret = _retention(z_ret, ret_st, w["gn"], pos0, chunk)
    y_lru, conv_tm, new_lru = _lru(z_lru.reshape(T, B, 2 * W), jnp.transpose(conv_st, (1, 0, 2)), lru_st,
                                   w["conv_w"], w["conv_b"], w["wa"], w["ba"], w["wx"], w["bx"], w["lam"],
                                   after=new_ret)
    return y_lru.reshape(T, B * W), y_ret, jnp.transpose(conv_tm, (1, 0, 2)), new_lru, new_ret


def kernel(x_prompt, x_sample, p_prompt, p_sample, state_conv, state_lru, state_ret, w_in, conv_w, conv_b,
           lru_wa, lru_ba, lru_wx, lru_bx, lru_lambda, ret_gn, w_out, ln1_g, ln1_b, router_w, router_b,
           exp_w_gate, exp_w_up, exp_w_down, sh_w_gate, sh_w_up, sh_w_down, ln2_g, ln2_b,
           ple_w_proj, ple_w_gate, ple_b_gate):
    depth = w_in.shape[0]
    alpha = (2 * depth) ** 0.25
    b_p, t_p, _ = x_prompt.shape
    b_s, t_s, _ = x_sample.shape
    W = state_conv.shape[-1]
    H, dh = state_ret.shape[2], state_ret.shape[3]
    n_tokens = b_p * t_p + b_s * t_s
    cap = -(-n_tokens // GMM_ROWS) * GMM_ROWS
    hp, hs = x_prompt, x_sample
    outs = [[] for _ in range(6)]
    for i in range(depth):
        prm = (w_in[i], conv_w[i], conv_b[i], lru_wa[i], lru_ba[i], lru_wx[i], lru_bx[i], lru_lambda[i],
               ret_gn[i], w_out[i], ln1_g[i], ln1_b[i], router_w[i], router_b[i], exp_w_gate[i], exp_w_up[i],
               exp_w_down[i], sh_w_gate[i], sh_w_up[i], sh_w_down[i], ln2_g[i], ln2_b[i],
               ple_w_proj[i], ple_w_gate[i], ple_b_gate[i])
        w = _prep_layer(prm)
        zc = jnp.zeros((b_p, CONV_WIDTH - 1, W), x_prompt.dtype)
        zl = jnp.zeros((b_p, W), F32)
        zr = jnp.zeros((b_p, H, dh, dh), F32)
        yl_p, yr_p, c_p, l_p, r_p = _mixers(hp, zc, zl, zr, 0, CHUNK, w)
        yl_s, yr_s, c_s, l_s, r_s = _mixers(hs, state_conv[i], state_lru[i], state_ret[i], PAST_LEN, t_s, w)
        for o, val in zip(outs, (c_p, l_p, r_p, c_s, l_s, r_s)):
            o.append(val)
        base_p, dest_p, wts_p, cnt_p, xs = _mix(yl_p, yr_p, hp, w, alpha, jnp.zeros((1, LANES), I32), None, cap)
        base_s, dest_s, wts_s, cnt_all, xs = _mix(yl_s, yr_s, hs, w, alpha, cnt_p, xs, cap)
        ys = _gmm(xs, cnt_all[0, :N_EXPERTS], cap, n_tokens, w["e_gate"], w["e_up"], w["e_down"])
        hp = _final(base_p, dest_p, wts_p, p_prompt[i], ys, w, b_p, t_p)
        hs = _final(base_s, dest_s, wts_s, p_sample[i], ys, w, b_s, t_s)
    return (hp, hs) + tuple(jnp.stack(o) for o in outs)
```

```python
import functools

import jax
import jax.numpy as jnp
from jax import lax
from jax.experimental import pallas as pl
from jax.experimental.pallas import tpu as pltpu

F32 = jnp.float32
BF16 = jnp.bfloat16
U32 = jnp.uint32
I32 = jnp.int32

CHUNK = 64
PAST_LEN = 1024
CONV_WIDTH = 4
LRU_C = 8.0
LRU_BLOCKS = 8
RET_HEADS = 8
ROPE_BASE = 10000.0
N_EXPERTS = 64
TOP_K = 8
N_GROUPS = 8
TOPK_GROUPS = 4
ROUTED_SCALE = 2.5
LN_EPS = 1e-5
GN_EPS = 1e-6

LANES = 128
ROW_SUBLANES = 8
DRAIN_UNROLL = 8
ISSUE_CHUNKS = 4
ROWS_PER_STEP = 512
MIX_ROWS = 256
GMM_ROWS = 512
RET_GROUP_ROWS = 256
LRU_TIME_TILE = 64
VMEM_LIMIT = 56 * 1024 * 1024


def _const_spec(shape):
    zeros = (0,) * len(shape)
    return pl.BlockSpec(shape, lambda *_: zeros, pipeline_mode=pl.Buffered(1))


def _params(n_axes):
    return pltpu.CompilerParams(dimension_semantics=("arbitrary",) * n_axes,
                                vmem_limit_bytes=VMEM_LIMIT)


def _layer_norm(x, g, b):
    mu = jnp.mean(x, axis=-1, keepdims=True)
    xc = x - mu
    var = jnp.mean(xc * xc, axis=-1, keepdims=True)
    return xc * lax.rsqrt(var + LN_EPS) * g + b


def _pack_bf16_pair(x):
    c = x.shape[1] // 2
    xb = x.astype(BF16).astype(F32)
    hi = pltpu.bitcast(xb[:, :c], U32)
    lo = pltpu.bitcast(xb[:, c:], U32)
    return hi | (lo >> 16)


def _unpack_bf16_pair(pk):
    hi = pltpu.bitcast(pk & jnp.uint32(0xFFFF0000), F32)
    lo = pltpu.bitcast(pk << 16, F32)
    return hi, lo


def _inproj_body(x_ref, w_ref, zl_ref, zr_ref, *, bb, tm, lru_cols, tn):
    d = x_ref.shape[-1]
    x = x_ref[...].reshape(bb * tm, d).astype(BF16)
    for j in range(w_ref.shape[1] // tn):
        c0 = j * tn
        acc = jnp.dot(x, w_ref[:, c0:c0 + tn], preferred_element_type=F32)
        for b in range(bb):
            rows = acc[b * tm:(b + 1) * tm]
            if c0 < lru_cols:
                zl_ref[:, b * lru_cols + c0:b * lru_cols + c0 + tn] = rows
            else:
                zr_ref[b, :, c0 - lru_cols:c0 - lru_cols + tn] = rows.astype(BF16)


def _inproj(x, w_in_b, lru_cols):
    B, T, D = x.shape
    n_cols = w_in_b.shape[1]
    ret_cols = n_cols - lru_cols
    tm = min(ROWS_PER_STEP, T)
    bb = min(B, ROWS_PER_STEP // tm)
    body = functools.partial(_inproj_body, bb=bb, tm=tm, lru_cols=lru_cols, tn=512)
    return pl.pallas_call(
        body,
        grid=(B // bb, T // tm),
        in_specs=[pl.BlockSpec((bb, tm, D), lambda b, t: (b, t, 0)),
                  _const_spec((D, n_cols))],
        out_specs=[pl.BlockSpec((tm, bb * lru_cols), lambda b, t: (t, b)),
                   pl.BlockSpec((bb, tm, ret_cols), lambda b, t: (b, t, 0))],
        out_shape=[jax.ShapeDtypeStruct((T, B * lru_cols), F32),
                   jax.ShapeDtypeStruct((B, T, ret_cols), BF16)],
        compiler_params=_params(2),
        name="inproj",
    )(x, w_in_b)


def _lru_body(xl_ref, gl_ref, conv0_ref, h0_ref, cw_ref, cb_ref, wa_ref, ba_ref, wx_ref, bx_ref, lam_ref,
              after_ref, y_ref, conv_out_ref, h_out_ref, xp_s, a_s, b_s, h_s, *, tt, rows):
    del after_ref
    i = pl.program_id(0)
    B, W = h0_ref.shape
    nblk = wa_ref.shape[0]
    blk = W // nblk

    @pl.when(i == 0)
    def _():
        xp_s[0:CONV_WIDTH - 1] = conv0_ref[...]
        h_s[...] = h0_ref[...]

    xp_s[CONV_WIDTH - 1:] = xl_ref[...]

    lam = lam_ref[...]
    neg = -lam
    softplus = jnp.maximum(neg, 0.0) + jnp.log1p(jnp.exp(-jnp.abs(neg)))
    decay = (-LRU_C) * softplus

    def gates(c, carry):
        t0 = pl.multiple_of(c * rows, rows)
        xc = cb_ref[...].reshape(1, 1, W)
        for j in range(CONV_WIDTH):
            xc = xc + xp_s[pl.ds(t0 + j, rows)] * cw_ref[j:j + 1].reshape(1, 1, W)
        xc2 = xc.reshape(rows * B, W)
        xcb = xc2.astype(BF16)
        r_parts, i_parts = [], []
        for n in range(nblk):
            xb = xcb[:, n * blk:(n + 1) * blk]
            r_parts.append(jnp.dot(xb, wa_ref[n], preferred_element_type=F32))
            i_parts.append(jnp.dot(xb, wx_ref[n], preferred_element_type=F32))
        r = jax.nn.sigmoid(jnp.concatenate(r_parts, axis=1) + ba_ref[...])
        ig = jax.nn.sigmoid(jnp.concatenate(i_parts, axis=1) + bx_ref[...])
        a = jnp.exp(decay * r)
        bterm = jnp.sqrt(1.0 - a * a) * (ig * xc2)
        a_s[pl.ds(t0, rows)] = a.reshape(rows, B, W)
        b_s[pl.ds(t0, rows)] = bterm.reshape(rows, B, W)
        return carry

    lax.fori_loop(0, tt // rows, gates, 0)

    def step(t, h):
        hn = a_s[t] * h + b_s[t]
        y_ref[t] = hn * jax.nn.gelu(gl_ref[t])
        return hn

    h_last = lax.fori_loop(0, tt, step, h_s[...], unroll=8)
    h_s[...] = h_last
    tail = xp_s[tt:tt + CONV_WIDTH - 1]
    xp_s[0:CONV_WIDTH - 1] = tail
    conv_out_ref[...] = tail
    h_out_ref[...] = h_last


def _lru(z_lru3, conv0_tm, h0, conv_w, conv_b, wa_b, ba, wx_b, bx, lam, after):
    T, B, W2 = z_lru3.shape
    W = W2 // 2
    tt = min(LRU_TIME_TILE, T)
    rows = max(1, min(tt, 128 // B))
    body = functools.partial(_lru_body, tt=tt, rows=rows)
    nb = wa_b.shape[0]
    blk = W // nb
    return pl.pallas_call(
        body,
        grid=(T // tt,),
        in_specs=[pl.BlockSpec((tt, B, W), lambda t: (t, 0, 0)),
                  pl.BlockSpec((tt, B, W), lambda t: (t, 0, 1)),
                  _const_spec((CONV_WIDTH - 1, B, W)),
                  _const_spec((B, W)),
                  _const_spec((CONV_WIDTH, W)),
                  _const_spec((1, W)),
                  _const_spec((nb, blk, blk)),
                  _const_spec((1, W)),
                  _const_spec((nb, blk, blk)),
                  _const_spec((1, W)),
                  _const_spec((1, W)),
                  pl.BlockSpec(memory_space=pl.ANY)],
        out_specs=[pl.BlockSpec((tt, B, W), lambda t: (t, 0, 0)),
                   pl.BlockSpec((CONV_WIDTH - 1, B, W), lambda t: (0, 0, 0)),
                   pl.BlockSpec((B, W), lambda t: (0, 0))],
        out_shape=[jax.ShapeDtypeStruct((T, B, W), F32),
                   jax.ShapeDtypeStruct((CONV_WIDTH - 1, B, W), F32),
                   jax.ShapeDtypeStruct((B, W), F32)],
        scratch_shapes=[pltpu.VMEM((tt + CONV_WIDTH - 1, B, W), F32),
                        pltpu.VMEM((tt, B, W), F32),
                        pltpu.VMEM((tt, B, W), F32),
                        pltpu.VMEM((B, W), F32)],
        compiler_params=_params(1),
        name="rglru",
    )(z_lru3, z_lru3, conv0_tm, h0, conv_w, conv_b, wa_b, ba, wx_b, bx, lam, after)


def _ret_body(q_ref, k_ref, v_ref, g_ref, cos_ref, sin_ref, mask_ref, qdec_ref, kdec_ref, cdec_ref, gn_ref,
              s0_ref, y_ref, s_out_ref, *, rg, hb, dh):
    T = q_ref.shape[1]
    scale = dh ** -0.5

    def rope(t, cos, sin):
        return t * cos + pltpu.roll(t, dh // 2, axis=1) * sin

    for hh in range(hb):
        cols = slice(hh * dh, (hh + 1) * dh)

        def group(c, s, hh=hh, cols=cols):
            r0 = pl.multiple_of(c * rg, rg)
            rws = pl.ds(r0, rg)
            cos = cos_ref[rws, :]
            sin = sin_ref[rws, :]
            q = rope(q_ref[0, rws, cols].astype(F32), cos, sin)
            k = rope(k_ref[0, rws, cols].astype(F32), cos, sin) * scale
            v = v_ref[0, rws, cols]
            scores = lax.dot_general(q.astype(BF16), k.astype(BF16), (((1,), (1,)), ((), ())),
                                     preferred_element_type=F32)
            scores = scores * mask_ref[hh]
            o = jnp.dot(scores.astype(BF16), v, preferred_element_type=F32)
            o = o + jnp.dot((q * qdec_ref[hh]).astype(BF16), s.astype(BF16), preferred_element_type=F32)
            kd = (k * kdec_ref[hh]).astype(BF16)
            kv = lax.dot_general(kd, v, (((0,), (0,)), ((), ())), preferred_element_type=F32)
            s_new = cdec_ref[hh] * s + kv
            mu = jnp.mean(o, axis=-1, keepdims=True)
            oc = o - mu
            var = jnp.mean(oc * oc, axis=-1, keepdims=True)
            on = oc * lax.rsqrt(var + GN_EPS) * gn_ref[:, cols]
            g = g_ref[0, rws, cols].astype(F32)
            y_ref[0, rws, cols] = (g * jax.nn.sigmoid(g) * on).astype(y_ref.dtype)
            return s_new

        n_groups = T // rg
        s_out_ref[0, hh] = lax.fori_loop(0, n_groups, group, s0_ref[0, hh],
                                         unroll=4 if n_groups % 4 == 0 else 1)


def _retention_tables(T, pos0, chunk, rg, dh):
    half = dh // 2
    inv = ROPE_BASE ** (-jnp.arange(half, dtype=F32) / half)
    pos = pos0 + jnp.arange(T)
    ang = pos.astype(F32)[:, None] * inv[None, :]
    cos, sin = jnp.cos(ang), jnp.sin(ang)
    cos2 = jnp.concatenate([cos, cos], axis=1)
    sin2 = jnp.concatenate([-sin, sin], axis=1)
    log_g = jnp.log1p(-jnp.exp2(-5.0 - jnp.arange(RET_HEADS, dtype=F32)))[:, None, None]
    idx = jnp.arange(rg, dtype=F32)
    ci = jnp.floor(idx / chunk)
    diff = idx[:, None] - idx[None, :]
    same = ci[:, None] == ci[None, :]
    earlier = ci[None, :] < ci[:, None]
    dist = jnp.where(same, jnp.abs(diff), diff)
    mask = jnp.where(same | earlier, jnp.exp(dist[None] * log_g), 0.0)
    ones = jnp.ones((1, 1, dh), F32)
    qdec = jnp.exp((idx + 1.0)[None, :, None] * log_g) * ones
    kdec = jnp.exp((rg - 1.0 - idx)[None, :, None] * log_g) * ones
    cdec = jnp.exp(rg * log_g) * ones
    return cos2, sin2, mask, qdec, kdec, cdec


def _retention(z_ret, s0, gn, pos0, chunk):
    B, T, C4 = z_ret.shape
    H = RET_HEADS
    dh = C4 // (4 * H)
    rg = min(T, max(chunk, (RET_GROUP_ROWS // chunk) * chunk))
    hb = H if T * H * dh <= 64 * 1024 else 1
    nh = H // hb
    cos2, sin2, mask, qdec, kdec, cdec = _retention_tables(T, pos0, chunk, rg, dh)
    body = functools.partial(_ret_body, rg=rg, hb=hb, dh=dh)
    col = lambda off: (lambda b, h: (b, 0, off * nh + h))
    return pl.pallas_call(
        body,
        grid=(B, nh),
        in_specs=[pl.BlockSpec((1, T, hb * dh), col(0)),
                  pl.BlockSpec((1, T, hb * dh), col(1)),
                  pl.BlockSpec((1, T, hb * dh), col(2)),
                  pl.BlockSpec((1, T, hb * dh), col(3)),
                  _const_spec((T, dh)),
                  _const_spec((T, dh)),
                  pl.BlockSpec((hb, rg, rg), lambda b, h: (h, 0, 0)),
                  pl.BlockSpec((hb, rg, dh), lambda b, h: (h, 0, 0)),
                  pl.BlockSpec((hb, rg, dh), lambda b, h: (h, 0, 0)),
                  pl.BlockSpec((hb, 1, dh), lambda b, h: (h, 0, 0)),
                  pl.BlockSpec((1, hb * dh), lambda b, h: (0, h)),
                  pl.BlockSpec((1, hb, dh, dh), lambda b, h: (b, h, 0, 0))],
        out_specs=[pl.BlockSpec((1, T, hb * dh), lambda b, h: (b, 0, h)),
                   pl.BlockSpec((1, hb, dh, dh), lambda b, h: (b, h, 0, 0))],
        out_shape=[jax.ShapeDtypeStruct((B, T, H * dh), BF16),
                   jax.ShapeDtypeStruct((B, H, dh, dh), F32)],
        compiler_params=_params(2),
        name="retention",
    )(z_ret, z_ret, z_ret, z_ret, cos2, sin2, mask, qdec, kdec, cdec, gn, s0)


def _seg_allreduce(v, lane, op):
    for s in (1, 2, 4):
        up = pltpu.roll(v, LANES - s, axis=1)
        dn = pltpu.roll(v, s, axis=1)
        v = op(v, jnp.where((lane & s) == 0, up, dn))
    return v


def _mix_body(*refs, bb, tm, alpha, cap, n_steps, aliased):
    (yl_ref, yr_ref, x_ref, wo_ref, g1_ref, b1_ref, rw_ref, rb_ref, sg_ref, su_ref, sd_ref,
     tri_ref, cnt_in_ref) = refs[:13]
    refs = refs[14:] if aliased else refs[13:]
    base_ref, dest_ref, wts_ref, cnt_ref, xs_ref, carry_s, xpk_s, dv_s, ds_s, row_sems, idx_sem = refs
    i = pl.program_id(0)
    m = bb * tm
    d = x_ref.shape[-1]
    w = yl_ref.shape[1] // bb
    slot = i % 2
    chunk = m // ISSUE_CHUNKS

    def row_copy(sl, n, dst):
        return pltpu.make_async_copy(xpk_s.at[sl, n], xs_ref.at[dst], row_sems.at[sl])

    def issue_rows(sl, lo, hi):
        def one(n, c):
            for kk in range(TOP_K):
                row_copy(sl, n, ds_s[sl, kk, n]).start()
            return c
        lax.fori_loop(lo, hi, one, 0)

    def issue_prev_chunk(c):
        @pl.when(i > 0)
        def _():
            issue_rows(1 - slot, c * chunk, (c + 1) * chunk)

    def drain(sl):
        def one(n, c):
            for _ in range(DRAIN_UNROLL * TOP_K):
                row_copy(sl, 0, 0).wait()
            return c
        lax.fori_loop(0, m // DRAIN_UNROLL, one, 0)

    @pl.when(i == 0)
    def _():
        carry_s[...] = jnp.zeros_like(carry_s)
        carry_s[0:1, :] = cnt_in_ref[...].astype(F32)

    issue_prev_chunk(0)
    yl = jnp.concatenate([yl_ref[:, b * w:(b + 1) * w] for b in range(bb)], axis=0).astype(BF16)
    yr = yr_ref[...].reshape(m, yr_ref.shape[-1])
    mix = jnp.dot(yl, wo_ref[:w], preferred_element_type=F32)
    mix = mix + jnp.dot(yr, wo_ref[w:], preferred_element_type=F32)
    x1 = _layer_norm(alpha * x_ref[...].reshape(m, d) + mix, g1_ref[...], b1_ref[...])
    x1b = x1.astype(BF16)
    issue_prev_chunk(1)

    logits = jnp.dot(x1b, rw_ref[...], preferred_element_type=F32)

    s = jax.nn.sigmoid(logits)
    sb = s + rb_ref[...]
    lane = lax.broadcasted_iota(I32, (m, LANES), 1)
    e_id = lane & (N_EXPERTS - 1)
    e_f = e_id.astype(F32)
    low = lane < N_EXPERTS
    e_low = jnp.where(low, e_f, -1.0)
    grp = e_id >> 3
    big = jnp.float32(1e9)
    ninf = jnp.float32(-jnp.inf)

    m1 = _seg_allreduce(sb, lane, jnp.maximum)
    first_max = _seg_allreduce(jnp.where(sb == m1, e_f, big), lane, jnp.minimum)
    m2 = _seg_allreduce(jnp.where(e_f == first_max, ninf, sb), lane, jnp.maximum)
    gs = m1 + m2
    hg = jnp.dot(x1b, sg_ref[...], preferred_element_type=F32)
    issue_prev_chunk(2)
    rank = jnp.zeros((m, LANES), F32)
    for dgrp in range(1, N_GROUPS):
        other = pltpu.roll(gs, 8 * dgrp, axis=1)
        tie = jnp.where(grp >= dgrp, 1.0, 0.0)
        rank = rank + jnp.where(other > gs, 1.0, jnp.where(other == gs, tie, 0.0))
    v = jnp.where(rank < TOPK_GROUPS, jnp.where(low, sb, ninf), ninf)
    hu = jnp.dot(x1b, su_ref[...], preferred_element_type=F32)
    hs = (hg * jax.nn.sigmoid(hg) * hu).astype(BF16)

    idx_cols, w_cols = [], []
    sel = jnp.zeros((m, LANES), F32)
    for rnd in range(TOP_K):
        mx = jnp.max(v, axis=1, keepdims=True)
        idx = jnp.min(jnp.where(v == mx, e_f, big), axis=1, keepdims=True)
        hit = e_low == idx
        w_cols.append(jnp.sum(jnp.where(hit, s, 0.0), axis=1, keepdims=True))
        idx_cols.append(idx)
        v = jnp.where(hit, ninf, v)
        sel = jnp.where(hit, 1.0, sel)
        if rnd == TOP_K // 2 - 1:
            base_ref[...] = alpha * x1 + jnp.dot(hs, sd_ref[...], preferred_element_type=F32)
            issue_prev_chunk(3)

    packed = _pack_bf16_pair(x1)
    for sub in range(ROW_SUBLANES):
        xpk_s[slot, :, sub, :] = packed[:, sub * LANES:(sub + 1) * LANES]

    cum = jnp.dot(tri_ref[...], sel.astype(BF16), preferred_element_type=F32) + carry_s[0:1, :]
    carry_s[0:1, :] = carry_s[0:1, :] + jnp.sum(sel, axis=0, keepdims=True)
    cnt_ref[...] = carry_s[0:1, :].astype(I32)

    wsum = w_cols[0]
    for c in w_cols[1:]:
        wsum = wsum + c
    d_out = jnp.zeros((m, LANES), F32)
    w_out = jnp.zeros((m, LANES), F32)
    for kk in range(TOP_K):
        hit = e_low == idx_cols[kk]
        pk = jnp.sum(jnp.where(hit, cum, 0.0), axis=1, keepdims=True)
        d_out = jnp.where(lane == kk, idx_cols[kk] * float(cap) + pk, d_out)
        w_out = jnp.where(lane == kk, w_cols[kk] / wsum * ROUTED_SCALE, w_out)
    wts_ref[...] = w_out[:, :TOP_K]
    dest_t = jnp.transpose(d_out)[:TOP_K].astype(I32)
    dest_ref[0] = dest_t
    dv_s[...] = dest_t
    to_smem = pltpu.make_async_copy(dv_s, ds_s.at[slot], idx_sem)
    to_smem.start()
    to_smem.wait()

    @pl.when(i > 0)
    def _():
        drain(1 - slot)

    @pl.when(i == n_steps - 1)
    def _():
        issue_rows(slot, 0, m)
        drain(slot)


def _mix(y_lru2, y_ret, x, w, alpha, cnt_in, xs, cap):
    B, T, D = x.shape
    assert D == 2 * ROW_SUBLANES * LANES, "a packed token row must fill exactly one (8, 128) tile"
    W = y_ret.shape[-1]
    tm = min(MIX_ROWS, T)
    bb = min(B, MIX_ROWS // tm)
    m = bb * tm
    n = B * T
    nt = T // tm
    n_steps = (B // bb) * nt
    hs = w["sg"].shape[1]
    aliased = xs is not None
    tri = (lax.broadcasted_iota(I32, (m, m), 1) < lax.broadcasted_iota(I32, (m, m), 0)).astype(BF16)
    body = functools.partial(_mix_body, bb=bb, tm=tm, alpha=alpha, cap=cap, n_steps=n_steps, aliased=aliased)
    in_specs = [pl.BlockSpec((tm, bb * W), lambda i: (i % nt, i // nt)),
                pl.BlockSpec((bb, tm, W), lambda i: (i // nt, i % nt, 0)),
                pl.BlockSpec((bb, tm, D), lambda i: (i // nt, i % nt, 0)),
                _const_spec((2 * W, D)),
                _const_spec((1, D)),
                _const_spec((1, D)),
                _const_spec((D, LANES)),
                _const_spec((1, LANES)),
                _const_spec((D, hs)),
                _const_spec((D, hs)),
                _const_spec((hs, D)),
                _const_spec((m, m)),
                _const_spec((1, LANES))]
    args = [y_lru2, y_ret, x, w["wo"], w["g1"], w["b1"], w["rw"], w["rb"], w["sg"], w["su"], w["sd"],
            tri, cnt_in]
    if aliased:
        in_specs.append(pl.BlockSpec(memory_space=pl.ANY))
        args.append(xs)
    return pl.pallas_call(
        body,
        grid=(n_steps,),
        in_specs=in_specs,
        out_specs=[pl.BlockSpec((m, D), lambda i: (i, 0)),
                   pl.BlockSpec((1, TOP_K, m), lambda i: (i, 0, 0)),
                   pl.BlockSpec((m, TOP_K), lambda i: (i, 0)),
                   pl.BlockSpec((1, LANES), lambda i: (0, 0)),
                   pl.BlockSpec(memory_space=pl.ANY)],
        out_shape=[jax.ShapeDtypeStruct((n, D), F32),
                   jax.ShapeDtypeStruct((n_steps, TOP_K, m), I32),
                   jax.ShapeDtypeStruct((n, TOP_K), F32),
                   jax.ShapeDtypeStruct((1, LANES), I32),
                   jax.ShapeDtypeStruct((N_EXPERTS * cap, ROW_SUBLANES, LANES), U32)],
        scratch_shapes=[pltpu.VMEM((8, LANES), F32),
                        pltpu.VMEM((2, m, ROW_SUBLANES, LANES), U32),
                        pltpu.VMEM((TOP_K, m), I32),
                        pltpu.SMEM((2, TOP_K, m), I32),
                        pltpu.SemaphoreType.DMA((2,)),
                        pltpu.SemaphoreType.DMA(())],
        input_output_aliases={13: 4} if aliased else {},
        compiler_params=_params(1),
        name="mix_router",
    )(*args)


def _gmm_body(ge_ref, gr_ref, gn_ref, gt_ref, gx_ref, gs_ref, xs_ref, wg_ref, wu_ref, wd_ref, ys_ref,
              wgf_s, wuf_s, wdf_s, wgu_s, wd_s, x_s, y_s, w_sems, in_sems, out_sems, *, tm, n_items):
    i = pl.program_id(0)
    hid = wg_ref.shape[2]
    slot = i % 2

    def weight_copies(expert, sl):
        return [pltpu.make_async_copy(wg_ref.at[expert], wgf_s.at[sl], w_sems.at[sl]),
                pltpu.make_async_copy(wu_ref.at[expert], wuf_s.at[sl], w_sems.at[sl]),
                pltpu.make_async_copy(wd_ref.at[expert], wdf_s.at[sl], w_sems.at[sl])]

    def tile_copies(to_vmem, item, sl):
        r0 = pl.multiple_of(gr_ref[item] * tm, tm)
        out = []
        for sub in range(ROW_SUBLANES):
            cols = pl.ds(sub * LANES, LANES)
            if to_vmem:
                out.append(pltpu.make_async_copy(xs_ref.at[pl.ds(r0, tm), sub, :], x_s.at[sl, :, cols],
                                                 in_sems.at[sl]))
            else:
                out.append(pltpu.make_async_copy(y_s.at[sl, :, cols], ys_ref.at[pl.ds(r0, tm), sub, :],
                                                 out_sems.at[sl]))
        return out

    @pl.when(i == 0)
    def _():
        for cp in weight_copies(ge_ref[0], 0):
            cp.start()
        for cp in tile_copies(True, 0, 0):
            cp.start()
        y_s[...] = jnp.zeros_like(y_s)

    nxt = jnp.minimum(i + 1, n_items - 1)

    @pl.when((i + 1 < n_items) & (gn_ref[nxt] > 0))
    def _():
        for cp in tile_copies(True, nxt, 1 - slot):
            cp.start()

    e = ge_ref[i]
    e_prev = ge_ref[jnp.maximum(i - 1, 0)]

    @pl.when((i == 0) | (e != e_prev))
    def _():
        wsl = gs_ref[i]
        nxt_e = gx_ref[i]

        @pl.when(nxt_e >= 0)
        def _():
            for cp in weight_copies(nxt_e, 1 - wsl):
                cp.start()

        for cp in weight_copies(e, wsl):
            cp.wait()
        wgu_s[:, :hid] = wgf_s[wsl].astype(BF16)
        wgu_s[:, hid:] = wuf_s[wsl].astype(BF16)
        wd_s[...] = wdf_s[wsl].astype(BF16)

    n_valid = gn_ref[i]

    @pl.when(n_valid > 0)
    def _():
        for cp in tile_copies(True, i, slot):
            cp.wait()

        @pl.when(i >= 2)
        def _():
            for cp in tile_copies(False, i, slot):
                cp.wait()

        def expert_mlp(rows):
            pk = x_s[slot, :rows]
            valid = lax.broadcasted_iota(I32, pk.shape, 0) < n_valid
            xa, xb = _unpack_bf16_pair(jnp.where(valid, pk, jnp.uint32(0)))
            x = jnp.concatenate([xa.astype(BF16), xb.astype(BF16)], axis=1)
            h2 = jnp.dot(x, wgu_s[...], preferred_element_type=F32)
            hg = h2[:, :hid]
            h = (hg * jax.nn.sigmoid(hg) * h2[:, hid:]).astype(BF16)
            y_s[slot, :rows] = _pack_bf16_pair(jnp.dot(h, wd_s[...], preferred_element_type=F32))

        @pl.when(n_valid > tm // 2)
        def _():
            expert_mlp(tm)

        @pl.when(n_valid <= tm // 2)
        def _():
            expert_mlp(tm // 2)

        for cp in tile_copies(False, i, slot):
            cp.start()

    @pl.when(i == n_items - 1)
    def _():
        total = gt_ref[0]

        @pl.when(total >= 2)
        def _():
            for cp in tile_copies(False, 0, total % 2):
                cp.wait()

        for cp in tile_copies(False, 0, (total + 1) % 2):
            cp.wait()


def _gmm_metadata(counts, cap, tm, n_items):
    e = counts.shape[0]
    tiles = (counts + tm - 1) // tm
    item_end = jnp.cumsum(tiles)
    total = item_end[-1]
    it = jnp.arange(n_items, dtype=I32)
    itc = jnp.minimum(it, total - 1)
    ge = jnp.sum((item_end[None, :] <= itc[:, None]).astype(I32), axis=1)
    onehot = ge[:, None] == jnp.arange(e, dtype=I32)[None, :]
    start = jnp.sum(jnp.where(onehot, (item_end - tiles)[None, :], 0), axis=1)
    cnt = jnp.sum(jnp.where(onehot, counts[None, :], 0), axis=1)
    j = itc - start
    gr = ge * (cap // tm) + j
    gn = jnp.where(it < total, jnp.clip(cnt - j * tm, 0, tm), 0)
    ids = jnp.arange(e, dtype=I32)
    live = tiles > 0
    later = live[None, :] & (ids[None, :] > ids[:, None])
    next_e = jnp.min(jnp.where(later, ids[None, :], e), axis=1)
    next_e = jnp.where(next_e < e, next_e, -1)
    wslot = (jnp.cumsum(live.astype(I32)) - 1) % 2
    gx = jnp.sum(jnp.where(onehot, next_e[None, :], 0), axis=1)
    gs = jnp.sum(jnp.where(onehot, wslot[None, :], 0), axis=1)
    return tuple(a.astype(I32) for a in (ge, gr, gn, total.reshape(1), gx, gs))


def _gmm(xs, counts, cap, n_tokens, wg, wu, wd):
    tm = GMM_ROWS
    e, d, hid = wg.shape
    n_items = (n_tokens * TOP_K) // tm + e
    meta = _gmm_metadata(counts, cap, tm, n_items)
    body = functools.partial(_gmm_body, tm=tm, n_items=n_items)
    grid_spec = pltpu.PrefetchScalarGridSpec(
        num_scalar_prefetch=6,
        grid=(n_items,),
        in_specs=[pl.BlockSpec(memory_space=pl.ANY)] * 4,
        out_specs=pl.BlockSpec(memory_space=pl.ANY),
        scratch_shapes=[pltpu.VMEM((2, d, hid), F32), pltpu.VMEM((2, d, hid), F32),
                        pltpu.VMEM((2, hid, d), F32),
                        pltpu.VMEM((d, 2 * hid), BF16), pltpu.VMEM((hid, d), BF16),
                        pltpu.VMEM((2, tm, d // 2), U32), pltpu.VMEM((2, tm, d // 2), U32),
                        pltpu.SemaphoreType.DMA((2,)), pltpu.SemaphoreType.DMA((2,)),
                        pltpu.SemaphoreType.DMA((2,))],
    )
    return pl.pallas_call(
        body,
        grid_spec=grid_spec,
        out_shape=jax.ShapeDtypeStruct(xs.shape, U32),
        compiler_params=_params(1),
        name="expert_gmm",
    )(*meta, xs, wg, wu, wd)


def _final_body(d0_ref, dn_ref, wts_ref, base_ref, p_ref, ys_ref, pg_ref, pb_ref, pp_ref, g2_ref, b2_ref,
                out_ref, rows_s, hi_s, lo_s, wrep_s, sems, *, bb, tm, n_steps):
    i = pl.program_id(0)
    m = bb * tm
    d = base_ref.shape[1]
    slot = i % 2

    def row_copy(sl, d_row, kk, n):
        return pltpu.make_async_copy(ys_ref.at[d_row], rows_s.at[sl, kk, n], sems.at[sl])

    chunk = m // ISSUE_CHUNKS

    def issue(sl, dref, lo, hi):
        def one(n, c):
            for kk in range(TOP_K):
                row_copy(sl, dref[0, kk, n], kk, n).start()
            return c
        lax.fori_loop(lo, hi, one, 0)

    def issue_next_chunk(c):
        @pl.when(i + 1 < n_steps)
        def _():
            issue(1 - slot, dn_ref, c * chunk, (c + 1) * chunk)

    @pl.when(i == 0)
    def _():
        issue(0, d0_ref, 0, m)

    def drain(n, c):
        for _ in range(DRAIN_UNROLL * TOP_K):
            row_copy(slot, 0, 0, 0).wait()
        return c

    lax.fori_loop(0, m // DRAIN_UNROLL, drain, 0)

    wts = wts_ref[...]
    for kk in range(TOP_K):
        wk = jnp.broadcast_to(wts[:, kk:kk + 1], (m, LANES))
        wrep_s[kk] = _pack_bf16_pair(jnp.concatenate([wk, wk], axis=1))

    def combine(n, c):
        acc = jnp.zeros((2 * ROW_SUBLANES, LANES), BF16)
        for kk in range(TOP_K):
            row = pltpu.bitcast(rows_s[slot, kk, n], BF16)
            wk = jnp.broadcast_to(wrep_s[kk, pl.ds(n, 1), :], (ROW_SUBLANES, LANES))
            acc = acc + pltpu.bitcast(wk, BF16) * row
        hi, lo = _unpack_bf16_pair(pltpu.bitcast(acc, U32))
        hi_s[n] = hi
        lo_s[n] = lo
        return c

    issue_next_chunk(0)
    lax.fori_loop(0, m // 2, combine, 0, unroll=2)
    issue_next_chunk(1)
    lax.fori_loop(m // 2, m, combine, 0, unroll=2)
    issue_next_chunk(2)
    routed = jnp.concatenate([hi_s[:, sub, :] for sub in range(ROW_SUBLANES)]
                             + [lo_s[:, sub, :] for sub in range(ROW_SUBLANES)], axis=1)
    x2 = _layer_norm(base_ref[...] + routed, g2_ref[...], b2_ref[...])
    issue_next_chunk(3)
    gate = jax.nn.sigmoid(jnp.dot(x2.astype(BF16), pg_ref[...], preferred_element_type=F32) + pb_ref[...])
    proj = jnp.dot(p_ref[...].reshape(m, p_ref.shape[-1]).astype(BF16), pp_ref[...], preferred_element_type=F32)
    out_ref[...] = (x2 + gate * proj).reshape(bb, tm, d)


def _final(base, dest, wts, p, ys, w, B, T):
    n, D = base.shape
    tm = min(MIX_ROWS, T)
    bb = min(B, MIX_ROWS // tm)
    m = bb * tm
    nt = T // tm
    n_steps = (B // bb) * nt
    pd = p.shape[-1]
    body = functools.partial(_final_body, bb=bb, tm=tm, n_steps=n_steps)
    return pl.pallas_call(
        body,
        grid=(n_steps,),
        in_specs=[pl.BlockSpec((1, TOP_K, m), lambda i: (0, 0, 0), memory_space=pltpu.SMEM),
                  pl.BlockSpec((1, TOP_K, m), lambda i: (jnp.minimum(i + 1, n_steps - 1), 0, 0),
                               memory_space=pltpu.SMEM),
                  pl.BlockSpec((m, TOP_K), lambda i: (i, 0)),
                  pl.BlockSpec((m, D), lambda i: (i, 0)),
                  pl.BlockSpec((bb, tm, pd), lambda i: (i // nt, i % nt, 0)),
                  pl.BlockSpec(memory_space=pl.ANY),
                  _const_spec((D, D)),
                  _const_spec((1, D)),
                  _const_spec((pd, D)),
                  _const_spec((1, D)),
                  _const_spec((1, D))],
        out_specs=pl.BlockSpec((bb, tm, D), lambda i: (i // nt, i % nt, 0)),
        out_shape=jax.ShapeDtypeStruct((B, T, D), F32),
        scratch_shapes=[pltpu.VMEM((2, TOP_K, m, ROW_SUBLANES, LANES), U32),
                        pltpu.VMEM((m, ROW_SUBLANES, LANES), F32),
                        pltpu.VMEM((m, ROW_SUBLANES, LANES), F32),
                        pltpu.VMEM((TOP_K, m, LANES), U32),
                        pltpu.SemaphoreType.DMA((2,))],
        compiler_params=_params(1),
        name="combine_final",
    )(dest, dest, wts, base, p, ys, w["pg"], w["pb"], w["pp"], w["g2"], w["b2"])


def _prep_layer(prm):
    (w_in, conv_w, conv_b, lru_wa, lru_ba, lru_wx, lru_bx, lru_lambda, ret_gn, w_out, ln1_g, ln1_b,
     router_w, router_b, e_gate, e_up, e_down, s_gate, s_up, s_down, ln2_g, ln2_b,
     ple_w_proj, ple_w_gate, ple_b_gate) = prm
    row = lambda v: v.reshape(1, -1)
    rw2 = jnp.concatenate([router_w, router_w], axis=1).astype(BF16)
    return dict(
        w_in=w_in.astype(BF16), conv_w=conv_w, conv_b=row(conv_b), wa=lru_wa.astype(BF16), ba=row(lru_ba),
        wx=lru_wx.astype(BF16), bx=row(lru_bx), lam=row(lru_lambda), gn=row(ret_gn), wo=w_out.astype(BF16),
        g1=row(ln1_g), b1=row(ln1_b), rw=rw2, rb=row(jnp.concatenate([router_b, router_b])),
        e_gate=e_gate, e_up=e_up, e_down=e_down, sg=s_gate.astype(BF16), su=s_up.astype(BF16),
        sd=s_down.astype(BF16), g2=row(ln2_g), b2=row(ln2_b), pp=ple_w_proj.astype(BF16),
        pg=ple_w_gate.astype(BF16), pb=row(ple_b_gate))


def _mixers(x, conv_st, lru_st, ret_st, pos0, chunk, w):
    B, T, D = x.shape
    W = conv_st.shape[-1]
    z_lru, z_ret = _inproj(x, w["w_in"], 2 * W)
    y_ret, new_ret = _retention(z_ret, ret_st, w["gn"], pos0, chunk)
    y_lru, conv_tm, new_lru = _lru(z_lru.reshape(T, B, 2 * W), jnp.transpose(conv_st, (1, 0, 2)), lru_st,
                                   w["conv_w"], w["conv_b"], w["wa"], w["ba"], w["wx"], w["bx"], w["lam"],
                                   after=new_ret)
    return y_lru.reshape(T, B * W), y_ret, jnp.transpose(conv_tm, (1, 0, 2)), new_lru, new_ret


def kernel(x_prompt, x_sample, p_prompt, p_sample, state_conv, state_lru, state_ret, w_in, conv_w, conv_b,
           lru_wa, lru_ba, lru_wx, lru_bx, lru_lambda, ret_gn, w_out, ln1_g, ln1_b, router_w, router_b,
           exp_w_gate, exp_w_up, exp_w_down, sh_w_gate, sh_w_up, sh_w_down, ln2_g, ln2_b,
           ple_w_proj, ple_w_gate, ple_b_gate):
    depth = w_in.shape[0]
    alpha = (2 * depth) ** 0.25
    b_p, t_p, _ = x_prompt.shape
    b_s, t_s, _ = x_sample.shape
    W = state_conv.shape[-1]
    H, dh = state_ret.shape[2], state_ret.shape[3]
    n_tokens = b_p * t_p + b_s * t_s
    cap = -(-n_tokens // GMM_ROWS) * GMM_ROWS
    hp, hs = x_prompt, x_sample
    outs = [[] for _ in range(6)]
    for i in range(depth):
        prm = (w_in[i], conv_w[i], conv_b[i], lru_wa[i], lru_ba[i], lru_wx[i], lru_bx[i], lru_lambda[i],
               ret_gn[i], w_out[i], ln1_g[i], ln1_b[i], router_w[i], router_b[i], exp_w_gate[i], exp_w_up[i],
               exp_w_down[i], sh_w_gate[i], sh_w_up[i], sh_w_down[i], ln2_g[i], ln2_b[i],
               ple_w_proj[i], ple_w_gate[i], ple_b_gate[i])
        w = _prep_layer(prm)
        zc = jnp.zeros((b_p, CONV_WIDTH - 1, W), x_prompt.dtype)
        zl = jnp.zeros((b_p, W), F32)
        zr = jnp.zeros((b_p, H, dh, dh), F32)
        yl_p, yr_p, c_p, l_p, r_p = _mixers(hp, zc, zl, zr, 0, CHUNK, w)
        yl_s, yr_s, c_s, l_s, r_s = _mixers(hs, state_conv[i], state_lru[i], state_ret[i], PAST_LEN, t_s, w)
        for o, val in zip(outs, (c_p, l_p, r_p, c_s, l_s, r_s)):
            o.append(val)
        base_p, dest_p, wts_p, cnt_p, xs = _mix(yl_p, yr_p, hp, w, alpha, jnp.zeros((1, LANES), I32), None, cap)
        base_s, dest_s, wts_s, cnt_all, xs = _mix(yl_s, yr_s, hs, w, alpha, cnt_p, xs, cap)
        ys = _gmm(xs, cnt_all[0, :N_EXPERTS], cap, n_tokens, w["e_gate"], w["e_up"], w["e_down"])
        hp = _final(base_p, dest_p, wts_p, p_prompt[i], ys, w, b_p, t_p)
        hs = _final(base_s, dest_s, wts_s, p_sample[i], ys, w, b_s, t_s)
    return (hp, hs) + tuple(jnp.stack(o) for o in outs)
```

```python
import functools

import jax
import jax.numpy as jnp
from jax import lax
from jax.experimental import pallas as pl
from jax.experimental.pallas import tpu as pltpu

F32 = jnp.float32
BF16 = jnp.bfloat16
U32 = jnp.uint32
I32 = jnp.int32

CHUNK = 64
PAST_LEN = 1024
CONV_WIDTH = 4
LRU_C = 8.0
LRU_BLOCKS = 8
RET_HEADS = 8
ROPE_BASE = 10000.0
N_EXPERTS = 64
TOP_K = 8
N_GROUPS = 8
TOPK_GROUPS = 4
ROUTED_SCALE = 2.5
LN_EPS = 1e-5
GN_EPS = 1e-6

LANES = 128
ROW_SUBLANES = 8
DRAIN_UNROLL = 8
ISSUE_CHUNKS = 4
ROWS_PER_STEP = 512
INPROJ_COLS = 512
MIX_ROWS = 256
GMM_ROWS = 512
RET_GROUP_ROWS = 256
LRU_TIME_TILE = 64
VMEM_LIMIT = 56 * 1024 * 1024


def _const_spec(shape):
    zeros = (0,) * len(shape)
    return pl.BlockSpec(shape, lambda *_: zeros, pipeline_mode=pl.Buffered(1))


def _params(n_axes):
    return pltpu.CompilerParams(dimension_semantics=("arbitrary",) * n_axes,
                                vmem_limit_bytes=VMEM_LIMIT)


def _layer_norm(x, g, b):
    mu = jnp.mean(x, axis=-1, keepdims=True)
    xc = x - mu
    var = jnp.mean(xc * xc, axis=-1, keepdims=True)
    return xc * lax.rsqrt(var + LN_EPS) * g + b


def _pack_bf16_pair(x):
    c = x.shape[1] // 2
    xb = x.astype(BF16).astype(F32)
    hi = pltpu.bitcast(xb[:, :c], U32)
    lo = pltpu.bitcast(xb[:, c:], U32)
    return hi | (lo >> 16)


def _unpack_bf16_pair(pk):
    hi = pltpu.bitcast(pk & jnp.uint32(0xFFFF0000), F32)
    lo = pltpu.bitcast(pk << 16, F32)
    return hi, lo


def _inproj_body(x_ref, w_ref, zl_ref, zr_ref, *, bb, tm, lru_cols, tn):
    d = x_ref.shape[-1]
    x = x_ref[...].reshape(bb * tm, d).astype(BF16)
    for j in range(w_ref.shape[1] // tn):
        c0 = j * tn
        acc = jnp.dot(x, w_ref[:, c0:c0 + tn], preferred_element_type=F32)
        for b in range(bb):
            rows = acc[b * tm:(b + 1) * tm]
            if c0 < lru_cols:
                zl_ref[:, b * lru_cols + c0:b * lru_cols + c0 + tn] = rows
            else:
                zr_ref[b, :, c0 - lru_cols:c0 - lru_cols + tn] = rows.astype(BF16)


def _inproj_cols_body(x_ref, w_ref, zl_ref, zr_ref, xb_s, *, n_lru):
    j = pl.program_id(0)
    bb, tm, d = x_ref.shape

    @pl.when(j == 0)
    def _():
        xb_s[...] = x_ref[...].reshape(bb * tm, d).astype(BF16)

    acc = jnp.dot(xb_s[...], w_ref[...], preferred_element_type=F32)

    @pl.when(j < n_lru)
    def _():
        for b in range(bb):
            zl_ref[:, b, :] = acc[b * tm:(b + 1) * tm]

    @pl.when(j >= n_lru)
    def _():
        zr_ref[...] = acc.reshape(zr_ref.shape).astype(BF16)


def _inproj_cols(x, w_in_b, lru_cols):
    B, T, D = x.shape
    n_cols = w_in_b.shape[1]
    tn = INPROJ_COLS
    n_lru = lru_cols // tn
    body = functools.partial(_inproj_cols_body, n_lru=n_lru)
    return pl.pallas_call(
        body,
        grid=(n_cols // tn,),
        in_specs=[_const_spec((B, T, D)),
                  pl.BlockSpec((D, tn), lambda j: (0, j))],
        out_specs=[pl.BlockSpec((T, B, tn), lambda j: (0, 0, jnp.minimum(j, n_lru - 1))),
                   pl.BlockSpec((B, T, tn), lambda j: (0, 0, jnp.maximum(j - n_lru, 0)))],
        out_shape=[jax.ShapeDtypeStruct((T, B, lru_cols), F32),
                   jax.ShapeDtypeStruct((B, T, n_cols - lru_cols), BF16)],
        scratch_shapes=[pltpu.VMEM((B * T, D), BF16)],
        compiler_params=_params(1),
        name="inproj",
    )(x, w_in_b)


def _inproj(x, w_in_b, lru_cols):
    B, T, D = x.shape
    if B * T <= ROWS_PER_STEP:
        return _inproj_cols(x, w_in_b, lru_cols)
    z_lru, z_ret = _inproj_rows(x, w_in_b, lru_cols)
    return z_lru.reshape(T, B, lru_cols), z_ret


def _inproj_rows(x, w_in_b, lru_cols):
    B, T, D = x.shape
    n_cols = w_in_b.shape[1]
    ret_cols = n_cols - lru_cols
    tm = min(ROWS_PER_STEP, T)
    bb = min(B, ROWS_PER_STEP // tm)
    body = functools.partial(_inproj_body, bb=bb, tm=tm, lru_cols=lru_cols, tn=INPROJ_COLS)
    return pl.pallas_call(
        body,
        grid=(B // bb, T // tm),
        in_specs=[pl.BlockSpec((bb, tm, D), lambda b, t: (b, t, 0)),
                  _const_spec((D, n_cols))],
        out_specs=[pl.BlockSpec((tm, bb * lru_cols), lambda b, t: (t, b)),
                   pl.BlockSpec((bb, tm, ret_cols), lambda b, t: (b, t, 0))],
        out_shape=[jax.ShapeDtypeStruct((T, B * lru_cols), F32),
                   jax.ShapeDtypeStruct((B, T, ret_cols), BF16)],
        compiler_params=_params(2),
        name="inproj",
    )(x, w_in_b)


def _lru_body(xl_ref, gl_ref, conv0_ref, h0_ref, cw_ref, cb_ref, wa_ref, ba_ref, wx_ref, bx_ref, lam_ref,
              after_ref, y_ref, conv_out_ref, h_out_ref, xp_s, a_s, b_s, h_s, *, tt, rows):
    del after_ref
    i = pl.program_id(0)
    B, W = h0_ref.shape
    nblk = wa_ref.shape[0]
    blk = W // nblk

    @pl.when(i == 0)
    def _():
        xp_s[0:CONV_WIDTH - 1] = conv0_ref[...]
        h_s[...] = h0_ref[...]

    xp_s[CONV_WIDTH - 1:] = xl_ref[...]

    lam = lam_ref[...]
    neg = -lam
    softplus = jnp.maximum(neg, 0.0) + jnp.log1p(jnp.exp(-jnp.abs(neg)))
    decay = (-LRU_C) * softplus

    def gates(c, carry):
        t0 = pl.multiple_of(c * rows, rows)
        xc = cb_ref[...].reshape(1, 1, W)
        for j in range(CONV_WIDTH):
            xc = xc + xp_s[pl.ds(t0 + j, rows)] * cw_ref[j:j + 1].reshape(1, 1, W)
        xc2 = xc.reshape(rows * B, W)
        xcb = xc2.astype(BF16)
        r_parts, i_parts = [], []
        for n in range(nblk):
            xb = xcb[:, n * blk:(n + 1) * blk]
            r_parts.append(jnp.dot(xb, wa_ref[n], preferred_element_type=F32))
            i_parts.append(jnp.dot(xb, wx_ref[n], preferred_element_type=F32))
        r = jax.nn.sigmoid(jnp.concatenate(r_parts, axis=1) + ba_ref[...])
        ig = jax.nn.sigmoid(jnp.concatenate(i_parts, axis=1) + bx_ref[...])
        a = jnp.exp(decay * r)
        bterm = jnp.sqrt(1.0 - a * a) * (ig * xc2)
        a_s[pl.ds(t0, rows)] = a.reshape(rows, B, W)
        b_s[pl.ds(t0, rows)] = bterm.reshape(rows, B, W)
        return carry

    lax.fori_loop(0, tt // rows, gates, 0)

    def step(t, h):
        hn = a_s[t] * h + b_s[t]
        y_ref[t] = hn * jax.nn.gelu(gl_ref[t])
        return hn

    h_last = lax.fori_loop(0, tt, step, h_s[...], unroll=8)
    h_s[...] = h_last
    tail = xp_s[tt:tt + CONV_WIDTH - 1]
    xp_s[0:CONV_WIDTH - 1] = tail
    conv_out_ref[...] = tail
    h_out_ref[...] = h_last


def _lru(z_lru3, conv0_tm, h0, conv_w, conv_b, wa_b, ba, wx_b, bx, lam, after):
    T, B, W2 = z_lru3.shape
    W = W2 // 2
    tt = min(LRU_TIME_TILE, T)
    rows = max(1, min(tt, 128 // B))
    body = functools.partial(_lru_body, tt=tt, rows=rows)
    nb = wa_b.shape[0]
    blk = W // nb
    return pl.pallas_call(
        body,
        grid=(T // tt,),
        in_specs=[pl.BlockSpec((tt, B, W), lambda t: (t, 0, 0)),
                  pl.BlockSpec((tt, B, W), lambda t: (t, 0, 1)),
                  _const_spec((CONV_WIDTH - 1, B, W)),
                  _const_spec((B, W)),
                  _const_spec((CONV_WIDTH, W)),
                  _const_spec((1, W)),
                  _const_spec((nb, blk, blk)),
                  _const_spec((1, W)),
                  _const_spec((nb, blk, blk)),
                  _const_spec((1, W)),
                  _const_spec((1, W)),
                  pl.BlockSpec(memory_space=pl.ANY)],
        out_specs=[pl.BlockSpec((tt, B, W), lambda t: (t, 0, 0)),
                   pl.BlockSpec((CONV_WIDTH - 1, B, W), lambda t: (0, 0, 0)),
                   pl.BlockSpec((B, W), lambda t: (0, 0))],
        out_shape=[jax.ShapeDtypeStruct((T, B, W), F32),
                   jax.ShapeDtypeStruct((CONV_WIDTH - 1, B, W), F32),
                   jax.ShapeDtypeStruct((B, W), F32)],
        scratch_shapes=[pltpu.VMEM((tt + CONV_WIDTH - 1, B, W), F32),
                        pltpu.VMEM((tt, B, W), F32),
                        pltpu.VMEM((tt, B, W), F32),
                        pltpu.VMEM((B, W), F32)],
        compiler_params=_params(1),
        name="rglru",
    )(z_lru3, z_lru3, conv0_tm, h0, conv_w, conv_b, wa_b, ba, wx_b, bx, lam, after)


def _ret_body(q_ref, k_ref, v_ref, g_ref, cos_ref, sin_ref, mask_ref, qdec_ref, kdec_ref, cdec_ref, gn_ref,
              s0_ref, y_ref, s_out_ref, *, rg, hb, dh):
    T = q_ref.shape[1]
    scale = dh ** -0.5

    def rope(t, cos, sin):
        return t * cos + pltpu.roll(t, dh // 2, axis=1) * sin

    for hh in range(hb):
        cols = slice(hh * dh, (hh + 1) * dh)

        def group(c, s, hh=hh, cols=cols):
            r0 = pl.multiple_of(c * rg, rg)
            rws = pl.ds(r0, rg)
            cos = cos_ref[rws, :]
            sin = sin_ref[rws, :]
            q = rope(q_ref[0, rws, cols].astype(F32), cos, sin)
            k = rope(k_ref[0, rws, cols].astype(F32), cos, sin) * scale
            v = v_ref[0, rws, cols]
            scores = lax.dot_general(q.astype(BF16), k.astype(BF16), (((1,), (1,)), ((), ())),
                                     preferred_element_type=F32)
            scores = scores * mask_ref[hh]
            o = jnp.dot(scores.astype(BF16), v, preferred_element_type=F32)
            o = o + jnp.dot((q * qdec_ref[hh]).astype(BF16), s.astype(BF16), preferred_element_type=F32)
            kd = (k * kdec_ref[hh]).astype(BF16)
            kv = lax.dot_general(kd, v, (((0,), (0,)), ((), ())), preferred_element_type=F32)
            s_new = cdec_ref[hh] * s + kv
            mu = jnp.mean(o, axis=-1, keepdims=True)
            oc = o - mu
            var = jnp.mean(oc * oc, axis=-1, keepdims=True)
            on = oc * lax.rsqrt(var + GN_EPS) * gn_ref[:, cols]
            g = g_ref[0, rws, cols].astype(F32)
            y_ref[0, rws, cols] = (g * jax.nn.sigmoid(g) * on).astype(y_ref.dtype)
            return s_new

        n_groups = T // rg
        s_out_ref[0, hh] = lax.fori_loop(0, n_groups, group, s0_ref[0, hh],
                                         unroll=4 if n_groups % 4 == 0 else 1)


def _retention_tables(T, pos0, chunk, rg, dh):
    half = dh // 2
    inv = ROPE_BASE ** (-jnp.arange(half, dtype=F32) / half)
    pos = pos0 + jnp.arange(T)
    ang = pos.astype(F32)[:, None] * inv[None, :]
    cos, sin = jnp.cos(ang), jnp.sin(ang)
    cos2 = jnp.concatenate([cos, cos], axis=1)
    sin2 = jnp.concatenate([-sin, sin], axis=1)
    log_g = jnp.log1p(-jnp.exp2(-5.0 - jnp.arange(RET_HEADS, dtype=F32)))[:, None, None]
    idx = jnp.arange(rg, dtype=F32)
    ci = jnp.floor(idx / chunk)
    diff = idx[:, None] - idx[None, :]
    same = ci[:, None] == ci[None, :]
    earlier = ci[None, :] < ci[:, None]
    dist = jnp.where(same, jnp.abs(diff), diff)
    mask = jnp.where(same | earlier, jnp.exp(dist[None] * log_g), 0.0)
    ones = jnp.ones((1, 1, dh), F32)
    qdec = jnp.exp((idx + 1.0)[None, :, None] * log_g) * ones
    kdec = jnp.exp((rg - 1.0 - idx)[None, :, None] * log_g) * ones
    cdec = jnp.exp(rg * log_g) * ones
    return cos2, sin2, mask, qdec, kdec, cdec


def _retention(z_ret, s0, gn, pos0, chunk):
    B, T, C4 = z_ret.shape
    H = RET_HEADS
    dh = C4 // (4 * H)
    rg = min(T, max(chunk, (RET_GROUP_ROWS // chunk) * chunk))
    hb = H if T * H * dh <= 64 * 1024 else 1
    nh = H // hb
    cos2, sin2, mask, qdec, kdec, cdec = _retention_tables(T, pos0, chunk, rg, dh)
    body = functools.partial(_ret_body, rg=rg, hb=hb, dh=dh)
    col = lambda off: (lambda b, h: (b, 0, off * nh + h))
    return pl.pallas_call(
        body,
        grid=(B, nh),
        in_specs=[pl.BlockSpec((1, T, hb * dh), col(0)),
                  pl.BlockSpec((1, T, hb * dh), col(1)),
                  pl.BlockSpec((1, T, hb * dh), col(2)),
                  pl.BlockSpec((1, T, hb * dh), col(3)),
                  _const_spec((T, dh)),
                  _const_spec((T, dh)),
                  pl.BlockSpec((hb, rg, rg), lambda b, h: (h, 0, 0)),
                  pl.BlockSpec((hb, rg, dh), lambda b, h: (h, 0, 0)),
                  pl.BlockSpec((hb, rg, dh), lambda b, h: (h, 0, 0)),
                  pl.BlockSpec((hb, 1, dh), lambda b, h: (h, 0, 0)),
                  pl.BlockSpec((1, hb * dh), lambda b, h: (0, h)),
                  pl.BlockSpec((1, hb, dh, dh), lambda b, h: (b, h, 0, 0))],
        out_specs=[pl.BlockSpec((1, T, hb * dh), lambda b, h: (b, 0, h)),
                   pl.BlockSpec((1, hb, dh, dh), lambda b, h: (b, h, 0, 0))],
        out_shape=[jax.ShapeDtypeStruct((B, T, H * dh), BF16),
                   jax.ShapeDtypeStruct((B, H, dh, dh), F32)],
        compiler_params=_params(2),
        name="retention",
    )(z_ret, z_ret, z_ret, z_ret, cos2, sin2, mask, qdec, kdec, cdec, gn, s0)


def _seg_allreduce(v, lane, op):
    for s in (1, 2, 4):
        up = pltpu.roll(v, LANES - s, axis=1)
        dn = pltpu.roll(v, s, axis=1)
        v = op(v, jnp.where((lane & s) == 0, up, dn))
    return v


def _mix_body(*refs, bb, tm, alpha, cap, n_steps, aliased):
    (yl_ref, yr_ref, x_ref, wo_ref, g1_ref, b1_ref, rw_ref, rb_ref, sg_ref, su_ref, sd_ref,
     tri_ref, cnt_in_ref) = refs[:13]
    refs = refs[14:] if aliased else refs[13:]
    base_ref, dest_ref, wts_ref, cnt_ref, xs_ref, carry_s, xpk_s, dv_s, ds_s, row_sems, idx_sem = refs
    i = pl.program_id(0)
    m = bb * tm
    d = x_ref.shape[-1]
    w = yl_ref.shape[1] // bb
    slot = i % 2
    chunk = m // ISSUE_CHUNKS

    def row_copy(sl, n, dst):
        return pltpu.make_async_copy(xpk_s.at[sl, n], xs_ref.at[dst], row_sems.at[sl])

    def issue_rows(sl, lo, hi):
        def one(n, c):
            for kk in range(TOP_K):
                row_copy(sl, n, ds_s[sl, kk, n]).start()
            return c
        lax.fori_loop(lo, hi, one, 0)

    def issue_prev_chunk(c):
        @pl.when(i > 0)
        def _():
            issue_rows(1 - slot, c * chunk, (c + 1) * chunk)

    def drain(sl):
        def one(n, c):
            for _ in range(DRAIN_UNROLL * TOP_K):
                row_copy(sl, 0, 0).wait()
            return c
        lax.fori_loop(0, m // DRAIN_UNROLL, one, 0)

    @pl.when(i == 0)
    def _():
        carry_s[...] = jnp.zeros_like(carry_s)
        carry_s[0:1, :] = cnt_in_ref[...].astype(F32)

    issue_prev_chunk(0)
    yl = jnp.concatenate([yl_ref[:, b * w:(b + 1) * w] for b in range(bb)], axis=0).astype(BF16)
    yr = yr_ref[...].reshape(m, yr_ref.shape[-1])
    mix = jnp.dot(jnp.concatenate([yl, yr], axis=1), wo_ref[...], preferred_element_type=F32)
    x1 = _layer_norm(alpha * x_ref[...].reshape(m, d) + mix, g1_ref[...], b1_ref[...])
    x1b = x1.astype(BF16)
    issue_prev_chunk(1)

    logits = jnp.dot(x1b, rw_ref[...], preferred_element_type=F32)

    s = jax.nn.sigmoid(logits)
    sb = s + rb_ref[...]
    lane = lax.broadcasted_iota(I32, (m, LANES), 1)
    e_id = lane & (N_EXPERTS - 1)
    e_f = e_id.astype(F32)
    low = lane < N_EXPERTS
    e_low = jnp.where(low, e_f, -1.0)
    grp = e_id >> 3
    big = jnp.float32(1e9)
    ninf = jnp.float32(-jnp.inf)

    m1 = _seg_allreduce(sb, lane, jnp.maximum)
    first_max = _seg_allreduce(jnp.where(sb == m1, e_f, big), lane, jnp.minimum)
    m2 = _seg_allreduce(jnp.where(e_f == first_max, ninf, sb), lane, jnp.maximum)
    gs = m1 + m2
    hg = jnp.dot(x1b, sg_ref[...], preferred_element_type=F32)
    issue_prev_chunk(2)
    rank = jnp.zeros((m, LANES), F32)
    for dgrp in range(1, N_GROUPS):
        other = pltpu.roll(gs, 8 * dgrp, axis=1)
        tie = jnp.where(grp >= dgrp, 1.0, 0.0)
        rank = rank + jnp.where(other > gs, 1.0, jnp.where(other == gs, tie, 0.0))
    v = jnp.where(rank < TOPK_GROUPS, jnp.where(low, sb, ninf), ninf)
    hu = jnp.dot(x1b, su_ref[...], preferred_element_type=F32)
    hs = (hg * jax.nn.sigmoid(hg) * hu).astype(BF16)

    idx_cols, w_cols = [], []
    sel = jnp.zeros((m, LANES), F32)
    for rnd in range(TOP_K):
        mx = jnp.max(v, axis=1, keepdims=True)
        idx = jnp.min(jnp.where(v == mx, e_f, big), axis=1, keepdims=True)
        hit = e_low == idx
        w_cols.append(jnp.sum(jnp.where(hit, s, 0.0), axis=1, keepdims=True))
        idx_cols.append(idx)
        v = jnp.where(hit, ninf, v)
        sel = jnp.where(hit, 1.0, sel)
        if rnd == TOP_K // 2 - 1:
            base_ref[...] = alpha * x1 + jnp.dot(hs, sd_ref[...], preferred_element_type=F32)
            issue_prev_chunk(3)

    packed = _pack_bf16_pair(x1)
    for sub in range(ROW_SUBLANES):
        xpk_s[slot, :, sub, :] = packed[:, sub * LANES:(sub + 1) * LANES]

    cum = jnp.dot(tri_ref[...], sel.astype(BF16), preferred_element_type=F32) + carry_s[0:1, :]
    carry_s[0:1, :] = carry_s[0:1, :] + jnp.sum(sel, axis=0, keepdims=True)
    cnt_ref[...] = carry_s[0:1, :].astype(I32)

    wsum = w_cols[0]
    for c in w_cols[1:]:
        wsum = wsum + c
    d_out = jnp.zeros((m, LANES), F32)
    w_out = jnp.zeros((m, LANES), F32)
    for kk in range(TOP_K):
        hit = e_low == idx_cols[kk]
        pk = jnp.sum(jnp.where(hit, cum, 0.0), axis=1, keepdims=True)
        d_out = jnp.where(lane == kk, idx_cols[kk] * float(cap) + pk, d_out)
        w_out = jnp.where(lane == kk, w_cols[kk] / wsum * ROUTED_SCALE, w_out)
    wts_ref[...] = w_out[:, :TOP_K]
    dest_t = jnp.transpose(d_out)[:TOP_K].astype(I32)
    dest_ref[0] = dest_t
    dv_s[...] = dest_t
    to_smem = pltpu.make_async_copy(dv_s, ds_s.at[slot], idx_sem)
    to_smem.start()
    to_smem.wait()

    @pl.when(i > 0)
    def _():
        drain(1 - slot)

    @pl.when(i == n_steps - 1)
    def _():
        issue_rows(slot, 0, m)
        drain(slot)


def _mix(y_lru2, y_ret, x, w, alpha, cnt_in, xs, cap):
    B, T, D = x.shape
    assert D == 2 * ROW_SUBLANES * LANES, "a packed token row must fill exactly one (8, 128) tile"
    W = y_ret.shape[-1]
    tm = min(MIX_ROWS, T)
    bb = min(B, MIX_ROWS // tm)
    m = bb * tm
    n = B * T
    nt = T // tm
    n_steps = (B // bb) * nt
    hs = w["sg"].shape[1]
    aliased = xs is not None
    tri = (lax.broadcasted_iota(I32, (m, m), 1) < lax.broadcasted_iota(I32, (m, m), 0)).astype(BF16)
    body = functools.partial(_mix_body, bb=bb, tm=tm, alpha=alpha, cap=cap, n_steps=n_steps, aliased=aliased)
    in_specs = [pl.BlockSpec((tm, bb * W), lambda i: (i % nt, i // nt)),
                pl.BlockSpec((bb, tm, W), lambda i: (i // nt, i % nt, 0)),
                pl.BlockSpec((bb, tm, D), lambda i: (i // nt, i % nt, 0)),
                _const_spec((2 * W, D)),
                _const_spec((1, D)),
                _const_spec((1, D)),
                _const_spec((D, LANES)),
                _const_spec((1, LANES)),
                _const_spec((D, hs)),
                _const_spec((D, hs)),
                _const_spec((hs, D)),
                _const_spec((m, m)),
                _const_spec((1, LANES))]
    args = [y_lru2, y_ret, x, w["wo"], w["g1"], w["b1"], w["rw"], w["rb"], w["sg"], w["su"], w["sd"],
            tri, cnt_in]
    if aliased:
        in_specs.append(pl.BlockSpec(memory_space=pl.ANY))
        args.append(xs)
    return pl.pallas_call(
        body,
        grid=(n_steps,),
        in_specs=in_specs,
        out_specs=[pl.BlockSpec((m, D), lambda i: (i, 0)),
                   pl.BlockSpec((1, TOP_K, m), lambda i: (i, 0, 0)),
                   pl.BlockSpec((m, TOP_K), lambda i: (i, 0)),
                   pl.BlockSpec((1, LANES), lambda i: (0, 0)),
                   pl.BlockSpec(memory_space=pl.ANY)],
        out_shape=[jax.ShapeDtypeStruct((n, D), F32),
                   jax.ShapeDtypeStruct((n_steps, TOP_K, m), I32),
                   jax.ShapeDtypeStruct((n, TOP_K), F32),
                   jax.ShapeDtypeStruct((1, LANES), I32),
                   jax.ShapeDtypeStruct((N_EXPERTS * cap, ROW_SUBLANES, LANES), U32)],
        scratch_shapes=[pltpu.VMEM((8, LANES), F32),
                        pltpu.VMEM((2, m, ROW_SUBLANES, LANES), U32),
                        pltpu.VMEM((TOP_K, m), I32),
                        pltpu.SMEM((2, TOP_K, m), I32),
                        pltpu.SemaphoreType.DMA((2,)),
                        pltpu.SemaphoreType.DMA(())],
        input_output_aliases={13: 4} if aliased else {},
        compiler_params=_params(1),
        name="mix_router",
    )(*args)


def _gmm_body(ge_ref, gr_ref, gn_ref, gt_ref, gx_ref, gs_ref, xs_ref, wg_ref, wu_ref, wd_ref, ys_ref,
              wgf_s, wuf_s, wdf_s, wgu_s, wd_s, x_s, y_s, w_sems, in_sems, out_sems, *, tm, n_items):
    i = pl.program_id(0)
    hid = wg_ref.shape[2]
    slot = i % 2

    def weight_copies(expert, sl):
        return [pltpu.make_async_copy(wg_ref.at[expert], wgf_s.at[sl], w_sems.at[sl]),
                pltpu.make_async_copy(wu_ref.at[expert], wuf_s.at[sl], w_sems.at[sl]),
                pltpu.make_async_copy(wd_ref.at[expert], wdf_s.at[sl], w_sems.at[sl])]

    def tile_copies(to_vmem, item, sl):
        r0 = pl.multiple_of(gr_ref[item] * tm, tm)
        out = []
        for sub in range(ROW_SUBLANES):
            cols = pl.ds(sub * LANES, LANES)
            if to_vmem:
                out.append(pltpu.make_async_copy(xs_ref.at[pl.ds(r0, tm), sub, :], x_s.at[sl, :, cols],
                                                 in_sems.at[sl]))
            else:
                out.append(pltpu.make_async_copy(y_s.at[sl, :, cols], ys_ref.at[pl.ds(r0, tm), sub, :],
                                                 out_sems.at[sl]))
        return out

    @pl.when(i == 0)
    def _():
        for cp in weight_copies(ge_ref[0], 0):
            cp.start()
        for cp in tile_copies(True, 0, 0):
            cp.start()
        y_s[...] = jnp.zeros_like(y_s)

    nxt = jnp.minimum(i + 1, n_items - 1)

    @pl.when((i + 1 < n_items) & (gn_ref[nxt] > 0))
    def _():
        for cp in tile_copies(True, nxt, 1 - slot):
            cp.start()

    e = ge_ref[i]
    e_prev = ge_ref[jnp.maximum(i - 1, 0)]

    @pl.when((i == 0) | (e != e_prev))
    def _():
        wsl = gs_ref[i]
        nxt_e = gx_ref[i]

        @pl.when(nxt_e >= 0)
        def _():
            for cp in weight_copies(nxt_e, 1 - wsl):
                cp.start()

        for cp in weight_copies(e, wsl):
            cp.wait()
        wgu_s[:, :hid] = wgf_s[wsl].astype(BF16)
        wgu_s[:, hid:] = wuf_s[wsl].astype(BF16)
        wd_s[...] = wdf_s[wsl].astype(BF16)

    n_valid = gn_ref[i]

    @pl.when(n_valid > 0)
    def _():
        for cp in tile_copies(True, i, slot):
            cp.wait()

        @pl.when(i >= 2)
        def _():
            for cp in tile_copies(False, i, slot):
                cp.wait()

        def expert_mlp(rows):
            pk = x_s[slot, :rows]
            valid = lax.broadcasted_iota(I32, pk.shape, 0) < n_valid
            xa, xb = _unpack_bf16_pair(jnp.where(valid, pk, jnp.uint32(0)))
            x = jnp.concatenate([xa.astype(BF16), xb.astype(BF16)], axis=1)
            h2 = jnp.dot(x, wgu_s[...], preferred_element_type=F32)
            hg = h2[:, :hid]
            h = (hg * jax.nn.sigmoid(hg) * h2[:, hid:]).astype(BF16)
            y_s[slot, :rows] = _pack_bf16_pair(jnp.dot(h, wd_s[...], preferred_element_type=F32))

        @pl.when(n_valid > tm // 2)
        def _():
            expert_mlp(tm)

        @pl.when(n_valid <= tm // 2)
        def _():
            expert_mlp(tm // 2)

        for cp in tile_copies(False, i, slot):
            cp.start()

    @pl.when(i == n_items - 1)
    def _():
        total = gt_ref[0]

        @pl.when(total >= 2)
        def _():
            for cp in tile_copies(False, 0, total % 2):
                cp.wait()

        for cp in tile_copies(False, 0, (total + 1) % 2):
            cp.wait()


def _gmm_metadata(counts, cap, tm, n_items):
    e = counts.shape[0]
    tiles = (counts + tm - 1) // tm
    item_end = jnp.cumsum(tiles)
    total = item_end[-1]
    it = jnp.arange(n_items, dtype=I32)
    itc = jnp.minimum(it, total - 1)
    ge = jnp.sum((item_end[None, :] <= itc[:, None]).astype(I32), axis=1)
    onehot = ge[:, None] == jnp.arange(e, dtype=I32)[None, :]
    start = jnp.sum(jnp.where(onehot, (item_end - tiles)[None, :], 0), axis=1)
    cnt = jnp.sum(jnp.where(onehot, counts[None, :], 0), axis=1)
    j = itc - start
    gr = ge * (cap // tm) + j
    gn = jnp.where(it < total, jnp.clip(cnt - j * tm, 0, tm), 0)
    ids = jnp.arange(e, dtype=I32)
    live = tiles > 0
    later = live[None, :] & (ids[None, :] > ids[:, None])
    next_e = jnp.min(jnp.where(later, ids[None, :], e), axis=1)
    next_e = jnp.where(next_e < e, next_e, -1)
    wslot = (jnp.cumsum(live.astype(I32)) - 1) % 2
    gx = jnp.sum(jnp.where(onehot, next_e[None, :], 0), axis=1)
    gs = jnp.sum(jnp.where(onehot, wslot[None, :], 0), axis=1)
    return tuple(a.astype(I32) for a in (ge, gr, gn, total.reshape(1), gx, gs))


def _gmm(xs, counts, cap, n_tokens, wg, wu, wd):
    tm = GMM_ROWS
    e, d, hid = wg.shape
    n_items = (n_tokens * TOP_K) // tm + e
    meta = _gmm_metadata(counts, cap, tm, n_items)
    body = functools.partial(_gmm_body, tm=tm, n_items=n_items)
    grid_spec = pltpu.PrefetchScalarGridSpec(
        num_scalar_prefetch=6,
        grid=(n_items,),
        in_specs=[pl.BlockSpec(memory_space=pl.ANY)] * 4,
        out_specs=pl.BlockSpec(memory_space=pl.ANY),
        scratch_shapes=[pltpu.VMEM((2, d, hid), F32), pltpu.VMEM((2, d, hid), F32),
                        pltpu.VMEM((2, hid, d), F32),
                        pltpu.VMEM((d, 2 * hid), BF16), pltpu.VMEM((hid, d), BF16),
                        pltpu.VMEM((2, tm, d // 2), U32), pltpu.VMEM((2, tm, d // 2), U32),
                        pltpu.SemaphoreType.DMA((2,)), pltpu.SemaphoreType.DMA((2,)),
                        pltpu.SemaphoreType.DMA((2,))],
    )
    return pl.pallas_call(
        body,
        grid_spec=grid_spec,
        out_shape=jax.ShapeDtypeStruct(xs.shape, U32),
        compiler_params=_params(1),
        name="expert_gmm",
    )(*meta, xs, wg, wu, wd)


def _final_body(d0_ref, dn_ref, wts_ref, base_ref, p_ref, ys_ref, pg_ref, pb_ref, pp_ref, g2_ref, b2_ref,
                out_ref, rows_s, sum_s, wrep_s, sems, *, bb, tm, n_steps):
    i = pl.program_id(0)
    m = bb * tm
    d = base_ref.shape[1]
    slot = i % 2

    def row_copy(sl, d_row, kk, n):
        return pltpu.make_async_copy(ys_ref.at[d_row], rows_s.at[sl, kk, n], sems.at[sl])

    chunk = m // ISSUE_CHUNKS

    def issue(sl, dref, lo, hi):
        def one(n, c):
            for kk in range(TOP_K):
                row_copy(sl, dref[0, kk, n], kk, n).start()
            return c
        lax.fori_loop(lo, hi, one, 0)

    def issue_next_chunk(c):
        @pl.when(i + 1 < n_steps)
        def _():
            issue(1 - slot, dn_ref, c * chunk, (c + 1) * chunk)

    @pl.when(i == 0)
    def _():
        issue(0, d0_ref, 0, m)

    def drain(n, c):
        for _ in range(DRAIN_UNROLL * TOP_K):
            row_copy(slot, 0, 0, 0).wait()
        return c

    lax.fori_loop(0, m // DRAIN_UNROLL, drain, 0)

    wts = wts_ref[...]
    for kk in range(TOP_K):
        wk = jnp.broadcast_to(wts[:, kk:kk + 1], (m, LANES))
        wrep_s[kk] = _pack_bf16_pair(jnp.concatenate([wk, wk], axis=1))

    def combine(n, c):
        acc = jnp.zeros((2 * ROW_SUBLANES, LANES), BF16)
        for kk in range(TOP_K):
            row = pltpu.bitcast(rows_s[slot, kk, n], BF16)
            wk = jnp.broadcast_to(wrep_s[kk, pl.ds(n, 1), :], (ROW_SUBLANES, LANES))
            acc = acc + pltpu.bitcast(wk, BF16) * row
        sum_s[n] = pltpu.bitcast(acc, U32)
        return c

    issue_next_chunk(0)
    lax.fori_loop(0, m // 2, combine, 0, unroll=2)
    issue_next_chunk(1)
    lax.fori_loop(m // 2, m, combine, 0, unroll=2)
    issue_next_chunk(2)
    routed = jnp.concatenate(_unpack_bf16_pair(
        jnp.concatenate([sum_s[:, sub, :] for sub in range(ROW_SUBLANES)], axis=1)), axis=1)
    x2 = _layer_norm(base_ref[...] + routed, g2_ref[...], b2_ref[...])
    issue_next_chunk(3)
    gate = jax.nn.sigmoid(jnp.dot(x2.astype(BF16), pg_ref[...], preferred_element_type=F32) + pb_ref[...])
    proj = jnp.dot(p_ref[...].reshape(m, p_ref.shape[-1]).astype(BF16), pp_ref[...], preferred_element_type=F32)
    out_ref[...] = (x2 + gate * proj).reshape(bb, tm, d)


def _final(base, dest, wts, p, ys, w, B, T):
    n, D = base.shape
    tm = min(MIX_ROWS, T)
    bb = min(B, MIX_ROWS // tm)
    m = bb * tm
    nt = T // tm
    n_steps = (B // bb) * nt
    pd = p.shape[-1]
    body = functools.partial(_final_body, bb=bb, tm=tm, n_steps=n_steps)
    return pl.pallas_call(
        body,
        grid=(n_steps,),
        in_specs=[pl.BlockSpec((1, TOP_K, m), lambda i: (0, 0, 0), memory_space=pltpu.SMEM),
                  pl.BlockSpec((1, TOP_K, m), lambda i: (jnp.minimum(i + 1, n_steps - 1), 0, 0),
                               memory_space=pltpu.SMEM),
                  pl.BlockSpec((m, TOP_K), lambda i: (i, 0)),
                  pl.BlockSpec((m, D), lambda i: (i, 0)),
                  pl.BlockSpec((bb, tm, pd), lambda i: (i // nt, i % nt, 0)),
                  pl.BlockSpec(memory_space=pl.ANY),
                  _const_spec((D, D)),
                  _const_spec((1, D)),
                  _const_spec((pd, D)),
                  _const_spec((1, D)),
                  _const_spec((1, D))],
        out_specs=pl.BlockSpec((bb, tm, D), lambda i: (i // nt, i % nt, 0)),
        out_shape=jax.ShapeDtypeStruct((B, T, D), F32),
        scratch_shapes=[pltpu.VMEM((2, TOP_K, m, ROW_SUBLANES, LANES), U32),
                        pltpu.VMEM((m, ROW_SUBLANES, LANES), U32),
                        pltpu.VMEM((TOP_K, m, LANES), U32),
                        pltpu.SemaphoreType.DMA((2,))],
        compiler_params=_params(1),
        name="combine_final",
    )(dest, dest, wts, base, p, ys, w["pg"], w["pb"], w["pp"], w["g2"], w["b2"])


def _prep_layer(prm):
    (w_in, conv_w, conv_b, lru_wa, lru_ba, lru_wx, lru_bx, lru_lambda, ret_gn, w_out, ln1_g, ln1_b,
     router_w, router_b, e_gate, e_up, e_down, s_gate, s_up, s_down, ln2_g, ln2_b,
     ple_w_proj, ple_w_gate, ple_b_gate) = prm
    row = lambda v: v.reshape(1, -1)
    rw2 = jnp.concatenate([router_w, router_w], axis=1).astype(BF16)
    return dict(
        w_in=w_in.astype(BF16), conv_w=conv_w, conv_b=row(conv_b), wa=lru_wa.astype(BF16), ba=row(lru_ba),
        wx=lru_wx.astype(BF16), bx=row(lru_bx), lam=row(lru_lambda), gn=row(ret_gn), wo=w_out.astype(BF16),
        g1=row(ln1_g), b1=row(ln1_b), rw=rw2, rb=row(jnp.concatenate([router_b, router_b])),
        e_gate=e_gate, e_up=e_up, e_down=e_down, sg=s_gate.astype(BF16), su=s_up.astype(BF16),
        sd=s_down.astype(BF16), g2=row(ln2_g), b2=row(ln2_b), pp=ple_w_proj.astype(BF16),
        pg=ple_w_gate.astype(BF16), pb=row(ple_b_gate))


def _mixers(x, conv_st, lru_st, ret_st, pos0, chunk, w):
    B, T, D = x.shape
    W = conv_st.shape[-1]
    z_lru, z_ret = _inproj(x, w["w_in"], 2 * W)
    y_ret, new_ret = _retention(z_ret, ret_st, w["gn"], pos0, chunk)
    y_lru, conv_tm, new_lru = _lru(z_lru, jnp.transpose(conv_st, (1, 0, 2)), lru_st,
                                   w["conv_w"], w["conv_b"], w["wa"], w["ba"], w["wx"], w["bx"], w["lam"],
                                   after=new_ret)
    return y_lru.reshape(T, B * W), y_ret, jnp.transpose(conv_tm, (1, 0, 2)), new_lru, new_ret


def kernel(x_prompt, x_sample, p_prompt, p_sample, state_conv, state_lru, state_ret, w_in, conv_w, conv_b,
           lru_wa, lru_ba, lru_wx, lru_bx, lru_lambda, ret_gn, w_out, ln1_g, ln1_b, router_w, router_b,
           exp_w_gate, exp_w_up, exp_w_down, sh_w_gate, sh_w_up, sh_w_down, ln2_g, ln2_b,
           ple_w_proj, ple_w_gate, ple_b_gate):
    depth = w_in.shape[0]
    alpha = (2 * depth) ** 0.25
    b_p, t_p, _ = x_prompt.shape
    b_s, t_s, _ = x_sample.shape
    W = state_conv.shape[-1]
    H, dh = state_ret.shape[2], state_ret.shape[3]
    n_tokens = b_p * t_p + b_s * t_s
    cap = -(-n_tokens // GMM_ROWS) * GMM_ROWS
    hp, hs = x_prompt, x_sample
    outs = [[] for _ in range(6)]
    for i in range(depth):
        prm = (w_in[i], conv_w[i], conv_b[i], lru_wa[i], lru_ba[i], lru_wx[i], lru_bx[i], lru_lambda[i],
               ret_gn[i], w_out[i], ln1_g[i], ln1_b[i], router_w[i], router_b[i], exp_w_gate[i], exp_w_up[i],
               exp_w_down[i], sh_w_gate[i], sh_w_up[i], sh_w_down[i], ln2_g[i], ln2_b[i],
               ple_w_proj[i], ple_w_gate[i], ple_b_gate[i])
        w = _prep_layer(prm)
        zc = jnp.zeros((b_p, CONV_WIDTH - 1, W), x_prompt.dtype)
        zl = jnp.zeros((b_p, W), F32)
        zr = jnp.zeros((b_p, H, dh, dh), F32)
        yl_p, yr_p, c_p, l_p, r_p = _mixers(hp, zc, zl, zr, 0, CHUNK, w)
        yl_s, yr_s, c_s, l_s, r_s = _mixers(hs, state_conv[i], state_lru[i], state_ret[i], PAST_LEN, t_s, w)
        for o, val in zip(outs, (c_p, l_p, r_p, c_s, l_s, r_s)):
            o.append(val)
        base_p, dest_p, wts_p, cnt_p, xs = _mix(yl_p, yr_p, hp, w, alpha, jnp.zeros((1, LANES), I32), None, cap)
        base_s, dest_s, wts_s, cnt_all, xs = _mix(yl_s, yr_s, hs, w, alpha, cnt_p, xs, cap)
        ys = _gmm(xs, cnt_all[0, :N_EXPERTS], cap, n_tokens, w["e_gate"], w["e_up"], w["e_down"])
        hp = _final(base_p, dest_p, wts_p, p_prompt[i], ys, w, b_p, t_p)
        hs = _final(base_s, dest_s, wts_s, p_sample[i], ys, w, b_s, t_s)
    return (hp, hs) + tuple(jnp.stack(o) for o in outs)
```

```python
import functools

import jax
import jax.numpy as jnp
from jax import lax
from jax.experimental import pallas as pl
from jax.experimental.pallas import tpu as pltpu

F32 = jnp.float32
BF16 = jnp.bfloat16
U32 = jnp.uint32
I32 = jnp.int32

CHUNK = 64
PAST_LEN = 1024
CONV_WIDTH = 4
LRU_C = 8.0
LRU_BLOCKS = 8
RET_HEADS = 8
ROPE_BASE = 10000.0
N_EXPERTS = 64
TOP_K = 8
N_GROUPS = 8
TOPK_GROUPS = 4
ROUTED_SCALE = 2.5
LN_EPS = 1e-5
GN_EPS = 1e-6

LANES = 128
ROW_SUBLANES = 8
DRAIN_UNROLL = 8
ISSUE_CHUNKS = 4
ROW_BUFFERS = 3
ROWS_PER_STEP = 512
INPROJ_COLS = 512
MIX_ROWS = 256
GMM_ROWS = 512
RET_GROUP_ROWS = 256
LRU_TIME_TILE = 64
VMEM_LIMIT = 56 * 1024 * 1024


def _const_spec(shape):
    zeros = (0,) * len(shape)
    return pl.BlockSpec(shape, lambda *_: zeros, pipeline_mode=pl.Buffered(1))


def _params(n_axes):
    return pltpu.CompilerParams(dimension_semantics=("arbitrary",) * n_axes,
                                vmem_limit_bytes=VMEM_LIMIT)


def _layer_norm(x, g, b):
    mu = jnp.mean(x, axis=-1, keepdims=True)
    xc = x - mu
    var = jnp.mean(xc * xc, axis=-1, keepdims=True)
    return xc * lax.rsqrt(var + LN_EPS) * g + b


def _pack_bf16_pair(x):
    c = x.shape[1] // 2
    xb = x.astype(BF16).astype(F32)
    hi = pltpu.bitcast(xb[:, :c], U32)
    lo = pltpu.bitcast(xb[:, c:], U32)
    return hi | (lo >> 16)


def _unpack_bf16_pair(pk):
    hi = pltpu.bitcast(pk & jnp.uint32(0xFFFF0000), F32)
    lo = pltpu.bitcast(pk << 16, F32)
    return hi, lo


def _inproj_body(x_ref, w_ref, zl_ref, zr_ref, *, bb, tm, lru_cols, tn):
    d = x_ref.shape[-1]
    x = x_ref[...].reshape(bb * tm, d).astype(BF16)
    for j in range(w_ref.shape[1] // tn):
        c0 = j * tn
        acc = jnp.dot(x, w_ref[:, c0:c0 + tn], preferred_element_type=F32)
        for b in range(bb):
            rows = acc[b * tm:(b + 1) * tm]
            if c0 < lru_cols:
                zl_ref[:, b * lru_cols + c0:b * lru_cols + c0 + tn] = rows
            else:
                zr_ref[b, :, c0 - lru_cols:c0 - lru_cols + tn] = rows.astype(BF16)


def _inproj_cols_body(x_ref, w_ref, zl_ref, zr_ref, xb_s, *, n_lru):
    j = pl.program_id(0)
    bb, tm, d = x_ref.shape

    @pl.when(j == 0)
    def _():
        xb_s[...] = x_ref[...].reshape(bb * tm, d).astype(BF16)

    acc = jnp.dot(xb_s[...], w_ref[...], preferred_element_type=F32)

    @pl.when(j < n_lru)
    def _():
        for b in range(bb):
            zl_ref[:, b, :] = acc[b * tm:(b + 1) * tm]

    @pl.when(j >= n_lru)
    def _():
        zr_ref[...] = acc.reshape(zr_ref.shape).astype(BF16)


def _inproj_cols(x, w_in_b, lru_cols):
    B, T, D = x.shape
    n_cols = w_in_b.shape[1]
    tn = INPROJ_COLS
    n_lru = lru_cols // tn
    body = functools.partial(_inproj_cols_body, n_lru=n_lru)
    return pl.pallas_call(
        body,
        grid=(n_cols // tn,),
        in_specs=[_const_spec((B, T, D)),
                  pl.BlockSpec((D, tn), lambda j: (0, j))],
        out_specs=[pl.BlockSpec((T, B, tn), lambda j: (0, 0, jnp.minimum(j, n_lru - 1))),
                   pl.BlockSpec((B, T, tn), lambda j: (0, 0, jnp.maximum(j - n_lru, 0)))],
        out_shape=[jax.ShapeDtypeStruct((T, B, lru_cols), F32),
                   jax.ShapeDtypeStruct((B, T, n_cols - lru_cols), BF16)],
        scratch_shapes=[pltpu.VMEM((B * T, D), BF16)],
        compiler_params=_params(1),
        name="inproj",
    )(x, w_in_b)


def _inproj(x, w_in_b, lru_cols):
    B, T, D = x.shape
    if B * T <= ROWS_PER_STEP:
        return _inproj_cols(x, w_in_b, lru_cols)
    z_lru, z_ret = _inproj_rows(x, w_in_b, lru_cols)
    return z_lru.reshape(T, B, lru_cols), z_ret


def _inproj_rows(x, w_in_b, lru_cols):
    B, T, D = x.shape
    n_cols = w_in_b.shape[1]
    ret_cols = n_cols - lru_cols
    tm = min(ROWS_PER_STEP, T)
    bb = min(B, ROWS_PER_STEP // tm)
    body = functools.partial(_inproj_body, bb=bb, tm=tm, lru_cols=lru_cols, tn=INPROJ_COLS)
    return pl.pallas_call(
        body,
        grid=(B // bb, T // tm),
        in_specs=[pl.BlockSpec((bb, tm, D), lambda b, t: (b, t, 0)),
                  _const_spec((D, n_cols))],
        out_specs=[pl.BlockSpec((tm, bb * lru_cols), lambda b, t: (t, b)),
                   pl.BlockSpec((bb, tm, ret_cols), lambda b, t: (b, t, 0))],
        out_shape=[jax.ShapeDtypeStruct((T, B * lru_cols), F32),
                   jax.ShapeDtypeStruct((B, T, ret_cols), BF16)],
        compiler_params=_params(2),
        name="inproj",
    )(x, w_in_b)


def _lru_body(xl_ref, gl_ref, conv0_ref, h0_ref, cw_ref, cb_ref, wa_ref, ba_ref, wx_ref, bx_ref, lam_ref,
              after_ref, y_ref, conv_out_ref, h_out_ref, xp_s, a_s, b_s, h_s, *, tt, rows):
    del after_ref
    i = pl.program_id(0)
    B, W = h0_ref.shape
    nblk = wa_ref.shape[0]
    blk = W // nblk

    @pl.when(i == 0)
    def _():
        xp_s[0:CONV_WIDTH - 1] = conv0_ref[...]
        h_s[...] = h0_ref[...]

    xp_s[CONV_WIDTH - 1:] = xl_ref[...]

    lam = lam_ref[...]
    neg = -lam
    softplus = jnp.maximum(neg, 0.0) + jnp.log1p(jnp.exp(-jnp.abs(neg)))
    decay = (-LRU_C) * softplus

    def gates(c, carry):
        t0 = pl.multiple_of(c * rows, rows)
        xc = cb_ref[...].reshape(1, 1, W)
        for j in range(CONV_WIDTH):
            xc = xc + xp_s[pl.ds(t0 + j, rows)] * cw_ref[j:j + 1].reshape(1, 1, W)
        xc2 = xc.reshape(rows * B, W)
        xcb = xc2.astype(BF16)
        r_parts, i_parts = [], []
        for n in range(nblk):
            xb = xcb[:, n * blk:(n + 1) * blk]
            r_parts.append(jnp.dot(xb, wa_ref[n], preferred_element_type=F32))
            i_parts.append(jnp.dot(xb, wx_ref[n], preferred_element_type=F32))
        r = jax.nn.sigmoid(jnp.concatenate(r_parts, axis=1) + ba_ref[...])
        ig = jax.nn.sigmoid(jnp.concatenate(i_parts, axis=1) + bx_ref[...])
        a = jnp.exp(decay * r)
        bterm = jnp.sqrt(1.0 - a * a) * (ig * xc2)
        a_s[pl.ds(t0, rows)] = a.reshape(rows, B, W)
        b_s[pl.ds(t0, rows)] = bterm.reshape(rows, B, W)
        return carry

    lax.fori_loop(0, tt // rows, gates, 0)

    def step(t, h):
        hn = a_s[t] * h + b_s[t]
        y_ref[t] = hn * jax.nn.gelu(gl_ref[t])
        return hn

    h_last = lax.fori_loop(0, tt, step, h_s[...], unroll=8)
    h_s[...] = h_last
    tail = xp_s[tt:tt + CONV_WIDTH - 1]
    xp_s[0:CONV_WIDTH - 1] = tail
    conv_out_ref[...] = tail
    h_out_ref[...] = h_last


def _lru(z_lru3, conv0_tm, h0, conv_w, conv_b, wa_b, ba, wx_b, bx, lam, after):
    T, B, W2 = z_lru3.shape
    W = W2 // 2
    tt = min(LRU_TIME_TILE, T)
    rows = max(1, min(tt, 128 // B))
    body = functools.partial(_lru_body, tt=tt, rows=rows)
    nb = wa_b.shape[0]
    blk = W // nb
    return pl.pallas_call(
        body,
        grid=(T // tt,),
        in_specs=[pl.BlockSpec((tt, B, W), lambda t: (t, 0, 0)),
                  pl.BlockSpec((tt, B, W), lambda t: (t, 0, 1)),
                  _const_spec((CONV_WIDTH - 1, B, W)),
                  _const_spec((B, W)),
                  _const_spec((CONV_WIDTH, W)),
                  _const_spec((1, W)),
                  _const_spec((nb, blk, blk)),
                  _const_spec((1, W)),
                  _const_spec((nb, blk, blk)),
                  _const_spec((1, W)),
                  _const_spec((1, W)),
                  pl.BlockSpec(memory_space=pl.ANY)],
        out_specs=[pl.BlockSpec((tt, B, W), lambda t: (t, 0, 0)),
                   pl.BlockSpec((CONV_WIDTH - 1, B, W), lambda t: (0, 0, 0)),
                   pl.BlockSpec((B, W), lambda t: (0, 0))],
        out_shape=[jax.ShapeDtypeStruct((T, B, W), F32),
                   jax.ShapeDtypeStruct((CONV_WIDTH - 1, B, W), F32),
                   jax.ShapeDtypeStruct((B, W), F32)],
        scratch_shapes=[pltpu.VMEM((tt + CONV_WIDTH - 1, B, W), F32),
                        pltpu.VMEM((tt, B, W), F32),
                        pltpu.VMEM((tt, B, W), F32),
                        pltpu.VMEM((B, W), F32)],
        compiler_params=_params(1),
        name="rglru",
    )(z_lru3, z_lru3, conv0_tm, h0, conv_w, conv_b, wa_b, ba, wx_b, bx, lam, after)


def _ret_body(q_ref, k_ref, v_ref, g_ref, cos_ref, sin_ref, mask_ref, qdec_ref, kdec_ref, cdec_ref, gn_ref,
              s0_ref, y_ref, s_out_ref, *, rg, hb, dh):
    T = q_ref.shape[1]
    scale = dh ** -0.5

    def rope(t, cos, sin):
        return t * cos + pltpu.roll(t, dh // 2, axis=1) * sin

    for hh in range(hb):
        cols = slice(hh * dh, (hh + 1) * dh)

        def group(c, s, hh=hh, cols=cols):
            r0 = pl.multiple_of(c * rg, rg)
            rws = pl.ds(r0, rg)
            cos = cos_ref[rws, :]
            sin = sin_ref[rws, :]
            q = rope(q_ref[0, rws, cols].astype(F32), cos, sin)
            k = rope(k_ref[0, rws, cols].astype(F32), cos, sin) * scale
            v = v_ref[0, rws, cols]
            scores = lax.dot_general(q.astype(BF16), k.astype(BF16), (((1,), (1,)), ((), ())),
                                     preferred_element_type=F32)
            scores = scores * mask_ref[hh]
            o = jnp.dot(scores.astype(BF16), v, preferred_element_type=F32)
            o = o + jnp.dot((q * qdec_ref[hh]).astype(BF16), s.astype(BF16), preferred_element_type=F32)
            kd = (k * kdec_ref[hh]).astype(BF16)
            kv = lax.dot_general(kd, v, (((0,), (0,)), ((), ())), preferred_element_type=F32)
            s_new = cdec_ref[hh] * s + kv
            mu = jnp.mean(o, axis=-1, keepdims=True)
            oc = o - mu
            var = jnp.mean(oc * oc, axis=-1, keepdims=True)
            on = oc * lax.rsqrt(var + GN_EPS) * gn_ref[:, cols]
            g = g_ref[0, rws, cols].astype(F32)
            y_ref[0, rws, cols] = (g * jax.nn.sigmoid(g) * on).astype(y_ref.dtype)
            return s_new

        n_groups = T // rg
        s_out_ref[0, hh] = lax.fori_loop(0, n_groups, group, s0_ref[0, hh],
                                         unroll=4 if n_groups % 4 == 0 else 1)


def _retention_tables(T, pos0, chunk, rg, dh):
    half = dh // 2
    inv = ROPE_BASE ** (-jnp.arange(half, dtype=F32) / half)
    pos = pos0 + jnp.arange(T)
    ang = pos.astype(F32)[:, None] * inv[None, :]
    cos, sin = jnp.cos(ang), jnp.sin(ang)
    cos2 = jnp.concatenate([cos, cos], axis=1)
    sin2 = jnp.concatenate([-sin, sin], axis=1)
    log_g = jnp.log1p(-jnp.exp2(-5.0 - jnp.arange(RET_HEADS, dtype=F32)))[:, None, None]
    idx = jnp.arange(rg, dtype=F32)
    ci = jnp.floor(idx / chunk)
    diff = idx[:, None] - idx[None, :]
    same = ci[:, None] == ci[None, :]
    earlier = ci[None, :] < ci[:, None]
    dist = jnp.where(same, jnp.abs(diff), diff)
    mask = jnp.where(same | earlier, jnp.exp(dist[None] * log_g), 0.0)
    ones = jnp.ones((1, 1, dh), F32)
    qdec = jnp.exp((idx + 1.0)[None, :, None] * log_g) * ones
    kdec = jnp.exp((rg - 1.0 - idx)[None, :, None] * log_g) * ones
    cdec = jnp.exp(rg * log_g) * ones
    return cos2, sin2, mask, qdec, kdec, cdec


def _retention(z_ret, s0, gn, pos0, chunk):
    B, T, C4 = z_ret.shape
    H = RET_HEADS
    dh = C4 // (4 * H)
    rg = min(T, max(chunk, (RET_GROUP_ROWS // chunk) * chunk))
    hb = H if T * H * dh <= 64 * 1024 else 1
    nh = H // hb
    cos2, sin2, mask, qdec, kdec, cdec = _retention_tables(T, pos0, chunk, rg, dh)
    body = functools.partial(_ret_body, rg=rg, hb=hb, dh=dh)
    col = lambda off: (lambda b, h: (b, 0, off * nh + h))
    return pl.pallas_call(
        body,
        grid=(B, nh),
        in_specs=[pl.BlockSpec((1, T, hb * dh), col(0)),
                  pl.BlockSpec((1, T, hb * dh), col(1)),
                  pl.BlockSpec((1, T, hb * dh), col(2)),
                  pl.BlockSpec((1, T, hb * dh), col(3)),
                  _const_spec((T, dh)),
                  _const_spec((T, dh)),
                  pl.BlockSpec((hb, rg, rg), lambda b, h: (h, 0, 0)),
                  pl.BlockSpec((hb, rg, dh), lambda b, h: (h, 0, 0)),
                  pl.BlockSpec((hb, rg, dh), lambda b, h: (h, 0, 0)),
                  pl.BlockSpec((hb, 1, dh), lambda b, h: (h, 0, 0)),
                  pl.BlockSpec((1, hb * dh), lambda b, h: (0, h)),
                  pl.BlockSpec((1, hb, dh, dh), lambda b, h: (b, h, 0, 0))],
        out_specs=[pl.BlockSpec((1, T, hb * dh), lambda b, h: (b, 0, h)),
                   pl.BlockSpec((1, hb, dh, dh), lambda b, h: (b, h, 0, 0))],
        out_shape=[jax.ShapeDtypeStruct((B, T, H * dh), BF16),
                   jax.ShapeDtypeStruct((B, H, dh, dh), F32)],
        compiler_params=_params(2),
        name="retention",
    )(z_ret, z_ret, z_ret, z_ret, cos2, sin2, mask, qdec, kdec, cdec, gn, s0)


def _seg_allreduce(v, lane, op):
    for s in (1, 2, 4):
        up = pltpu.roll(v, LANES - s, axis=1)
        dn = pltpu.roll(v, s, axis=1)
        v = op(v, jnp.where((lane & s) == 0, up, dn))
    return v


def _mix_body(*refs, bb, tm, alpha, cap, n_steps, aliased):
    (yl_ref, yr_ref, x_ref, wo_ref, g1_ref, b1_ref, rw_ref, rb_ref, sg_ref, su_ref, sd_ref,
     tri_ref, cnt_in_ref) = refs[:13]
    refs = refs[14:] if aliased else refs[13:]
    base_ref, dest_ref, wts_ref, cnt_ref, xs_ref, carry_s, xpk_s, dv_s, ds_s, row_sems, idx_sem = refs
    i = pl.program_id(0)
    m = bb * tm
    d = x_ref.shape[-1]
    w = yl_ref.shape[1] // bb
    slot = i % 2
    chunk = m // ISSUE_CHUNKS

    def row_copy(sl, n, dst):
        return pltpu.make_async_copy(xpk_s.at[sl, n], xs_ref.at[dst], row_sems.at[sl])

    def issue_rows(sl, lo, hi):
        def one(n, c):
            for kk in range(TOP_K):
                row_copy(sl, n, ds_s[sl, kk, n]).start(priority=kk % 2)
            return c
        lax.fori_loop(lo, hi, one, 0)

    def issue_prev_chunk(c):
        for n in range(c * chunk, (c + 1) * chunk):
            for kk in range(TOP_K):
                row_copy(1 - slot, n, ds_s[1 - slot, kk, n]).start(priority=kk % 2)

    def drain(sl):
        def one(n, c):
            for _ in range(DRAIN_UNROLL * TOP_K):
                row_copy(sl, 0, 0).wait()
            return c
        lax.fori_loop(0, m // DRAIN_UNROLL, one, 0)

    @pl.when(i == 0)
    def _():
        carry_s[...] = jnp.zeros_like(carry_s)
        carry_s[0:1, :] = cnt_in_ref[...].astype(F32)
        xpk_s[1] = jnp.zeros(xpk_s.shape[1:], U32)
        spare = (N_EXPERTS * cap + lax.broadcasted_iota(I32, (TOP_K, m), 0) * m
                 + lax.broadcasted_iota(I32, (TOP_K, m), 1))
        dv_s[...] = spare
        first = pltpu.make_async_copy(dv_s, ds_s.at[1], idx_sem)
        first.start()
        first.wait()

    issue_prev_chunk(0)
    yl = jnp.concatenate([yl_ref[:, b * w:(b + 1) * w] for b in range(bb)], axis=0).astype(BF16)
    yr = yr_ref[...].reshape(m, yr_ref.shape[-1])
    mix = jnp.dot(jnp.concatenate([yl, yr], axis=1), wo_ref[...], preferred_element_type=F32)
    x1 = _layer_norm(alpha * x_ref[...].reshape(m, d) + mix, g1_ref[...], b1_ref[...])
    x1b = x1.astype(BF16)
    issue_prev_chunk(1)

    logits = jnp.dot(x1b, rw_ref[...], preferred_element_type=F32)

    s = jax.nn.sigmoid(logits)
    sb = s + rb_ref[...]
    lane = lax.broadcasted_iota(I32, (m, LANES), 1)
    e_id = lane & (N_EXPERTS - 1)
    e_f = e_id.astype(F32)
    low = lane < N_EXPERTS
    e_low = jnp.where(low, e_f, -1.0)
    grp = e_id >> 3
    big = jnp.float32(1e9)
    ninf = jnp.float32(-jnp.inf)

    m1 = _seg_allreduce(sb, lane, jnp.maximum)
    first_max = _seg_allreduce(jnp.where(sb == m1, e_f, big), lane, jnp.minimum)
    m2 = _seg_allreduce(jnp.where(e_f == first_max, ninf, sb), lane, jnp.maximum)
    gs = m1 + m2
    hg = jnp.dot(x1b, sg_ref[...], preferred_element_type=F32)
    issue_prev_chunk(2)
    rank = jnp.zeros((m, LANES), F32)
    for dgrp in range(1, N_GROUPS):
        other = pltpu.roll(gs, 8 * dgrp, axis=1)
        tie = jnp.where(grp >= dgrp, 1.0, 0.0)
        rank = rank + jnp.where(other > gs, 1.0, jnp.where(other == gs, tie, 0.0))
    v = jnp.where(rank < TOPK_GROUPS, jnp.where(low, sb, ninf), ninf)
    hu = jnp.dot(x1b, su_ref[...], preferred_element_type=F32)
    hs = (hg * jax.nn.sigmoid(hg) * hu).astype(BF16)

    idx_cols, w_cols = [], []
    sel = jnp.zeros((m, LANES), F32)
    for rnd in range(TOP_K):
        mx = jnp.max(v, axis=1, keepdims=True)
        idx = jnp.min(jnp.where(v == mx, e_f, big), axis=1, keepdims=True)
        hit = e_low == idx
        w_cols.append(jnp.sum(jnp.where(hit, s, 0.0), axis=1, keepdims=True))
        idx_cols.append(idx)
        v = jnp.where(hit, ninf, v)
        sel = jnp.where(hit, 1.0, sel)
        if rnd == TOP_K // 2 - 1:
            base_ref[...] = alpha * x1 + jnp.dot(hs, sd_ref[...], preferred_element_type=F32)
            issue_prev_chunk(3)

    packed = _pack_bf16_pair(x1)
    for sub in range(ROW_SUBLANES):
        xpk_s[slot, :, sub, :] = packed[:, sub * LANES:(sub + 1) * LANES]

    cum = jnp.dot(tri_ref[...], sel.astype(BF16), preferred_element_type=F32) + carry_s[0:1, :]
    carry_s[0:1, :] = carry_s[0:1, :] + jnp.sum(sel, axis=0, keepdims=True)
    cnt_ref[...] = carry_s[0:1, :].astype(I32)

    wsum = w_cols[0]
    for c in w_cols[1:]:
        wsum = wsum + c
    d_out = jnp.zeros((m, LANES), F32)
    w_out = jnp.zeros((m, LANES), F32)
    for kk in range(TOP_K):
        hit = e_low == idx_cols[kk]
        pk = jnp.sum(jnp.where(hit, cum, 0.0), axis=1, keepdims=True)
        d_out = jnp.where(lane == kk, idx_cols[kk] * float(cap) + pk, d_out)
        w_out = jnp.where(lane == kk, w_cols[kk] / wsum * ROUTED_SCALE, w_out)
    wts_ref[...] = w_out[:, :TOP_K]
    dest_t = jnp.transpose(d_out)[:TOP_K].astype(I32)
    dest_ref[0] = dest_t
    dv_s[...] = dest_t
    to_smem = pltpu.make_async_copy(dv_s, ds_s.at[slot], idx_sem)
    to_smem.start()
    to_smem.wait()

    drain(1 - slot)

    @pl.when(i == n_steps - 1)
    def _():
        issue_rows(slot, 0, m)
        drain(slot)


def _mix(y_lru2, y_ret, x, w, alpha, cnt_in, xs, cap):
    B, T, D = x.shape
    assert D == 2 * ROW_SUBLANES * LANES, "a packed token row must fill exactly one (8, 128) tile"
    W = y_ret.shape[-1]
    tm = min(MIX_ROWS, T)
    bb = min(B, MIX_ROWS // tm)
    m = bb * tm
    n = B * T
    nt = T // tm
    n_steps = (B // bb) * nt
    hs = w["sg"].shape[1]
    aliased = xs is not None
    tri = (lax.broadcasted_iota(I32, (m, m), 1) < lax.broadcasted_iota(I32, (m, m), 0)).astype(BF16)
    body = functools.partial(_mix_body, bb=bb, tm=tm, alpha=alpha, cap=cap, n_steps=n_steps, aliased=aliased)
    in_specs = [pl.BlockSpec((tm, bb * W), lambda i: (i % nt, i // nt)),
                pl.BlockSpec((bb, tm, W), lambda i: (i // nt, i % nt, 0)),
                pl.BlockSpec((bb, tm, D), lambda i: (i // nt, i % nt, 0)),
                _const_spec((2 * W, D)),
                _const_spec((1, D)),
                _const_spec((1, D)),
                _const_spec((D, LANES)),
                _const_spec((1, LANES)),
                _const_spec((D, hs)),
                _const_spec((D, hs)),
                _const_spec((hs, D)),
                _const_spec((m, m)),
                _const_spec((1, LANES))]
    args = [y_lru2, y_ret, x, w["wo"], w["g1"], w["b1"], w["rw"], w["rb"], w["sg"], w["su"], w["sd"],
            tri, cnt_in]
    if aliased:
        in_specs.append(pl.BlockSpec(memory_space=pl.ANY))
        args.append(xs)
    return pl.pallas_call(
        body,
        grid=(n_steps,),
        in_specs=in_specs,
        out_specs=[pl.BlockSpec((m, D), lambda i: (i, 0)),
                   pl.BlockSpec((1, TOP_K, m), lambda i: (i, 0, 0)),
                   pl.BlockSpec((m, TOP_K), lambda i: (i, 0)),
                   pl.BlockSpec((1, LANES), lambda i: (0, 0)),
                   pl.BlockSpec(memory_space=pl.ANY)],
        out_shape=[jax.ShapeDtypeStruct((n, D), F32),
                   jax.ShapeDtypeStruct((n_steps, TOP_K, m), I32),
                   jax.ShapeDtypeStruct((n, TOP_K), F32),
                   jax.ShapeDtypeStruct((1, LANES), I32),
                   jax.ShapeDtypeStruct((N_EXPERTS * cap + MIX_ROWS * TOP_K, ROW_SUBLANES, LANES), U32)],
        scratch_shapes=[pltpu.VMEM((8, LANES), F32),
                        pltpu.VMEM((2, m, ROW_SUBLANES, LANES), U32),
                        pltpu.VMEM((TOP_K, m), I32),
                        pltpu.SMEM((2, TOP_K, m), I32),
                        pltpu.SemaphoreType.DMA((2,)),
                        pltpu.SemaphoreType.DMA(())],
        input_output_aliases={13: 4} if aliased else {},
        compiler_params=_params(1),
        name="mix_router",
    )(*args)


def _gmm_body(ge_ref, gr_ref, gn_ref, gt_ref, gx_ref, gs_ref, xs_ref, wg_ref, wu_ref, wd_ref, ys_ref,
              wgf_s, wuf_s, wdf_s, wgu_s, wd_s, x_s, y_s, w_sems, in_sems, out_sems, *, tm, n_items):
    i = pl.program_id(0)
    hid = wg_ref.shape[2]
    slot = i % 2

    def weight_copies(expert, sl):
        return [pltpu.make_async_copy(wg_ref.at[expert], wgf_s.at[sl], w_sems.at[sl]),
                pltpu.make_async_copy(wu_ref.at[expert], wuf_s.at[sl], w_sems.at[sl]),
                pltpu.make_async_copy(wd_ref.at[expert], wdf_s.at[sl], w_sems.at[sl])]

    def tile_copies(to_vmem, item, sl):
        r0 = pl.multiple_of(gr_ref[item] * tm, tm)
        out = []
        for sub in range(ROW_SUBLANES):
            cols = pl.ds(sub * LANES, LANES)
            if to_vmem:
                out.append(pltpu.make_async_copy(xs_ref.at[pl.ds(r0, tm), sub, :], x_s.at[sl, :, cols],
                                                 in_sems.at[sl]))
            else:
                out.append(pltpu.make_async_copy(y_s.at[sl, :, cols], ys_ref.at[pl.ds(r0, tm), sub, :],
                                                 out_sems.at[sl]))
        return out

    @pl.when(i == 0)
    def _():
        for cp in weight_copies(ge_ref[0], 0):
            cp.start()
        for cp in tile_copies(True, 0, 0):
            cp.start()
        y_s[...] = jnp.zeros_like(y_s)

    nxt = jnp.minimum(i + 1, n_items - 1)

    @pl.when((i + 1 < n_items) & (gn_ref[nxt] > 0))
    def _():
        for cp in tile_copies(True, nxt, 1 - slot):
            cp.start()

    e = ge_ref[i]
    e_prev = ge_ref[jnp.maximum(i - 1, 0)]

    @pl.when((i == 0) | (e != e_prev))
    def _():
        wsl = gs_ref[i]
        nxt_e = gx_ref[i]

        @pl.when(nxt_e >= 0)
        def _():
            for cp in weight_copies(nxt_e, 1 - wsl):
                cp.start()

        for cp in weight_copies(e, wsl):
            cp.wait()
        wgu_s[:, :hid] = wgf_s[wsl].astype(BF16)
        wgu_s[:, hid:] = wuf_s[wsl].astype(BF16)
        wd_s[...] = wdf_s[wsl].astype(BF16)

    n_valid = gn_ref[i]

    @pl.when(n_valid > 0)
    def _():
        for cp in tile_copies(True, i, slot):
            cp.wait()

        @pl.when(i >= 2)
        def _():
            for cp in tile_copies(False, i, slot):
                cp.wait()

        def expert_mlp(rows):
            pk = x_s[slot, :rows]
            valid = lax.broadcasted_iota(I32, pk.shape, 0) < n_valid
            xa, xb = _unpack_bf16_pair(jnp.where(valid, pk, jnp.uint32(0)))
            x = jnp.concatenate([xa.astype(BF16), xb.astype(BF16)], axis=1)
            h2 = jnp.dot(x, wgu_s[...], preferred_element_type=F32)
            hg = h2[:, :hid]
            h = (hg * jax.nn.sigmoid(hg) * h2[:, hid:]).astype(BF16)
            y_s[slot, :rows] = _pack_bf16_pair(jnp.dot(h, wd_s[...], preferred_element_type=F32))

        @pl.when(n_valid > tm // 2)
        def _():
            expert_mlp(tm)

        @pl.when(n_valid <= tm // 2)
        def _():
            expert_mlp(tm // 2)

        for cp in tile_copies(False, i, slot):
            cp.start()

    @pl.when(i == n_items - 1)
    def _():
        total = gt_ref[0]

        @pl.when(total >= 2)
        def _():
            for cp in tile_copies(False, 0, total % 2):
                cp.wait()

        for cp in tile_copies(False, 0, (total + 1) % 2):
            cp.wait()


def _gmm_metadata(counts, cap, tm, n_items):
    e = counts.shape[0]
    tiles = (counts + tm - 1) // tm
    item_end = jnp.cumsum(tiles)
    total = item_end[-1]
    it = jnp.arange(n_items, dtype=I32)
    itc = jnp.minimum(it, total - 1)
    ge = jnp.sum((item_end[None, :] <= itc[:, None]).astype(I32), axis=1)
    onehot = ge[:, None] == jnp.arange(e, dtype=I32)[None, :]
    start = jnp.sum(jnp.where(onehot, (item_end - tiles)[None, :], 0), axis=1)
    cnt = jnp.sum(jnp.where(onehot, counts[None, :], 0), axis=1)
    j = itc - start
    gr = ge * (cap // tm) + j
    gn = jnp.where(it < total, jnp.clip(cnt - j * tm, 0, tm), 0)
    ids = jnp.arange(e, dtype=I32)
    live = tiles > 0
    later = live[None, :] & (ids[None, :] > ids[:, None])
    next_e = jnp.min(jnp.where(later, ids[None, :], e), axis=1)
    next_e = jnp.where(next_e < e, next_e, -1)
    wslot = (jnp.cumsum(live.astype(I32)) - 1) % 2
    gx = jnp.sum(jnp.where(onehot, next_e[None, :], 0), axis=1)
    gs = jnp.sum(jnp.where(onehot, wslot[None, :], 0), axis=1)
    return tuple(a.astype(I32) for a in (ge, gr, gn, total.reshape(1), gx, gs))


def _gmm(xs, counts, cap, n_tokens, wg, wu, wd):
    tm = GMM_ROWS
    e, d, hid = wg.shape
    n_items = (n_tokens * TOP_K) // tm + e
    meta = _gmm_metadata(counts, cap, tm, n_items)
    body = functools.partial(_gmm_body, tm=tm, n_items=n_items)
    grid_spec = pltpu.PrefetchScalarGridSpec(
        num_scalar_prefetch=6,
        grid=(n_items,),
        in_specs=[pl.BlockSpec(memory_space=pl.ANY)] * 4,
        out_specs=pl.BlockSpec(memory_space=pl.ANY),
        scratch_shapes=[pltpu.VMEM((2, d, hid), F32), pltpu.VMEM((2, d, hid), F32),
                        pltpu.VMEM((2, hid, d), F32),
                        pltpu.VMEM((d, 2 * hid), BF16), pltpu.VMEM((hid, d), BF16),
                        pltpu.VMEM((2, tm, d // 2), U32), pltpu.VMEM((2, tm, d // 2), U32),
                        pltpu.SemaphoreType.DMA((2,)), pltpu.SemaphoreType.DMA((2,)),
                        pltpu.SemaphoreType.DMA((2,))],
    )
    return pl.pallas_call(
        body,
        grid_spec=grid_spec,
        out_shape=jax.ShapeDtypeStruct(xs.shape, U32),
        compiler_params=_params(1),
        name="expert_gmm",
    )(*meta, xs, wg, wu, wd)


def _final_body(d0_ref, d1_ref, dn_ref, wts_ref, base_ref, p_ref, ys_ref, pg_ref, pb_ref, pp_ref, g2_ref, b2_ref,
                out_ref, rows_s, sum_s, wrep_s, sems, *, bb, tm, n_steps):
    i = pl.program_id(0)
    m = bb * tm
    d = base_ref.shape[1]
    slot = lax.rem(i, ROW_BUFFERS)
    ahead = lax.rem(i + 2, ROW_BUFFERS)

    def row_copy(sl, d_row, kk, n):
        return pltpu.make_async_copy(ys_ref.at[d_row], rows_s.at[sl, kk, n], sems.at[sl])

    def issue_token(sl, dref, n):
        for kk in range(TOP_K):
            row_copy(sl, dref[0, kk, n], kk, n).start(priority=kk % 2)

    def issue_tile(sl, dref):
        def one(n, c):
            issue_token(sl, dref, n)
            return c
        lax.fori_loop(0, m, one, 0)

    def issue_ahead(lo, hi):
        for n in range(lo, hi):
            issue_token(ahead, dn_ref, n)

    def drain(sl):
        def one(n, c):
            for _ in range(DRAIN_UNROLL * TOP_K):
                row_copy(sl, 0, 0, 0).wait()
            return c
        lax.fori_loop(0, m // DRAIN_UNROLL, one, 0)

    @pl.when(i == 0)
    def _():
        issue_tile(0, d0_ref)
        issue_tile(1, d1_ref)

    drain(slot)

    wts = wts_ref[...]
    for kk in range(TOP_K):
        wk = jnp.broadcast_to(wts[:, kk:kk + 1], (m, LANES))
        wrep_s[kk] = _pack_bf16_pair(jnp.concatenate([wk, wk], axis=1))

    def combine(n, c):
        acc = jnp.zeros((2 * ROW_SUBLANES, LANES), BF16)
        for kk in range(TOP_K):
            row = pltpu.bitcast(rows_s[slot, kk, n], BF16)
            wk = jnp.broadcast_to(wrep_s[kk, pl.ds(n, 1), :], (ROW_SUBLANES, LANES))
            acc = acc + pltpu.bitcast(wk, BF16) * row
        sum_s[n] = pltpu.bitcast(acc, U32)
        return c

    lax.fori_loop(0, m, combine, 0, unroll=2)
    issue_ahead(0, m // 2)
    routed = jnp.concatenate(_unpack_bf16_pair(
        jnp.concatenate([sum_s[:, sub, :] for sub in range(ROW_SUBLANES)], axis=1)), axis=1)
    x2 = _layer_norm(base_ref[...] + routed, g2_ref[...], b2_ref[...])
    issue_ahead(m // 2, m)
    gate = jax.nn.sigmoid(jnp.dot(x2.astype(BF16), pg_ref[...], preferred_element_type=F32) + pb_ref[...])
    proj = jnp.dot(p_ref[...].reshape(m, p_ref.shape[-1]).astype(BF16), pp_ref[...], preferred_element_type=F32)
    out_ref[...] = (x2 + gate * proj).reshape(bb, tm, d)

    @pl.when(i == n_steps - 1)
    def _():
        drain(lax.rem(i + 1, ROW_BUFFERS))
        drain(ahead)


def _final(base, dest, wts, p, ys, w, B, T):
    n, D = base.shape
    tm = min(MIX_ROWS, T)
    bb = min(B, MIX_ROWS // tm)
    m = bb * tm
    nt = T // tm
    n_steps = (B // bb) * nt
    pd = p.shape[-1]
    body = functools.partial(_final_body, bb=bb, tm=tm, n_steps=n_steps)
    return pl.pallas_call(
        body,
        grid=(n_steps,),
        in_specs=[pl.BlockSpec((1, TOP_K, m), lambda i: (0, 0, 0), memory_space=pltpu.SMEM),
                  pl.BlockSpec((1, TOP_K, m), lambda i: (min(1, n_steps - 1), 0, 0), memory_space=pltpu.SMEM),
                  pl.BlockSpec((1, TOP_K, m), lambda i: (jnp.minimum(i + 2, n_steps - 1), 0, 0),
                               memory_space=pltpu.SMEM),
                  pl.BlockSpec((m, TOP_K), lambda i: (i, 0)),
                  pl.BlockSpec((m, D), lambda i: (i, 0)),
                  pl.BlockSpec((bb, tm, pd), lambda i: (i // nt, i % nt, 0)),
                  pl.BlockSpec(memory_space=pl.ANY),
                  _const_spec((D, D)),
                  _const_spec((1, D)),
                  _const_spec((pd, D)),
                  _const_spec((1, D)),
                  _const_spec((1, D))],
        out_specs=pl.BlockSpec((bb, tm, D), lambda i: (i // nt, i % nt, 0)),
        out_shape=jax.ShapeDtypeStruct((B, T, D), F32),
        scratch_shapes=[pltpu.VMEM((ROW_BUFFERS, TOP_K, m, ROW_SUBLANES, LANES), U32),
                        pltpu.VMEM((m, ROW_SUBLANES, LANES), U32),
                        pltpu.VMEM((TOP_K, m, LANES), U32),
                        pltpu.SemaphoreType.DMA((ROW_BUFFERS,))],
        compiler_params=_params(1),
        name="combine_final",
    )(dest, dest, dest, wts, base, p, ys, w["pg"], w["pb"], w["pp"], w["g2"], w["b2"])


def _prep_layer(prm):
    (w_in, conv_w, conv_b, lru_wa, lru_ba, lru_wx, lru_bx, lru_lambda, ret_gn, w_out, ln1_g, ln1_b,
     router_w, router_b, e_gate, e_up, e_down, s_gate, s_up, s_down, ln2_g, ln2_b,
     ple_w_proj, ple_w_gate, ple_b_gate) = prm
    row = lambda v: v.reshape(1, -1)
    rw2 = jnp.concatenate([router_w, router_w], axis=1).astype(BF16)
    return dict(
        w_in=w_in.astype(BF16), conv_w=conv_w, conv_b=row(conv_b), wa=lru_wa.astype(BF16), ba=row(lru_ba),
        wx=lru_wx.astype(BF16), bx=row(lru_bx), lam=row(lru_lambda), gn=row(ret_gn), wo=w_out.astype(BF16),
        g1=row(ln1_g), b1=row(ln1_b), rw=rw2, rb=row(jnp.concatenate([router_b, router_b])),
        e_gate=e_gate, e_up=e_up, e_down=e_down, sg=s_gate.astype(BF16), su=s_up.astype(BF16),
        sd=s_down.astype(BF16), g2=row(ln2_g), b2=row(ln2_b), pp=ple_w_proj.astype(BF16),
        pg=ple_w_gate.astype(BF16), pb=row(ple_b_gate))


def _mixers(x, conv_st, lru_st, ret_st, pos0, chunk, w):
    B, T, D = x.shape
    W = conv_st.shape[-1]
    z_lru, z_ret = _inproj(x, w["w_in"], 2 * W)
    y_ret, new_ret = _retention(z_ret, ret_st, w["gn"], pos0, chunk)
    y_lru, conv_tm, new_lru = _lru(z_lru, jnp.transpose(conv_st, (1, 0, 2)), lru_st,
                                   w["conv_w"], w["conv_b"], w["wa"], w["ba"], w["wx"], w["bx"], w["lam"],
                                   after=new_ret)
    return y_lru.reshape(T, B * W), y_ret, jnp.transpose(conv_tm, (1, 0, 2)), new_lru, new_ret


def kernel(x_prompt, x_sample, p_prompt, p_sample, state_conv, state_lru, state_ret, w_in, conv_w, conv_b,
           lru_wa, lru_ba, lru_wx, lru_bx, lru_lambda, ret_gn, w_out, ln1_g, ln1_b, router_w, router_b,
           exp_w_gate, exp_w_up, exp_w_down, sh_w_gate, sh_w_up, sh_w_down, ln2_g, ln2_b,
           ple_w_proj, ple_w_gate, ple_b_gate):
    depth = w_in.shape[0]
    alpha = (2 * depth) ** 0.25
    b_p, t_p, _ = x_prompt.shape
    b_s, t_s, _ = x_sample.shape
    W = state_conv.shape[-1]
    H, dh = state_ret.shape[2], state_ret.shape[3]
    n_tokens = b_p * t_p + b_s * t_s
    cap = -(-n_tokens // GMM_ROWS) * GMM_ROWS
    hp, hs = x_prompt, x_sample
    outs = [[] for _ in range(6)]
    for i in range(depth):
        prm = (w_in[i], conv_w[i], conv_b[i], lru_wa[i], lru_ba[i], lru_wx[i], lru_bx[i], lru_lambda[i],
               ret_gn[i], w_out[i], ln1_g[i], ln1_b[i], router_w[i], router_b[i], exp_w_gate[i], exp_w_up[i],
               exp_w_down[i], sh_w_gate[i], sh_w_up[i], sh_w_down[i], ln2_g[i], ln2_b[i],
               ple_w_proj[i], ple_w_gate[i], ple_b_gate[i])
        w = _prep_layer(prm)
        zc = jnp.zeros((b_p, CONV_WIDTH - 1, W), x_prompt.dtype)
        zl = jnp.zeros((b_p, W), F32)
        zr = jnp.zeros((b_p, H, dh, dh), F32)
        yl_p, yr_p, c_p, l_p, r_p = _mixers(hp, zc, zl, zr, 0, CHUNK, w)
        yl_s, yr_s, c_s, l_s, r_s = _mixers(hs, state_conv[i], state_lru[i], state_ret[i], PAST_LEN, t_s, w)
        for o, val in zip(outs, (c_p, l_p, r_p, c_s, l_s, r_s)):
            o.append(val)
        base_p, dest_p, wts_p, cnt_p, xs = _mix(yl_p, yr_p, hp, w, alpha, jnp.zeros((1, LANES), I32), None, cap)
        base_s, dest_s, wts_s, cnt_all, xs = _mix(yl_s, yr_s, hs, w, alpha, cnt_p, xs, cap)
        ys = _gmm(xs, cnt_all[0, :N_EXPERTS], cap, n_tokens, w["e_gate"], w["e_up"], w["e_down"])
        hp = _final(base_p, dest_p, wts_p, p_prompt[i], ys, w, b_p, t_p)
        hs = _final(base_s, dest_s, wts_s, p_sample[i], ys, w, b_s, t_s)
    return (hp, hs) + tuple(jnp.stack(o) for o in outs)
```

```python
import functools

import jax
import jax.numpy as jnp
from jax import lax
from jax.experimental import pallas as pl
from jax.experimental.pallas import tpu as pltpu

F32 = jnp.float32
BF16 = jnp.bfloat16
U32 = jnp.uint32
I32 = jnp.int32

CHUNK = 64
PAST_LEN = 1024
CONV_WIDTH = 4
LRU_C = 8.0
RET_HEADS = 8
ROPE_BASE = 10000.0
N_EXPERTS = 64
TOP_K = 8
N_GROUPS = 8
TOPK_GROUPS = 4
ROUTED_SCALE = 2.5
LN_EPS = 1e-5
GN_EPS = 1e-6

LANES = 128
ROW_SUBLANES = 8
DRAIN_UNROLL = 8
ISSUE_CHUNKS = 4
ROW_BUFFERS = 3
ROWS_PER_STEP = 512
INPROJ_COLS = 512
MIX_ROWS = 256
GMM_ROWS = 512
RET_GROUP_ROWS = 256
LRU_TIME_TILE = 64
LRU_GATE_ROWS = 128
RET_ALL_HEADS_ELEMS = 64 * 1024
VMEM_LIMIT = 56 * 1024 * 1024


def _const_spec(shape):
    zeros = (0,) * len(shape)
    return pl.BlockSpec(shape, lambda *_: zeros, pipeline_mode=pl.Buffered(1))


def _params(n_axes):
    return pltpu.CompilerParams(dimension_semantics=("arbitrary",) * n_axes,
                                vmem_limit_bytes=VMEM_LIMIT)


def _layer_norm(x, g, b):
    mu = jnp.mean(x, axis=-1, keepdims=True)
    xc = x - mu
    var = jnp.mean(xc * xc, axis=-1, keepdims=True)
    return xc * lax.rsqrt(var + LN_EPS) * g + b


def _pack_bf16_pair(x):
    c = x.shape[1] // 2
    xb = x.astype(BF16).astype(F32)
    hi = pltpu.bitcast(xb[:, :c], U32)
    lo = pltpu.bitcast(xb[:, c:], U32)
    return hi | (lo >> 16)


def _unpack_bf16_pair(pk):
    hi = pltpu.bitcast(pk & jnp.uint32(0xFFFF0000), F32)
    lo = pltpu.bitcast(pk << 16, F32)
    return hi, lo


def _inproj_body(x_ref, w_ref, zl_ref, zr_ref, *, bb, tm, lru_cols, tn):
    d = x_ref.shape[-1]
    x = x_ref[...].reshape(bb * tm, d).astype(BF16)
    for j in range(w_ref.shape[1] // tn):
        c0 = j * tn
        acc = jnp.dot(x, w_ref[:, c0:c0 + tn], preferred_element_type=F32)
        for b in range(bb):
            rows = acc[b * tm:(b + 1) * tm]
            if c0 < lru_cols:
                zl_ref[:, b * lru_cols + c0:b * lru_cols + c0 + tn] = rows
            else:
                zr_ref[b, :, c0 - lru_cols:c0 - lru_cols + tn] = rows.astype(BF16)


def _inproj_cols_body(x_ref, w_ref, zl_ref, zr_ref, xb_s, *, n_lru):
    j = pl.program_id(0)
    bb, tm, d = x_ref.shape

    @pl.when(j == 0)
    def _():
        xb_s[...] = x_ref[...].reshape(bb * tm, d).astype(BF16)

    acc = jnp.dot(xb_s[...], w_ref[...], preferred_element_type=F32)

    @pl.when(j < n_lru)
    def _():
        for b in range(bb):
            zl_ref[:, b, :] = acc[b * tm:(b + 1) * tm]

    @pl.when(j >= n_lru)
    def _():
        zr_ref[...] = acc.reshape(zr_ref.shape).astype(BF16)


def _inproj_cols(x, w_in_b, lru_cols):
    B, T, D = x.shape
    n_cols = w_in_b.shape[1]
    tn = INPROJ_COLS
    n_lru = lru_cols // tn
    body = functools.partial(_inproj_cols_body, n_lru=n_lru)
    return pl.pallas_call(
        body,
        grid=(n_cols // tn,),
        in_specs=[_const_spec((B, T, D)),
                  pl.BlockSpec((D, tn), lambda j: (0, j))],
        out_specs=[pl.BlockSpec((T, B, tn), lambda j: (0, 0, jnp.minimum(j, n_lru - 1))),
                   pl.BlockSpec((B, T, tn), lambda j: (0, 0, jnp.maximum(j - n_lru, 0)))],
        out_shape=[jax.ShapeDtypeStruct((T, B, lru_cols), F32),
                   jax.ShapeDtypeStruct((B, T, n_cols - lru_cols), BF16)],
        scratch_shapes=[pltpu.VMEM((B * T, D), BF16)],
        compiler_params=_params(1),
        name="inproj",
    )(x, w_in_b)


def _inproj(x, w_in_b, lru_cols):
    B, T, D = x.shape
    if B * T <= ROWS_PER_STEP:
        return _inproj_cols(x, w_in_b, lru_cols)
    z_lru, z_ret = _inproj_rows(x, w_in_b, lru_cols)
    return z_lru.reshape(T, B, lru_cols), z_ret


def _inproj_rows(x, w_in_b, lru_cols):
    B, T, D = x.shape
    n_cols = w_in_b.shape[1]
    ret_cols = n_cols - lru_cols
    tm = min(ROWS_PER_STEP, T)
    bb = min(B, ROWS_PER_STEP // tm)
    body = functools.partial(_inproj_body, bb=bb, tm=tm, lru_cols=lru_cols, tn=INPROJ_COLS)
    return pl.pallas_call(
        body,
        grid=(B // bb, T // tm),
        in_specs=[pl.BlockSpec((bb, tm, D), lambda b, t: (b, t, 0)),
                  _const_spec((D, n_cols))],
        out_specs=[pl.BlockSpec((tm, bb * lru_cols), lambda b, t: (t, b)),
                   pl.BlockSpec((bb, tm, ret_cols), lambda b, t: (b, t, 0))],
        out_shape=[jax.ShapeDtypeStruct((T, B * lru_cols), F32),
                   jax.ShapeDtypeStruct((B, T, ret_cols), BF16)],
        compiler_params=_params(2),
        name="inproj",
    )(x, w_in_b)


def _lru_body(xl_ref, gl_ref, conv0_ref, h0_ref, cw_ref, cb_ref, wa_ref, ba_ref, wx_ref, bx_ref, lam_ref,
              after_ref, y_ref, conv_out_ref, h_out_ref, xp_s, a_s, b_s, h_s, *, tt, rows):
    del after_ref
    i = pl.program_id(0)
    B, W = h0_ref.shape
    nblk = wa_ref.shape[0]
    blk = W // nblk

    @pl.when(i == 0)
    def _():
        xp_s[0:CONV_WIDTH - 1] = conv0_ref[...]
        h_s[...] = h0_ref[...]

    xp_s[CONV_WIDTH - 1:] = xl_ref[...]

    lam = lam_ref[...]
    neg = -lam
    softplus = jnp.maximum(neg, 0.0) + jnp.log1p(jnp.exp(-jnp.abs(neg)))
    decay = (-LRU_C) * softplus

    def gates(c, carry):
        t0 = pl.multiple_of(c * rows, rows)
        xc = cb_ref[...].reshape(1, 1, W)
        for j in range(CONV_WIDTH):
            xc = xc + xp_s[pl.ds(t0 + j, rows)] * cw_ref[j:j + 1].reshape(1, 1, W)
        xc2 = xc.reshape(rows * B, W)
        xcb = xc2.astype(BF16)
        r_parts, i_parts = [], []
        for n in range(nblk):
            xb = xcb[:, n * blk:(n + 1) * blk]
            r_parts.append(jnp.dot(xb, wa_ref[n], preferred_element_type=F32))
            i_parts.append(jnp.dot(xb, wx_ref[n], preferred_element_type=F32))
        r = jax.nn.sigmoid(jnp.concatenate(r_parts, axis=1) + ba_ref[...])
        ig = jax.nn.sigmoid(jnp.concatenate(i_parts, axis=1) + bx_ref[...])
        a = jnp.exp(decay * r)
        bterm = jnp.sqrt(1.0 - a * a) * (ig * xc2)
        a_s[pl.ds(t0, rows)] = a.reshape(rows, B, W)
        b_s[pl.ds(t0, rows)] = bterm.reshape(rows, B, W)
        return carry

    lax.fori_loop(0, tt // rows, gates, 0)

    def step(t, h):
        hn = a_s[t] * h + b_s[t]
        y_ref[t] = hn * jax.nn.gelu(gl_ref[t])
        return hn

    h_last = lax.fori_loop(0, tt, step, h_s[...], unroll=8)
    h_s[...] = h_last
    tail = xp_s[tt:tt + CONV_WIDTH - 1]
    xp_s[0:CONV_WIDTH - 1] = tail
    conv_out_ref[...] = tail
    h_out_ref[...] = h_last


def _lru(z_lru3, conv0_tm, h0, conv_w, conv_b, wa_b, ba, wx_b, bx, lam, after):
    T, B, W2 = z_lru3.shape
    W = W2 // 2
    tt = min(LRU_TIME_TILE, T)
    rows = max(1, min(tt, LRU_GATE_ROWS // B))
    body = functools.partial(_lru_body, tt=tt, rows=rows)
    nb = wa_b.shape[0]
    blk = W // nb
    return pl.pallas_call(
        body,
        grid=(T // tt,),
        in_specs=[pl.BlockSpec((tt, B, W), lambda t: (t, 0, 0)),
                  pl.BlockSpec((tt, B, W), lambda t: (t, 0, 1)),
                  _const_spec((CONV_WIDTH - 1, B, W)),
                  _const_spec((B, W)),
                  _const_spec((CONV_WIDTH, W)),
                  _const_spec((1, W)),
                  _const_spec((nb, blk, blk)),
                  _const_spec((1, W)),
                  _const_spec((nb, blk, blk)),
                  _const_spec((1, W)),
                  _const_spec((1, W)),
                  pl.BlockSpec(memory_space=pl.ANY)],
        out_specs=[pl.BlockSpec((tt, B, W), lambda t: (t, 0, 0)),
                   pl.BlockSpec((CONV_WIDTH - 1, B, W), lambda t: (0, 0, 0)),
                   pl.BlockSpec((B, W), lambda t: (0, 0))],
        out_shape=[jax.ShapeDtypeStruct((T, B, W), F32),
                   jax.ShapeDtypeStruct((CONV_WIDTH - 1, B, W), F32),
                   jax.ShapeDtypeStruct((B, W), F32)],
        scratch_shapes=[pltpu.VMEM((tt + CONV_WIDTH - 1, B, W), F32),
                        pltpu.VMEM((tt, B, W), F32),
                        pltpu.VMEM((tt, B, W), F32),
                        pltpu.VMEM((B, W), F32)],
        compiler_params=_params(1),
        name="rglru",
    )(z_lru3, z_lru3, conv0_tm, h0, conv_w, conv_b, wa_b, ba, wx_b, bx, lam, after)


def _ret_body(q_ref, k_ref, v_ref, g_ref, cos_ref, sin_ref, mask_ref, qdec_ref, kdec_ref, cdec_ref, gn_ref,
              s0_ref, y_ref, s_out_ref, *, rg, hb, dh):
    T = q_ref.shape[1]
    scale = dh ** -0.5

    def rope(t, cos, sin):
        return t * cos + pltpu.roll(t, dh // 2, axis=1) * sin

    for hh in range(hb):
        cols = slice(hh * dh, (hh + 1) * dh)

        def group(c, s, hh=hh, cols=cols):
            r0 = pl.multiple_of(c * rg, rg)
            rws = pl.ds(r0, rg)
            cos = cos_ref[rws, :]
            sin = sin_ref[rws, :]
            q = rope(q_ref[0, rws, cols].astype(F32), cos, sin)
            k = rope(k_ref[0, rws, cols].astype(F32), cos, sin) * scale
            v = v_ref[0, rws, cols]
            scores = lax.dot_general(q.astype(BF16), k.astype(BF16), (((1,), (1,)), ((), ())),
                                     preferred_element_type=F32)
            scores = scores * mask_ref[hh]
            o = jnp.dot(scores.astype(BF16), v, preferred_element_type=F32)
            o = o + jnp.dot((q * qdec_ref[hh]).astype(BF16), s.astype(BF16), preferred_element_type=F32)
            kd = (k * kdec_ref[hh]).astype(BF16)
            kv = lax.dot_general(kd, v, (((0,), (0,)), ((), ())), preferred_element_type=F32)
            s_new = cdec_ref[hh] * s + kv
            mu = jnp.mean(o, axis=-1, keepdims=True)
            oc = o - mu
            var = jnp.mean(oc * oc, axis=-1, keepdims=True)
            on = oc * lax.rsqrt(var + GN_EPS) * gn_ref[:, cols]
            g = g_ref[0, rws, cols].astype(F32)
            y_ref[0, rws, cols] = (g * jax.nn.sigmoid(g) * on).astype(y_ref.dtype)
            return s_new

        n_groups = T // rg
        s_out_ref[0, hh] = lax.fori_loop(0, n_groups, group, s0_ref[0, hh],
                                         unroll=4 if n_groups % 4 == 0 else 1)


def _retention_tables(T, pos0, chunk, rg, dh):
    half = dh // 2
    inv = ROPE_BASE ** (-jnp.arange(half, dtype=F32) / half)
    pos = pos0 + jnp.arange(T)
    ang = pos.astype(F32)[:, None] * inv[None, :]
    cos, sin = jnp.cos(ang), jnp.sin(ang)
    cos2 = jnp.concatenate([cos, cos], axis=1)
    sin2 = jnp.concatenate([-sin, sin], axis=1)
    log_g = jnp.log1p(-jnp.exp2(-5.0 - jnp.arange(RET_HEADS, dtype=F32)))[:, None, None]
    idx = jnp.arange(rg, dtype=F32)
    ci = jnp.floor(idx / chunk)
    diff = idx[:, None] - idx[None, :]
    same = ci[:, None] == ci[None, :]
    earlier = ci[None, :] < ci[:, None]
    dist = jnp.where(same, jnp.abs(diff), diff)
    mask = jnp.where(same | earlier, jnp.exp(dist[None] * log_g), 0.0)
    ones = jnp.ones((1, 1, dh), F32)
    qdec = jnp.exp((idx + 1.0)[None, :, None] * log_g) * ones
    kdec = jnp.exp((rg - 1.0 - idx)[None, :, None] * log_g) * ones
    cdec = jnp.exp(rg * log_g) * ones
    return cos2, sin2, mask, qdec, kdec, cdec


def _retention(z_ret, s0, gn, pos0, chunk):
    B, T, C4 = z_ret.shape
    H = RET_HEADS
    dh = C4 // (4 * H)
    rg = min(T, max(chunk, (RET_GROUP_ROWS // chunk) * chunk))
    hb = H if T * H * dh <= RET_ALL_HEADS_ELEMS else 1
    nh = H // hb
    cos2, sin2, mask, qdec, kdec, cdec = _retention_tables(T, pos0, chunk, rg, dh)
    body = functools.partial(_ret_body, rg=rg, hb=hb, dh=dh)
    col = lambda off: (lambda b, h: (b, 0, off * nh + h))
    return pl.pallas_call(
        body,
        grid=(B, nh),
        in_specs=[pl.BlockSpec((1, T, hb * dh), col(0)),
                  pl.BlockSpec((1, T, hb * dh), col(1)),
                  pl.BlockSpec((1, T, hb * dh), col(2)),
                  pl.BlockSpec((1, T, hb * dh), col(3)),
                  _const_spec((T, dh)),
                  _const_spec((T, dh)),
                  pl.BlockSpec((hb, rg, rg), lambda b, h: (h, 0, 0)),
                  pl.BlockSpec((hb, rg, dh), lambda b, h: (h, 0, 0)),
                  pl.BlockSpec((hb, rg, dh), lambda b, h: (h, 0, 0)),
                  pl.BlockSpec((hb, 1, dh), lambda b, h: (h, 0, 0)),
                  pl.BlockSpec((1, hb * dh), lambda b, h: (0, h)),
                  pl.BlockSpec((1, hb, dh, dh), lambda b, h: (b, h, 0, 0))],
        out_specs=[pl.BlockSpec((1, T, hb * dh), lambda b, h: (b, 0, h)),
                   pl.BlockSpec((1, hb, dh, dh), lambda b, h: (b, h, 0, 0))],
        out_shape=[jax.ShapeDtypeStruct((B, T, H * dh), BF16),
                   jax.ShapeDtypeStruct((B, H, dh, dh), F32)],
        compiler_params=_params(2),
        name="retention",
    )(z_ret, z_ret, z_ret, z_ret, cos2, sin2, mask, qdec, kdec, cdec, gn, s0)


def _seg_allreduce(v, lane, op):
    for s in (1, 2, 4):
        up = pltpu.roll(v, LANES - s, axis=1)
        dn = pltpu.roll(v, s, axis=1)
        v = op(v, jnp.where((lane & s) == 0, up, dn))
    return v


def _mix_body(*refs, bb, tm, alpha, cap, n_steps, aliased):
    (yl_ref, yr_ref, x_ref, wo_ref, g1_ref, b1_ref, rw_ref, rb_ref, sg_ref, su_ref, sd_ref,
     tri_ref, cnt_in_ref) = refs[:13]
    refs = refs[14:] if aliased else refs[13:]
    base_ref, dest_ref, wts_ref, cnt_ref, xs_ref, carry_s, xpk_s, dv_s, ds_s, row_sems, idx_sem = refs
    i = pl.program_id(0)
    m = bb * tm
    d = x_ref.shape[-1]
    w = yl_ref.shape[1] // bb
    slot = i % 2
    chunk = m // ISSUE_CHUNKS

    def row_copy(sl, n, dst):
        return pltpu.make_async_copy(xpk_s.at[sl, n], xs_ref.at[dst], row_sems.at[sl])

    def issue_rows(sl, lo, hi):
        def one(n, c):
            for kk in range(TOP_K):
                row_copy(sl, n, ds_s[sl, kk, n]).start(priority=kk % 2)
            return c
        lax.fori_loop(lo, hi, one, 0)

    def issue_prev_chunk(c):
        for n in range(c * chunk, (c + 1) * chunk):
            for kk in range(TOP_K):
                row_copy(1 - slot, n, ds_s[1 - slot, kk, n]).start(priority=kk % 2)

    def drain(sl):
        def one(n, c):
            for _ in range(DRAIN_UNROLL * TOP_K):
                row_copy(sl, 0, 0).wait()
            return c
        lax.fori_loop(0, m // DRAIN_UNROLL, one, 0)

    @pl.when(i == 0)
    def _():
        carry_s[...] = jnp.zeros_like(carry_s)
        carry_s[0:1, :] = cnt_in_ref[...].astype(F32)
        xpk_s[1] = jnp.zeros(xpk_s.shape[1:], U32)
        spare = (N_EXPERTS * cap + lax.broadcasted_iota(I32, (TOP_K, m), 0) * m
                 + lax.broadcasted_iota(I32, (TOP_K, m), 1))
        dv_s[...] = spare
        first = pltpu.make_async_copy(dv_s, ds_s.at[1], idx_sem)
        first.start()
        first.wait()

    issue_prev_chunk(0)
    yl = jnp.concatenate([yl_ref[:, b * w:(b + 1) * w] for b in range(bb)], axis=0).astype(BF16)
    yr = yr_ref[...].reshape(m, yr_ref.shape[-1])
    mix = jnp.dot(jnp.concatenate([yl, yr], axis=1), wo_ref[...], preferred_element_type=F32)
    x1 = _layer_norm(alpha * x_ref[...].reshape(m, d) + mix, g1_ref[...], b1_ref[...])
    x1b = x1.astype(BF16)
    issue_prev_chunk(1)

    logits = jnp.dot(x1b, rw_ref[...], preferred_element_type=F32)

    s = jax.nn.sigmoid(logits)
    sb = s + rb_ref[...]
    lane = lax.broadcasted_iota(I32, (m, LANES), 1)
    e_id = lane & (N_EXPERTS - 1)
    e_f = e_id.astype(F32)
    low = lane < N_EXPERTS
    e_low = jnp.where(low, e_f, -1.0)
    grp = e_id >> 3
    big = jnp.float32(1e9)
    ninf = jnp.float32(-jnp.inf)

    m1 = _seg_allreduce(sb, lane, jnp.maximum)
    first_max = _seg_allreduce(jnp.where(sb == m1, e_f, big), lane, jnp.minimum)
    m2 = _seg_allreduce(jnp.where(e_f == first_max, ninf, sb), lane, jnp.maximum)
    gs = m1 + m2
    hg = jnp.dot(x1b, sg_ref[...], preferred_element_type=F32)
    issue_prev_chunk(2)
    rank = jnp.zeros((m, LANES), F32)
    for dgrp in range(1, N_GROUPS):
        other = pltpu.roll(gs, 8 * dgrp, axis=1)
        tie = jnp.where(grp >= dgrp, 1.0, 0.0)
        rank = rank + jnp.where(other > gs, 1.0, jnp.where(other == gs, tie, 0.0))
    v = jnp.where(rank < TOPK_GROUPS, jnp.where(low, sb, ninf), ninf)
    hu = jnp.dot(x1b, su_ref[...], preferred_element_type=F32)
    hs = (hg * jax.nn.sigmoid(hg) * hu).astype(BF16)

    idx_cols, w_cols = [], []
    sel = jnp.zeros((m, LANES), F32)
    for rnd in range(TOP_K):
        mx = jnp.max(v, axis=1, keepdims=True)
        idx = jnp.min(jnp.where(v == mx, e_f, big), axis=1, keepdims=True)
        hit = e_low == idx
        w_cols.append(jnp.sum(jnp.where(hit, s, 0.0), axis=1, keepdims=True))
        idx_cols.append(idx)
        v = jnp.where(hit, ninf, v)
        sel = jnp.where(hit, 1.0, sel)
        if rnd == TOP_K // 2 - 1:
            base_ref[...] = alpha * x1 + jnp.dot(hs, sd_ref[...], preferred_element_type=F32)
            issue_prev_chunk(3)

    packed = _pack_bf16_pair(x1)
    for sub in range(ROW_SUBLANES):
        xpk_s[slot, :, sub, :] = packed[:, sub * LANES:(sub + 1) * LANES]

    cum = jnp.dot(tri_ref[...], sel.astype(BF16), preferred_element_type=F32) + carry_s[0:1, :]
    carry_s[0:1, :] = carry_s[0:1, :] + jnp.sum(sel, axis=0, keepdims=True)
    cnt_ref[...] = carry_s[0:1, :].astype(I32)

    wsum = w_cols[0]
    for c in w_cols[1:]:
        wsum = wsum + c
    d_out = jnp.zeros((m, LANES), F32)
    w_out = jnp.zeros((m, LANES), F32)
    for kk in range(TOP_K):
        hit = e_low == idx_cols[kk]
        pk = jnp.sum(jnp.where(hit, cum, 0.0), axis=1, keepdims=True)
        d_out = jnp.where(lane == kk, idx_cols[kk] * float(cap) + pk, d_out)
        w_out = jnp.where(lane == kk, w_cols[kk] / wsum * ROUTED_SCALE, w_out)
    wts_ref[...] = w_out[:, :TOP_K]
    dest_t = jnp.transpose(d_out)[:TOP_K].astype(I32)
    dest_ref[0] = dest_t
    dv_s[...] = dest_t
    to_smem = pltpu.make_async_copy(dv_s, ds_s.at[slot], idx_sem)
    to_smem.start()
    to_smem.wait()

    drain(1 - slot)

    @pl.when(i == n_steps - 1)
    def _():
        issue_rows(slot, 0, m)
        drain(slot)


def _mix(y_lru2, y_ret, x, w, alpha, cnt_in, xs, cap):
    B, T, D = x.shape
    assert D == 2 * ROW_SUBLANES * LANES, "a packed token row must fill exactly one (8, 128) tile"
    W = y_ret.shape[-1]
    tm = min(MIX_ROWS, T)
    bb = min(B, MIX_ROWS // tm)
    m = bb * tm
    n = B * T
    nt = T // tm
    n_steps = (B // bb) * nt
    hs = w["sg"].shape[1]
    aliased = xs is not None
    tri = (lax.broadcasted_iota(I32, (m, m), 1) < lax.broadcasted_iota(I32, (m, m), 0)).astype(BF16)
    body = functools.partial(_mix_body, bb=bb, tm=tm, alpha=alpha, cap=cap, n_steps=n_steps, aliased=aliased)
    in_specs = [pl.BlockSpec((tm, bb * W), lambda i: (i % nt, i // nt)),
                pl.BlockSpec((bb, tm, W), lambda i: (i // nt, i % nt, 0)),
                pl.BlockSpec((bb, tm, D), lambda i: (i // nt, i % nt, 0)),
                _const_spec((2 * W, D)),
                _const_spec((1, D)),
                _const_spec((1, D)),
                _const_spec((D, LANES)),
                _const_spec((1, LANES)),
                _const_spec((D, hs)),
                _const_spec((D, hs)),
                _const_spec((hs, D)),
                _const_spec((m, m)),
                _const_spec((1, LANES))]
    args = [y_lru2, y_ret, x, w["wo"], w["g1"], w["b1"], w["rw"], w["rb"], w["sg"], w["su"], w["sd"],
            tri, cnt_in]
    if aliased:
        in_specs.append(pl.BlockSpec(memory_space=pl.ANY))
        args.append(xs)
    return pl.pallas_call(
        body,
        grid=(n_steps,),
        in_specs=in_specs,
        out_specs=[pl.BlockSpec((m, D), lambda i: (i, 0)),
                   pl.BlockSpec((1, TOP_K, m), lambda i: (i, 0, 0)),
                   pl.BlockSpec((m, TOP_K), lambda i: (i, 0)),
                   pl.BlockSpec((1, LANES), lambda i: (0, 0)),
                   pl.BlockSpec(memory_space=pl.ANY)],
        out_shape=[jax.ShapeDtypeStruct((n, D), F32),
                   jax.ShapeDtypeStruct((n_steps, TOP_K, m), I32),
                   jax.ShapeDtypeStruct((n, TOP_K), F32),
                   jax.ShapeDtypeStruct((1, LANES), I32),
                   jax.ShapeDtypeStruct((N_EXPERTS * cap + MIX_ROWS * TOP_K, ROW_SUBLANES, LANES), U32)],
        scratch_shapes=[pltpu.VMEM((8, LANES), F32),
                        pltpu.VMEM((2, m, ROW_SUBLANES, LANES), U32),
                        pltpu.VMEM((TOP_K, m), I32),
                        pltpu.SMEM((2, TOP_K, m), I32),
                        pltpu.SemaphoreType.DMA((2,)),
                        pltpu.SemaphoreType.DMA(())],
        input_output_aliases={13: 4} if aliased else {},
        compiler_params=_params(1),
        name="mix_router",
    )(*args)


def _gmm_body(ge_ref, gr_ref, gn_ref, gt_ref, gx_ref, gs_ref, xs_ref, wg_ref, wu_ref, wd_ref, ys_ref,
              wgf_s, wuf_s, wdf_s, wgu_s, wd_s, x_s, y_s, w_sems, in_sems, out_sems, *, tm, n_items):
    i = pl.program_id(0)
    hid = wg_ref.shape[2]
    slot = i % 2

    def weight_copies(expert, sl):
        return [pltpu.make_async_copy(wg_ref.at[expert], wgf_s.at[sl], w_sems.at[sl]),
                pltpu.make_async_copy(wu_ref.at[expert], wuf_s.at[sl], w_sems.at[sl]),
                pltpu.make_async_copy(wd_ref.at[expert], wdf_s.at[sl], w_sems.at[sl])]

    def tile_copies(to_vmem, item, sl):
        r0 = pl.multiple_of(gr_ref[item] * tm, tm)
        out = []
        for sub in range(ROW_SUBLANES):
            cols = pl.ds(sub * LANES, LANES)
            if to_vmem:
                out.append(pltpu.make_async_copy(xs_ref.at[pl.ds(r0, tm), sub, :], x_s.at[sl, :, cols],
                                                 in_sems.at[sl]))
            else:
                out.append(pltpu.make_async_copy(y_s.at[sl, :, cols], ys_ref.at[pl.ds(r0, tm), sub, :],
                                                 out_sems.at[sl]))
        return out

    @pl.when(i == 0)
    def _():
        for cp in weight_copies(ge_ref[0], 0):
            cp.start()
        for cp in tile_copies(True, 0, 0):
            cp.start()
        y_s[...] = jnp.zeros_like(y_s)

    nxt = jnp.minimum(i + 1, n_items - 1)

    @pl.when((i + 1 < n_items) & (gn_ref[nxt] > 0))
    def _():
        for cp in tile_copies(True, nxt, 1 - slot):
            cp.start()

    e = ge_ref[i]
    e_prev = ge_ref[jnp.maximum(i - 1, 0)]

    @pl.when((i == 0) | (e != e_prev))
    def _():
        wsl = gs_ref[i]
        nxt_e = gx_ref[i]

        @pl.when(nxt_e >= 0)
        def _():
            for cp in weight_copies(nxt_e, 1 - wsl):
                cp.start()

        for cp in weight_copies(e, wsl):
            cp.wait()
        wgu_s[:, :hid] = wgf_s[wsl].astype(BF16)
        wgu_s[:, hid:] = wuf_s[wsl].astype(BF16)
        wd_s[...] = wdf_s[wsl].astype(BF16)

    n_valid = gn_ref[i]

    @pl.when(n_valid > 0)
    def _():
        for cp in tile_copies(True, i, slot):
            cp.wait()

        @pl.when(i >= 2)
        def _():
            for cp in tile_copies(False, i, slot):
                cp.wait()

        def expert_mlp(rows):
            pk = x_s[slot, :rows]
            valid = lax.broadcasted_iota(I32, pk.shape, 0) < n_valid
            xa, xb = _unpack_bf16_pair(jnp.where(valid, pk, jnp.uint32(0)))
            x = jnp.concatenate([xa.astype(BF16), xb.astype(BF16)], axis=1)
            h2 = jnp.dot(x, wgu_s[...], preferred_element_type=F32)
            hg = h2[:, :hid]
            h = (hg * jax.nn.sigmoid(hg) * h2[:, hid:]).astype(BF16)
            y_s[slot, :rows] = _pack_bf16_pair(jnp.dot(h, wd_s[...], preferred_element_type=F32))

        @pl.when(n_valid > tm // 2)
        def _():
            expert_mlp(tm)

        @pl.when(n_valid <= tm // 2)
        def _():
            expert_mlp(tm // 2)

        for cp in tile_copies(False, i, slot):
            cp.start()

    @pl.when(i == n_items - 1)
    def _():
        total = gt_ref[0]

        @pl.when(total >= 2)
        def _():
            for cp in tile_copies(False, 0, total % 2):
                cp.wait()

        for cp in tile_copies(False, 0, (total + 1) % 2):
            cp.wait()


def _gmm_metadata(counts, cap, tm, n_items):
    e = counts.shape[0]
    tiles = (counts + tm - 1) // tm
    item_end = jnp.cumsum(tiles)
    total = item_end[-1]
    it = jnp.arange(n_items, dtype=I32)
    itc = jnp.minimum(it, total - 1)
    ge = jnp.sum((item_end[None, :] <= itc[:, None]).astype(I32), axis=1)
    onehot = ge[:, None] == jnp.arange(e, dtype=I32)[None, :]
    start = jnp.sum(jnp.where(onehot, (item_end - tiles)[None, :], 0), axis=1)
    cnt = jnp.sum(jnp.where(onehot, counts[None, :], 0), axis=1)
    j = itc - start
    gr = ge * (cap // tm) + j
    gn = jnp.where(it < total, jnp.clip(cnt - j * tm, 0, tm), 0)
    ids = jnp.arange(e, dtype=I32)
    live = tiles > 0
    later = live[None, :] & (ids[None, :] > ids[:, None])
    next_e = jnp.min(jnp.where(later, ids[None, :], e), axis=1)
    next_e = jnp.where(next_e < e, next_e, -1)
    wslot = (jnp.cumsum(live.astype(I32)) - 1) % 2
    gx = jnp.sum(jnp.where(onehot, next_e[None, :], 0), axis=1)
    gs = jnp.sum(jnp.where(onehot, wslot[None, :], 0), axis=1)
    return tuple(a.astype(I32) for a in (ge, gr, gn, total.reshape(1), gx, gs))


def _gmm(xs, counts, cap, n_tokens, wg, wu, wd):
    tm = GMM_ROWS
    e, d, hid = wg.shape
    n_items = (n_tokens * TOP_K) // tm + e
    meta = _gmm_metadata(counts, cap, tm, n_items)
    body = functools.partial(_gmm_body, tm=tm, n_items=n_items)
    grid_spec = pltpu.PrefetchScalarGridSpec(
        num_scalar_prefetch=6,
        grid=(n_items,),
        in_specs=[pl.BlockSpec(memory_space=pl.ANY)] * 4,
        out_specs=pl.BlockSpec(memory_space=pl.ANY),
        scratch_shapes=[pltpu.VMEM((2, d, hid), F32), pltpu.VMEM((2, d, hid), F32),
                        pltpu.VMEM((2, hid, d), F32),
                        pltpu.VMEM((d, 2 * hid), BF16), pltpu.VMEM((hid, d), BF16),
                        pltpu.VMEM((2, tm, d // 2), U32), pltpu.VMEM((2, tm, d // 2), U32),
                        pltpu.SemaphoreType.DMA((2,)), pltpu.SemaphoreType.DMA((2,)),
                        pltpu.SemaphoreType.DMA((2,))],
    )
    return pl.pallas_call(
        body,
        grid_spec=grid_spec,
        out_shape=jax.ShapeDtypeStruct(xs.shape, U32),
        compiler_params=_params(1),
        name="expert_gmm",
    )(*meta, xs, wg, wu, wd)


def _final_body(d0_ref, d1_ref, dn_ref, wts_ref, base_ref, p_ref, ys_ref, pg_ref, pb_ref, pp_ref, g2_ref, b2_ref,
                out_ref, rows_s, sum_s, wrep_s, sems, *, bb, tm, n_steps):
    i = pl.program_id(0)
    m = bb * tm
    d = base_ref.shape[1]
    slot = lax.rem(i, ROW_BUFFERS)
    ahead = lax.rem(i + 2, ROW_BUFFERS)

    def row_copy(sl, d_row, kk, n):
        return pltpu.make_async_copy(ys_ref.at[d_row], rows_s.at[sl, kk, n], sems.at[sl])

    def issue_token(sl, dref, n):
        for kk in range(TOP_K):
            row_copy(sl, dref[0, kk, n], kk, n).start(priority=kk % 2)

    def issue_tile(sl, dref):
        def one(n, c):
            issue_token(sl, dref, n)
            return c
        lax.fori_loop(0, m, one, 0)

    def issue_ahead(lo, hi):
        for n in range(lo, hi):
            issue_token(ahead, dn_ref, n)

    def drain(sl):
        def one(n, c):
            for _ in range(DRAIN_UNROLL * TOP_K):
                row_copy(sl, 0, 0, 0).wait()
            return c
        lax.fori_loop(0, m // DRAIN_UNROLL, one, 0)

    @pl.when(i == 0)
    def _():
        issue_tile(0, d0_ref)
        issue_tile(1, d1_ref)

    drain(slot)

    wts = wts_ref[...]
    for kk in range(TOP_K):
        wk = jnp.broadcast_to(wts[:, kk:kk + 1], (m, LANES))
        wrep_s[kk] = _pack_bf16_pair(jnp.concatenate([wk, wk], axis=1))

    def combine(n, c):
        acc = jnp.zeros((2 * ROW_SUBLANES, LANES), BF16)
        for kk in range(TOP_K):
            row = pltpu.bitcast(rows_s[slot, kk, n], BF16)
            wk = jnp.broadcast_to(wrep_s[kk, pl.ds(n, 1), :], (ROW_SUBLANES, LANES))
            acc = acc + pltpu.bitcast(wk, BF16) * row
        sum_s[n] = pltpu.bitcast(acc, U32)
        return c

    lax.fori_loop(0, m, combine, 0, unroll=2)
    issue_ahead(0, m // 2)
    routed = jnp.concatenate(_unpack_bf16_pair(
        jnp.concatenate([sum_s[:, sub, :] for sub in range(ROW_SUBLANES)], axis=1)), axis=1)
    x2 = _layer_norm(base_ref[...] + routed, g2_ref[...], b2_ref[...])
    issue_ahead(m // 2, m)
    gate = jax.nn.sigmoid(jnp.dot(x2.astype(BF16), pg_ref[...], preferred_element_type=F32) + pb_ref[...])
    proj = jnp.dot(p_ref[...].reshape(m, p_ref.shape[-1]).astype(BF16), pp_ref[...], preferred_element_type=F32)
    out_ref[...] = (x2 + gate * proj).reshape(bb, tm, d)

    @pl.when(i == n_steps - 1)
    def _():
        drain(lax.rem(i + 1, ROW_BUFFERS))
        drain(ahead)


def _final(base, dest, wts, p, ys, w, B, T):
    n, D = base.shape
    tm = min(MIX_ROWS, T)
    bb = min(B, MIX_ROWS // tm)
    m = bb * tm
    nt = T // tm
    n_steps = (B // bb) * nt
    pd = p.shape[-1]
    body = functools.partial(_final_body, bb=bb, tm=tm, n_steps=n_steps)
    return pl.pallas_call(
        body,
        grid=(n_steps,),
        in_specs=[pl.BlockSpec((1, TOP_K, m), lambda i: (0, 0, 0), memory_space=pltpu.SMEM),
                  pl.BlockSpec((1, TOP_K, m), lambda i: (min(1, n_steps - 1), 0, 0), memory_space=pltpu.SMEM),
                  pl.BlockSpec((1, TOP_K, m), lambda i: (jnp.minimum(i + 2, n_steps - 1), 0, 0),
                               memory_space=pltpu.SMEM),
                  pl.BlockSpec((m, TOP_K), lambda i: (i, 0)),
                  pl.BlockSpec((m, D), lambda i: (i, 0)),
                  pl.BlockSpec((bb, tm, pd), lambda i: (i // nt, i % nt, 0)),
                  pl.BlockSpec(memory_space=pl.ANY),
                  _const_spec((D, D)),
                  _const_spec((1, D)),
                  _const_spec((pd, D)),
                  _const_spec((1, D)),
                  _const_spec((1, D))],
        out_specs=pl.BlockSpec((bb, tm, D), lambda i: (i // nt, i % nt, 0)),
        out_shape=jax.ShapeDtypeStruct((B, T, D), F32),
        scratch_shapes=[pltpu.VMEM((ROW_BUFFERS, TOP_K, m, ROW_SUBLANES, LANES), U32),
                        pltpu.VMEM((m, ROW_SUBLANES, LANES), U32),
                        pltpu.VMEM((TOP_K, m, LANES), U32),
                        pltpu.SemaphoreType.DMA((ROW_BUFFERS,))],
        compiler_params=_params(1),
        name="combine_final",
    )(dest, dest, dest, wts, base, p, ys, w["pg"], w["pb"], w["pp"], w["g2"], w["b2"])


def _prep_layer(prm):
    (w_in, conv_w, conv_b, lru_wa, lru_ba, lru_wx, lru_bx, lru_lambda, ret_gn, w_out, ln1_g, ln1_b,
     router_w, router_b, e_gate, e_up, e_down, s_gate, s_up, s_down, ln2_g, ln2_b,
     ple_w_proj, ple_w_gate, ple_b_gate) = prm
    row = lambda v: v.reshape(1, -1)
    rw2 = jnp.concatenate([router_w, router_w], axis=1).astype(BF16)
    return dict(
        w_in=w_in.astype(BF16), conv_w=conv_w, conv_b=row(conv_b), wa=lru_wa.astype(BF16), ba=row(lru_ba),
        wx=lru_wx.astype(BF16), bx=row(lru_bx), lam=row(lru_lambda), gn=row(ret_gn), wo=w_out.astype(BF16),
        g1=row(ln1_g), b1=row(ln1_b), rw=rw2, rb=row(jnp.concatenate([router_b, router_b])),
        e_gate=e_gate, e_up=e_up, e_down=e_down, sg=s_gate.astype(BF16), su=s_up.astype(BF16),
        sd=s_down.astype(BF16), g2=row(ln2_g), b2=row(ln2_b), pp=ple_w_proj.astype(BF16),
        pg=ple_w_gate.astype(BF16), pb=row(ple_b_gate))


def _mixers(x, conv_st, lru_st, ret_st, pos0, chunk, w):
    B, T, D = x.shape
    W = conv_st.shape[-1]
    z_lru, z_ret = _inproj(x, w["w_in"], 2 * W)
    y_ret, new_ret = _retention(z_ret, ret_st, w["gn"], pos0, chunk)
    y_lru, conv_tm, new_lru = _lru(z_lru, jnp.transpose(conv_st, (1, 0, 2)), lru_st,
                                   w["conv_w"], w["conv_b"], w["wa"], w["ba"], w["wx"], w["bx"], w["lam"],
                                   after=new_ret)
    return y_lru.reshape(T, B * W), y_ret, jnp.transpose(conv_tm, (1, 0, 2)), new_lru, new_ret


def kernel(x_prompt, x_sample, p_prompt, p_sample, state_conv, state_lru, state_ret, w_in, conv_w, conv_b,
           lru_wa, lru_ba, lru_wx, lru_bx, lru_lambda, ret_gn, w_out, ln1_g, ln1_b, router_w, router_b,
           exp_w_gate, exp_w_up, exp_w_down, sh_w_gate, sh_w_up, sh_w_down, ln2_g, ln2_b,
           ple_w_proj, ple_w_gate, ple_b_gate):
    depth = w_in.shape[0]
    alpha = (2 * depth) ** 0.25
    b_p, t_p, _ = x_prompt.shape
    b_s, t_s, _ = x_sample.shape
    W = state_conv.shape[-1]
    H, dh = state_ret.shape[2], state_ret.shape[3]
    n_tokens = b_p * t_p + b_s * t_s
    cap = -(-n_tokens // GMM_ROWS) * GMM_ROWS
    hp, hs = x_prompt, x_sample
    outs = [[] for _ in range(6)]
    for i in range(depth):
        prm = (w_in[i], conv_w[i], conv_b[i], lru_wa[i], lru_ba[i], lru_wx[i], lru_bx[i], lru_lambda[i],
               ret_gn[i], w_out[i], ln1_g[i], ln1_b[i], router_w[i], router_b[i], exp_w_gate[i], exp_w_up[i],
               exp_w_down[i], sh_w_gate[i], sh_w_up[i], sh_w_down[i], ln2_g[i], ln2_b[i],
               ple_w_proj[i], ple_w_gate[i], ple_b_gate[i])
        w = _prep_layer(prm)
        zc = jnp.zeros((b_p, CONV_WIDTH - 1, W), x_prompt.dtype)
        zl = jnp.zeros((b_p, W), F32)
        zr = jnp.zeros((b_p, H, dh, dh), F32)
        yl_p, yr_p, c_p, l_p, r_p = _mixers(hp, zc, zl, zr, 0, CHUNK, w)
        yl_s, yr_s, c_s, l_s, r_s = _mixers(hs, state_conv[i], state_lru[i], state_ret[i], PAST_LEN, t_s, w)
        for o, val in zip(outs, (c_p, l_p, r_p, c_s, l_s, r_s)):
            o.append(val)
        base_p, dest_p, wts_p, cnt_p, xs = _mix(yl_p, yr_p, hp, w, alpha, jnp.zeros((1, LANES), I32), None, cap)
        base_s, dest_s, wts_s, cnt_all, xs = _mix(yl_s, yr_s, hs, w, alpha, cnt_p, xs, cap)
        ys = _gmm(xs, cnt_all[0, :N_EXPERTS], cap, n_tokens, w["e_gate"], w["e_up"], w["e_down"])
        hp = _final(base_p, dest_p, wts_p, p_prompt[i], ys, w, b_p, t_p)
        hs = _final(base_s, dest_s, wts_s, p_sample[i], ys, w, b_s, t_s)
    return (hp, hs) + tuple(jnp.stack(o) for o in outs)
```

```python
import functools

import jax
import jax.numpy as jnp
from jax import lax
from jax.experimental import pallas as pl
from jax.experimental.pallas import tpu as pltpu

F32 = jnp.float32
BF16 = jnp.bfloat16
U32 = jnp.uint32
I32 = jnp.int32

CHUNK = 64
PAST_LEN = 1024
CONV_WIDTH = 4
LRU_C = 8.0
RET_HEADS = 8
ROPE_BASE = 10000.0
N_EXPERTS = 64
TOP_K = 8
N_GROUPS = 8
TOPK_GROUPS = 4
ROUTED_SCALE = 2.5
LN_EPS = 1e-5
GN_EPS = 1e-6

LANES = 128
ROW_SUBLANES = 8
DRAIN_UNROLL = 8
ISSUE_CHUNKS = 4
ROW_BUFFERS = 3
ROWS_PER_STEP = 512
INPROJ_COLS = 512
MIX_ROWS = 256
GMM_ROWS = 512
RET_GROUP_ROWS = 256
LRU_TIME_TILE = 64
LRU_GATE_ROWS = 128
RET_ALL_HEADS_ELEMS = 64 * 1024
VMEM_LIMIT = 56 * 1024 * 1024


def _const_spec(shape):
    zeros = (0,) * len(shape)
    return pl.BlockSpec(shape, lambda *_: zeros, pipeline_mode=pl.Buffered(1))


def _params(n_axes):
    return pltpu.CompilerParams(dimension_semantics=("arbitrary",) * n_axes,
                                vmem_limit_bytes=VMEM_LIMIT)


def _layer_norm(x, g, b):
    mu = jnp.mean(x, axis=-1, keepdims=True)
    xc = x - mu
    var = jnp.mean(xc * xc, axis=-1, keepdims=True)
    return xc * lax.rsqrt(var + LN_EPS) * g + b


def _pack_bf16_pair(x):
    c = x.shape[1] // 2
    xb = x.astype(BF16).astype(F32)
    hi = pltpu.bitcast(xb[:, :c], U32)
    lo = pltpu.bitcast(xb[:, c:], U32)
    return hi | (lo >> 16)


def _unpack_bf16_pair(pk):
    hi = pltpu.bitcast(pk & jnp.uint32(0xFFFF0000), F32)
    lo = pltpu.bitcast(pk << 16, F32)
    return hi, lo


def _inproj_body(x_ref, w_ref, zl_ref, zr_ref, *, bb, tm, lru_cols, tn):
    d = x_ref.shape[-1]
    x = x_ref[...].reshape(bb * tm, d).astype(BF16)
    for j in range(w_ref.shape[1] // tn):
        c0 = j * tn
        acc = jnp.dot(x, w_ref[:, c0:c0 + tn], preferred_element_type=F32)
        for b in range(bb):
            rows = acc[b * tm:(b + 1) * tm]
            if c0 < lru_cols:
                zl_ref[:, b * lru_cols + c0:b * lru_cols + c0 + tn] = rows
            else:
                zr_ref[b, :, c0 - lru_cols:c0 - lru_cols + tn] = rows.astype(BF16)


def _inproj_cols_body(x_ref, w_ref, zl_ref, zr_ref, xb_s, *, n_lru):
    j = pl.program_id(0)
    bb, tm, d = x_ref.shape

    @pl.when(j == 0)
    def _():
        xb_s[...] = x_ref[...].reshape(bb * tm, d).astype(BF16)

    acc = jnp.dot(xb_s[...], w_ref[...], preferred_element_type=F32)

    @pl.when(j < n_lru)
    def _():
        for b in range(bb):
            zl_ref[:, b, :] = acc[b * tm:(b + 1) * tm]

    @pl.when(j >= n_lru)
    def _():
        zr_ref[...] = acc.reshape(zr_ref.shape).astype(BF16)


def _inproj_cols(x, w_in_b, lru_cols):
    B, T, D = x.shape
    n_cols = w_in_b.shape[1]
    tn = INPROJ_COLS
    n_lru = lru_cols // tn
    body = functools.partial(_inproj_cols_body, n_lru=n_lru)
    return pl.pallas_call(
        body,
        grid=(n_cols // tn,),
        in_specs=[_const_spec((B, T, D)),
                  pl.BlockSpec((D, tn), lambda j: (0, j))],
        out_specs=[pl.BlockSpec((T, B, tn), lambda j: (0, 0, jnp.minimum(j, n_lru - 1))),
                   pl.BlockSpec((B, T, tn), lambda j: (0, 0, jnp.maximum(j - n_lru, 0)))],
        out_shape=[jax.ShapeDtypeStruct((T, B, lru_cols), F32),
                   jax.ShapeDtypeStruct((B, T, n_cols - lru_cols), BF16)],
        scratch_shapes=[pltpu.VMEM((B * T, D), BF16)],
        compiler_params=_params(1),
        name="inproj",
    )(x, w_in_b)


def _inproj(x, w_in_b, lru_cols):
    B, T, D = x.shape
    if B * T <= ROWS_PER_STEP:
        return _inproj_cols(x, w_in_b, lru_cols)
    z_lru, z_ret = _inproj_rows(x, w_in_b, lru_cols)
    return z_lru.reshape(T, B, lru_cols), z_ret


def _inproj_rows(x, w_in_b, lru_cols):
    B, T, D = x.shape
    n_cols = w_in_b.shape[1]
    ret_cols = n_cols - lru_cols
    tm = min(ROWS_PER_STEP, T)
    bb = min(B, ROWS_PER_STEP // tm)
    body = functools.partial(_inproj_body, bb=bb, tm=tm, lru_cols=lru_cols, tn=INPROJ_COLS)
    return pl.pallas_call(
        body,
        grid=(B // bb, T // tm),
        in_specs=[pl.BlockSpec((bb, tm, D), lambda b, t: (b, t, 0)),
                  _const_spec((D, n_cols))],
        out_specs=[pl.BlockSpec((tm, bb * lru_cols), lambda b, t: (t, b)),
                   pl.BlockSpec((bb, tm, ret_cols), lambda b, t: (b, t, 0))],
        out_shape=[jax.ShapeDtypeStruct((T, B * lru_cols), F32),
                   jax.ShapeDtypeStruct((B, T, ret_cols), BF16)],
        compiler_params=_params(2),
        name="inproj",
    )(x, w_in_b)


def _lru_body(xl_ref, gl_ref, conv0_ref, h0_ref, cw_ref, cb_ref, wa_ref, ba_ref, wx_ref, bx_ref, lam_ref,
              after_ref, y_ref, conv_out_ref, h_out_ref, xp_s, a_s, b_s, h_s, *, tt, rows):
    del after_ref
    i = pl.program_id(0)
    B, W = h0_ref.shape
    nblk = wa_ref.shape[0]
    blk = W // nblk

    @pl.when(i == 0)
    def _():
        xp_s[0:CONV_WIDTH - 1] = conv0_ref[...]
        h_s[...] = h0_ref[...]

    xp_s[CONV_WIDTH - 1:] = xl_ref[...]

    lam = lam_ref[...]
    neg = -lam
    softplus = jnp.maximum(neg, 0.0) + jnp.log1p(jnp.exp(-jnp.abs(neg)))
    decay = (-LRU_C) * softplus

    def gates(c, carry):
        t0 = pl.multiple_of(c * rows, rows)
        xc = cb_ref[...].reshape(1, 1, W)
        for j in range(CONV_WIDTH):
            xc = xc + xp_s[pl.ds(t0 + j, rows)] * cw_ref[j:j + 1].reshape(1, 1, W)
        xc2 = xc.reshape(rows * B, W)
        xcb = xc2.astype(BF16)
        r_parts, i_parts = [], []
        for n in range(nblk):
            xb = xcb[:, n * blk:(n + 1) * blk]
            r_parts.append(jnp.dot(xb, wa_ref[n], preferred_element_type=F32))
            i_parts.append(jnp.dot(xb, wx_ref[n], preferred_element_type=F32))
        r = jax.nn.sigmoid(jnp.concatenate(r_parts, axis=1) + ba_ref[...])
        ig = jax.nn.sigmoid(jnp.concatenate(i_parts, axis=1) + bx_ref[...])
        a = jnp.exp(decay * r)
        bterm = jnp.sqrt(1.0 - a * a) * (ig * xc2)
        a_s[pl.ds(t0, rows)] = a.reshape(rows, B, W)
        b_s[pl.ds(t0, rows)] = bterm.reshape(rows, B, W)
        return carry

    lax.fori_loop(0, tt // rows, gates, 0)

    def step(t, h):
        hn = a_s[t] * h + b_s[t]
        y_ref[t] = hn * jax.nn.gelu(gl_ref[t])
        return hn

    h_last = lax.fori_loop(0, tt, step, h_s[...], unroll=8)
    h_s[...] = h_last
    tail = xp_s[tt:tt + CONV_WIDTH - 1]
    xp_s[0:CONV_WIDTH - 1] = tail
    conv_out_ref[...] = tail
    h_out_ref[...] = h_last


def _lru(z_lru3, conv0_tm, h0, conv_w, conv_b, wa_b, ba, wx_b, bx, lam, after):
    T, B, W2 = z_lru3.shape
    W = W2 // 2
    tt = min(LRU_TIME_TILE, T)
    rows = max(1, min(tt, LRU_GATE_ROWS // B))
    body = functools.partial(_lru_body, tt=tt, rows=rows)
    nb = wa_b.shape[0]
    blk = W // nb
    return pl.pallas_call(
        body,
        grid=(T // tt,),
        in_specs=[pl.BlockSpec((tt, B, W), lambda t: (t, 0, 0)),
                  pl.BlockSpec((tt, B, W), lambda t: (t, 0, 1)),
                  _const_spec((CONV_WIDTH - 1, B, W)),
                  _const_spec((B, W)),
                  _const_spec((CONV_WIDTH, W)),
                  _const_spec((1, W)),
                  _const_spec((nb, blk, blk)),
                  _const_spec((1, W)),
                  _const_spec((nb, blk, blk)),
                  _const_spec((1, W)),
                  _const_spec((1, W)),
                  pl.BlockSpec(memory_space=pl.ANY)],
        out_specs=[pl.BlockSpec((tt, B, W), lambda t: (t, 0, 0)),
                   pl.BlockSpec((CONV_WIDTH - 1, B, W), lambda t: (0, 0, 0)),
                   pl.BlockSpec((B, W), lambda t: (0, 0))],
        out_shape=[jax.ShapeDtypeStruct((T, B, W), F32),
                   jax.ShapeDtypeStruct((CONV_WIDTH - 1, B, W), F32),
                   jax.ShapeDtypeStruct((B, W), F32)],
        scratch_shapes=[pltpu.VMEM((tt + CONV_WIDTH - 1, B, W), F32),
                        pltpu.VMEM((tt, B, W), F32),
                        pltpu.VMEM((tt, B, W), F32),
                        pltpu.VMEM((B, W), F32)],
        compiler_params=_params(1),
        name="rglru",
    )(z_lru3, z_lru3, conv0_tm, h0, conv_w, conv_b, wa_b, ba, wx_b, bx, lam, after)


def _ret_body(q_ref, k_ref, v_ref, g_ref, cos_ref, sin_ref, mask_ref, qdec_ref, kdec_ref, cdec_ref, gn_ref,
              s0_ref, y_ref, s_out_ref, *, rg, hb, dh):
    T = q_ref.shape[1]
    scale = dh ** -0.5

    def rope(t, cos, sin):
        return t * cos + pltpu.roll(t, dh // 2, axis=1) * sin

    for hh in range(hb):
        cols = slice(hh * dh, (hh + 1) * dh)

        def group(c, s, hh=hh, cols=cols):
            r0 = pl.multiple_of(c * rg, rg)
            rws = pl.ds(r0, rg)
            cos = cos_ref[rws, :]
            sin = sin_ref[rws, :]
            q = rope(q_ref[0, rws, cols].astype(F32), cos, sin)
            k = rope(k_ref[0, rws, cols].astype(F32), cos, sin) * scale
            v = v_ref[0, rws, cols]
            scores = lax.dot_general(q.astype(BF16), k.astype(BF16), (((1,), (1,)), ((), ())),
                                     preferred_element_type=F32)
            scores = scores * mask_ref[hh]
            o = jnp.dot(scores.astype(BF16), v, preferred_element_type=F32)
            o = o + jnp.dot((q * qdec_ref[hh]).astype(BF16), s.astype(BF16), preferred_element_type=F32)
            kd = (k * kdec_ref[hh]).astype(BF16)
            kv = lax.dot_general(kd, v, (((0,), (0,)), ((), ())), preferred_element_type=F32)
            s_new = cdec_ref[hh] * s + kv
            mu = jnp.mean(o, axis=-1, keepdims=True)
            oc = o - mu
            var = jnp.mean(oc * oc, axis=-1, keepdims=True)
            on = oc * lax.rsqrt(var + GN_EPS) * gn_ref[:, cols]
            g = g_ref[0, rws, cols].astype(F32)
            y_ref[0, rws, cols] = (g * jax.nn.sigmoid(g) * on).astype(y_ref.dtype)
            return s_new

        n_groups = T // rg
        s_out_ref[0, hh] = lax.fori_loop(0, n_groups, group, s0_ref[0, hh],
                                         unroll=4 if n_groups % 4 == 0 else 1)


def _retention_tables(T, pos0, chunk, rg, dh):
    half = dh // 2
    inv = ROPE_BASE ** (-jnp.arange(half, dtype=F32) / half)
    pos = pos0 + jnp.arange(T)
    ang = pos.astype(F32)[:, None] * inv[None, :]
    cos, sin = jnp.cos(ang), jnp.sin(ang)
    cos2 = jnp.concatenate([cos, cos], axis=1)
    sin2 = jnp.concatenate([-sin, sin], axis=1)
    log_g = jnp.log1p(-jnp.exp2(-5.0 - jnp.arange(RET_HEADS, dtype=F32)))[:, None, None]
    idx = jnp.arange(rg, dtype=F32)
    ci = jnp.floor(idx / chunk)
    diff = idx[:, None] - idx[None, :]
    same = ci[:, None] == ci[None, :]
    earlier = ci[None, :] < ci[:, None]
    dist = jnp.where(same, jnp.abs(diff), diff)
    mask = jnp.where(same | earlier, jnp.exp(dist[None] * log_g), 0.0)
    ones = jnp.ones((1, 1, dh), F32)
    qdec = jnp.exp((idx + 1.0)[None, :, None] * log_g) * ones
    kdec = jnp.exp((rg - 1.0 - idx)[None, :, None] * log_g) * ones
    cdec = jnp.exp(rg * log_g) * ones
    return cos2, sin2, mask, qdec, kdec, cdec


def _retention(z_ret, s0, gn, pos0, chunk):
    B, T, C4 = z_ret.shape
    H = RET_HEADS
    dh = C4 // (4 * H)
    rg = min(T, max(chunk, (RET_GROUP_ROWS // chunk) * chunk))
    hb = H if T * H * dh <= RET_ALL_HEADS_ELEMS else 1
    nh = H // hb
    cos2, sin2, mask, qdec, kdec, cdec = _retention_tables(T, pos0, chunk, rg, dh)
    body = functools.partial(_ret_body, rg=rg, hb=hb, dh=dh)
    col = lambda off: (lambda b, h: (b, 0, off * nh + h))
    return pl.pallas_call(
        body,
        grid=(B, nh),
        in_specs=[pl.BlockSpec((1, T, hb * dh), col(0)),
                  pl.BlockSpec((1, T, hb * dh), col(1)),
                  pl.BlockSpec((1, T, hb * dh), col(2)),
                  pl.BlockSpec((1, T, hb * dh), col(3)),
                  _const_spec((T, dh)),
                  _const_spec((T, dh)),
                  pl.BlockSpec((hb, rg, rg), lambda b, h: (h, 0, 0)),
                  pl.BlockSpec((hb, rg, dh), lambda b, h: (h, 0, 0)),
                  pl.BlockSpec((hb, rg, dh), lambda b, h: (h, 0, 0)),
                  pl.BlockSpec((hb, 1, dh), lambda b, h: (h, 0, 0)),
                  pl.BlockSpec((1, hb * dh), lambda b, h: (0, h)),
                  pl.BlockSpec((1, hb, dh, dh), lambda b, h: (b, h, 0, 0))],
        out_specs=[pl.BlockSpec((1, T, hb * dh), lambda b, h: (b, 0, h)),
                   pl.BlockSpec((1, hb, dh, dh), lambda b, h: (b, h, 0, 0))],
        out_shape=[jax.ShapeDtypeStruct((B, T, H * dh), BF16),
                   jax.ShapeDtypeStruct((B, H, dh, dh), F32)],
        compiler_params=_params(2),
        name="retention",
    )(z_ret, z_ret, z_ret, z_ret, cos2, sin2, mask, qdec, kdec, cdec, gn, s0)


def _seg_allreduce(v, lane, op):
    for s in (1, 2, 4):
        up = pltpu.roll(v, LANES - s, axis=1)
        dn = pltpu.roll(v, s, axis=1)
        v = op(v, jnp.where((lane & s) == 0, up, dn))
    return v


def _mix_body(*refs, bb, tm, alpha, cap, n_steps, aliased):
    (yl_ref, yr_ref, x_ref, wo_ref, g1_ref, b1_ref, rw_ref, rb_ref, sg_ref, su_ref, sd_ref,
     tri_ref, cnt_in_ref) = refs[:13]
    refs = refs[14:] if aliased else refs[13:]
    base_ref, dest_ref, wts_ref, cnt_ref, xs_ref, carry_s, xpk_s, dv_s, ds_s, row_sems, idx_sem = refs
    i = pl.program_id(0)
    m = bb * tm
    d = x_ref.shape[-1]
    w = yl_ref.shape[1] // bb
    slot = i % 2
    chunk = m // ISSUE_CHUNKS

    def row_copy(sl, n, dst):
        return pltpu.make_async_copy(xpk_s.at[sl, n], xs_ref.at[dst], row_sems.at[sl])

    def issue_rows(sl, lo, hi):
        def one(n, c):
            for kk in range(TOP_K):
                row_copy(sl, n, ds_s[sl, kk, n]).start(priority=kk % 2)
            return c
        lax.fori_loop(lo, hi, one, 0)

    def issue_prev_chunk(c):
        for n in range(c * chunk, (c + 1) * chunk):
            for kk in range(TOP_K):
                row_copy(1 - slot, n, ds_s[1 - slot, kk, n]).start(priority=kk % 2)

    def drain(sl):
        def one(n, c):
            for _ in range(DRAIN_UNROLL * TOP_K):
                row_copy(sl, 0, 0).wait()
            return c
        lax.fori_loop(0, m // DRAIN_UNROLL, one, 0)

    @pl.when(i == 0)
    def _():
        carry_s[...] = jnp.zeros_like(carry_s)
        carry_s[0:1, :] = cnt_in_ref[...].astype(F32)
        xpk_s[1] = jnp.zeros(xpk_s.shape[1:], U32)
        spare = (N_EXPERTS * cap + lax.broadcasted_iota(I32, (TOP_K, m), 0) * m
                 + lax.broadcasted_iota(I32, (TOP_K, m), 1))
        dv_s[...] = spare
        first = pltpu.make_async_copy(dv_s, ds_s.at[1], idx_sem)
        first.start()
        first.wait()

    issue_prev_chunk(0)
    yl = jnp.concatenate([yl_ref[:, b * w:(b + 1) * w] for b in range(bb)], axis=0).astype(BF16)
    yr = yr_ref[...].reshape(m, yr_ref.shape[-1])
    mix = jnp.dot(jnp.concatenate([yl, yr], axis=1), wo_ref[...], preferred_element_type=F32)
    x1 = _layer_norm(alpha * x_ref[...].reshape(m, d) + mix, g1_ref[...], b1_ref[...])
    x1b = x1.astype(BF16)
    issue_prev_chunk(1)

    logits = jnp.dot(x1b, rw_ref[...], preferred_element_type=F32)

    s = jax.nn.sigmoid(logits)
    sb = s + rb_ref[...]
    lane = lax.broadcasted_iota(I32, (m, LANES), 1)
    e_id = lane & (N_EXPERTS - 1)
    e_f = e_id.astype(F32)
    low = lane < N_EXPERTS
    e_low = jnp.where(low, e_f, -1.0)
    grp = e_id >> 3
    big = jnp.float32(1e9)
    ninf = jnp.float32(-jnp.inf)

    m1 = _seg_allreduce(sb, lane, jnp.maximum)
    first_max = _seg_allreduce(jnp.where(sb == m1, e_f, big), lane, jnp.minimum)
    m2 = _seg_allreduce(jnp.where(e_f == first_max, ninf, sb), lane, jnp.maximum)
    gs = m1 + m2
    hg = jnp.dot(x1b, sg_ref[...], preferred_element_type=F32)
    issue_prev_chunk(2)
    rank = jnp.zeros((m, LANES), F32)
    for dgrp in range(1, N_GROUPS):
        other = pltpu.roll(gs, 8 * dgrp, axis=1)
        tie = jnp.where(grp >= dgrp, 1.0, 0.0)
        rank = rank + jnp.where(other > gs, 1.0, jnp.where(other == gs, tie, 0.0))
    v = jnp.where(rank < TOPK_GROUPS, jnp.where(low, sb, ninf), ninf)
    hu = jnp.dot(x1b, su_ref[...], preferred_element_type=F32)
    hs = (hg * jax.nn.sigmoid(hg) * hu).astype(BF16)

    idx_cols, w_cols = [], []
    sel = jnp.zeros((m, LANES), F32)
    for rnd in range(TOP_K):
        mx = jnp.max(v, axis=1, keepdims=True)
        idx = jnp.min(jnp.where(v == mx, e_f, big), axis=1, keepdims=True)
        hit = e_low == idx
        w_cols.append(jnp.sum(jnp.where(hit, s, 0.0), axis=1, keepdims=True))
        idx_cols.append(idx)
        v = jnp.where(hit, ninf, v)
        sel = jnp.where(hit, 1.0, sel)
        if rnd == TOP_K // 2 - 1:
            base_ref[...] = alpha * x1 + jnp.dot(hs, sd_ref[...], preferred_element_type=F32)
            issue_prev_chunk(3)

    packed = _pack_bf16_pair(x1)
    for sub in range(ROW_SUBLANES):
        xpk_s[slot, :, sub, :] = packed[:, sub * LANES:(sub + 1) * LANES]

    cum = jnp.dot(tri_ref[...], sel.astype(BF16), preferred_element_type=F32) + carry_s[0:1, :]
    carry_s[0:1, :] = carry_s[0:1, :] + jnp.sum(sel, axis=0, keepdims=True)
    cnt_ref[...] = carry_s[0:1, :].astype(I32)

    wsum = w_cols[0]
    for c in w_cols[1:]:
        wsum = wsum + c
    d_out = jnp.zeros((m, LANES), F32)
    w_out = jnp.zeros((m, LANES), F32)
    for kk in range(TOP_K):
        hit = e_low == idx_cols[kk]
        pk = jnp.sum(jnp.where(hit, cum, 0.0), axis=1, keepdims=True)
        d_out = jnp.where(lane == kk, idx_cols[kk] * float(cap) + pk, d_out)
        w_out = jnp.where(lane == kk, w_cols[kk] / wsum * ROUTED_SCALE, w_out)
    wts_ref[...] = w_out[:, :TOP_K]
    dest_t = jnp.transpose(d_out)[:TOP_K].astype(I32)
    dest_ref[0] = dest_t
    dv_s[...] = dest_t
    to_smem = pltpu.make_async_copy(dv_s, ds_s.at[slot], idx_sem)
    to_smem.start()
    to_smem.wait()

    drain(1 - slot)

    @pl.when(i == n_steps - 1)
    def _():
        issue_rows(slot, 0, m)
        drain(slot)


def _mix(y_lru2, y_ret, x, w, alpha, cnt_in, xs, cap):
    B, T, D = x.shape
    assert D == 2 * ROW_SUBLANES * LANES, "a packed token row must fill exactly one (8, 128) tile"
    W = y_ret.shape[-1]
    tm = min(MIX_ROWS, T)
    bb = min(B, MIX_ROWS // tm)
    m = bb * tm
    n = B * T
    nt = T // tm
    n_steps = (B // bb) * nt
    hs = w["sg"].shape[1]
    aliased = xs is not None
    tri = (lax.broadcasted_iota(I32, (m, m), 1) < lax.broadcasted_iota(I32, (m, m), 0)).astype(BF16)
    body = functools.partial(_mix_body, bb=bb, tm=tm, alpha=alpha, cap=cap, n_steps=n_steps, aliased=aliased)
    in_specs = [pl.BlockSpec((tm, bb * W), lambda i: (i % nt, i // nt)),
                pl.BlockSpec((bb, tm, W), lambda i: (i // nt, i % nt, 0)),
                pl.BlockSpec((bb, tm, D), lambda i: (i // nt, i % nt, 0)),
                _const_spec((2 * W, D)),
                _const_spec((1, D)),
                _const_spec((1, D)),
                _const_spec((D, LANES)),
                _const_spec((1, LANES)),
                _const_spec((D, hs)),
                _const_spec((D, hs)),
                _const_spec((hs, D)),
                _const_spec((m, m)),
                _const_spec((1, LANES))]
    args = [y_lru2, y_ret, x, w["wo"], w["g1"], w["b1"], w["rw"], w["rb"], w["sg"], w["su"], w["sd"],
            tri, cnt_in]
    if aliased:
        in_specs.append(pl.BlockSpec(memory_space=pl.ANY))
        args.append(xs)
    return pl.pallas_call(
        body,
        grid=(n_steps,),
        in_specs=in_specs,
        out_specs=[pl.BlockSpec((m, D), lambda i: (i, 0)),
                   pl.BlockSpec((1, TOP_K, m), lambda i: (i, 0, 0)),
                   pl.BlockSpec((m, TOP_K), lambda i: (i, 0)),
                   pl.BlockSpec((1, LANES), lambda i: (0, 0)),
                   pl.BlockSpec(memory_space=pl.ANY)],
        out_shape=[jax.ShapeDtypeStruct((n, D), F32),
                   jax.ShapeDtypeStruct((n_steps, TOP_K, m), I32),
                   jax.ShapeDtypeStruct((n, TOP_K), F32),
                   jax.ShapeDtypeStruct((1, LANES), I32),
                   jax.ShapeDtypeStruct((N_EXPERTS * cap + MIX_ROWS * TOP_K, ROW_SUBLANES, LANES), U32)],
        scratch_shapes=[pltpu.VMEM((8, LANES), F32),
                        pltpu.VMEM((2, m, ROW_SUBLANES, LANES), U32),
                        pltpu.VMEM((TOP_K, m), I32),
                        pltpu.SMEM((2, TOP_K, m), I32),
                        pltpu.SemaphoreType.DMA((2,)),
                        pltpu.SemaphoreType.DMA(())],
        input_output_aliases={13: 4} if aliased else {},
        compiler_params=_params(1),
        name="mix_router",
    )(*args)


def _gmm_body(ge_ref, gr_ref, gn_ref, gt_ref, gx_ref, gs_ref, xs_ref, wg_ref, wu_ref, wd_ref, ys_ref,
              wgf_s, wuf_s, wdf_s, wgu_s, wd_s, x_s, y_s, w_sems, in_sems, out_sems, *, tm, n_items):
    i = pl.program_id(0)
    hid = wg_ref.shape[2]
    slot = i % 2

    def weight_copies(expert, sl):
        return [pltpu.make_async_copy(wg_ref.at[expert], wgf_s.at[sl], w_sems.at[sl]),
                pltpu.make_async_copy(wu_ref.at[expert], wuf_s.at[sl], w_sems.at[sl]),
                pltpu.make_async_copy(wd_ref.at[expert], wdf_s.at[sl], w_sems.at[sl])]

    def tile_copies(to_vmem, item, sl):
        r0 = pl.multiple_of(gr_ref[item] * tm, tm)
        out = []
        for sub in range(ROW_SUBLANES):
            cols = pl.ds(sub * LANES, LANES)
            if to_vmem:
                out.append(pltpu.make_async_copy(xs_ref.at[pl.ds(r0, tm), sub, :], x_s.at[sl, :, cols],
                                                 in_sems.at[sl]))
            else:
                out.append(pltpu.make_async_copy(y_s.at[sl, :, cols], ys_ref.at[pl.ds(r0, tm), sub, :],
                                                 out_sems.at[sl]))
        return out

    @pl.when(i == 0)
    def _():
        for cp in weight_copies(ge_ref[0], 0):
            cp.start()
        for cp in tile_copies(True, 0, 0):
            cp.start()
        y_s[...] = jnp.zeros_like(y_s)

    nxt = jnp.minimum(i + 1, n_items - 1)

    @pl.when((i + 1 < n_items) & (gn_ref[nxt] > 0))
    def _():
        for cp in tile_copies(True, nxt, 1 - slot):
            cp.start()

    e = ge_ref[i]
    e_prev = ge_ref[jnp.maximum(i - 1, 0)]

    @pl.when((i == 0) | (e != e_prev))
    def _():
        wsl = gs_ref[i]
        nxt_e = gx_ref[i]

        @pl.when(nxt_e >= 0)
        def _():
            for cp in weight_copies(nxt_e, 1 - wsl):
                cp.start()

        for cp in weight_copies(e, wsl):
            cp.wait()
        wgu_s[:, :hid] = wgf_s[wsl].astype(BF16)
        wgu_s[:, hid:] = wuf_s[wsl].astype(BF16)
        wd_s[...] = wdf_s[wsl].astype(BF16)

    n_valid = gn_ref[i]

    @pl.when(n_valid > 0)
    def _():
        for cp in tile_copies(True, i, slot):
            cp.wait()

        @pl.when(i >= 2)
        def _():
            for cp in tile_copies(False, i, slot):
                cp.wait()

        def expert_mlp(rows):
            pk = x_s[slot, :rows]
            valid = lax.broadcasted_iota(I32, pk.shape, 0) < n_valid
            xa, xb = _unpack_bf16_pair(jnp.where(valid, pk, jnp.uint32(0)))
            x = jnp.concatenate([xa.astype(BF16), xb.astype(BF16)], axis=1)
            h2 = jnp.dot(x, wgu_s[...], preferred_element_type=F32)
            hg = h2[:, :hid]
            h = (hg * jax.nn.sigmoid(hg) * h2[:, hid:]).astype(BF16)
            y_s[slot, :rows] = _pack_bf16_pair(jnp.dot(h, wd_s[...], preferred_element_type=F32))

        @pl.when(n_valid > tm // 2)
        def _():
            expert_mlp(tm)

        @pl.when((n_valid > tm // 4) & (n_valid <= tm // 2))
        def _():
            expert_mlp(tm // 2)

        @pl.when(n_valid <= tm // 4)
        def _():
            expert_mlp(tm // 4)

        for cp in tile_copies(False, i, slot):
            cp.start()

    @pl.when(i == n_items - 1)
    def _():
        total = gt_ref[0]

        @pl.when(total >= 2)
        def _():
            for cp in tile_copies(False, 0, total % 2):
                cp.wait()

        for cp in tile_copies(False, 0, (total + 1) % 2):
            cp.wait()


def _gmm_metadata(counts, cap, tm, n_items):
    e = counts.shape[0]
    tiles = (counts + tm - 1) // tm
    item_end = jnp.cumsum(tiles)
    total = item_end[-1]
    it = jnp.arange(n_items, dtype=I32)
    itc = jnp.minimum(it, total - 1)
    ge = jnp.sum((item_end[None, :] <= itc[:, None]).astype(I32), axis=1)
    onehot = ge[:, None] == jnp.arange(e, dtype=I32)[None, :]
    start = jnp.sum(jnp.where(onehot, (item_end - tiles)[None, :], 0), axis=1)
    cnt = jnp.sum(jnp.where(onehot, counts[None, :], 0), axis=1)
    j = itc - start
    gr = ge * (cap // tm) + j
    gn = jnp.where(it < total, jnp.clip(cnt - j * tm, 0, tm), 0)
    ids = jnp.arange(e, dtype=I32)
    live = tiles > 0
    later = live[None, :] & (ids[None, :] > ids[:, None])
    next_e = jnp.min(jnp.where(later, ids[None, :], e), axis=1)
    next_e = jnp.where(next_e < e, next_e, -1)
    wslot = (jnp.cumsum(live.astype(I32)) - 1) % 2
    gx = jnp.sum(jnp.where(onehot, next_e[None, :], 0), axis=1)
    gs = jnp.sum(jnp.where(onehot, wslot[None, :], 0), axis=1)
    return tuple(a.astype(I32) for a in (ge, gr, gn, total.reshape(1), gx, gs))


def _gmm(xs, counts, cap, n_tokens, wg, wu, wd):
    tm = GMM_ROWS
    e, d, hid = wg.shape
    n_items = (n_tokens * TOP_K) // tm + e
    meta = _gmm_metadata(counts, cap, tm, n_items)
    body = functools.partial(_gmm_body, tm=tm, n_items=n_items)
    grid_spec = pltpu.PrefetchScalarGridSpec(
        num_scalar_prefetch=6,
        grid=(n_items,),
        in_specs=[pl.BlockSpec(memory_space=pl.ANY)] * 4,
        out_specs=pl.BlockSpec(memory_space=pl.ANY),
        scratch_shapes=[pltpu.VMEM((2, d, hid), F32), pltpu.VMEM((2, d, hid), F32),
                        pltpu.VMEM((2, hid, d), F32),
                        pltpu.VMEM((d, 2 * hid), BF16), pltpu.VMEM((hid, d), BF16),
                        pltpu.VMEM((2, tm, d // 2), U32), pltpu.VMEM((2, tm, d // 2), U32),
                        pltpu.SemaphoreType.DMA((2,)), pltpu.SemaphoreType.DMA((2,)),
                        pltpu.SemaphoreType.DMA((2,))],
    )
    return pl.pallas_call(
        body,
        grid_spec=grid_spec,
        out_shape=jax.ShapeDtypeStruct(xs.shape, U32),
        compiler_params=_params(1),
        name="expert_gmm",
    )(*meta, xs, wg, wu, wd)


def _final_body(d0_ref, d1_ref, dn_ref, wts_ref, base_ref, p_ref, ys_ref, pg_ref, pb_ref, pp_ref, g2_ref, b2_ref,
                out_ref, rows_s, sum_s, wrep_s, sems, *, bb, tm, n_steps):
    i = pl.program_id(0)
    m = bb * tm
    d = base_ref.shape[1]
    slot = lax.rem(i, ROW_BUFFERS)
    ahead = lax.rem(i + 2, ROW_BUFFERS)

    def row_copy(sl, d_row, kk, n):
        return pltpu.make_async_copy(ys_ref.at[d_row], rows_s.at[sl, kk, n], sems.at[sl])

    def issue_token(sl, dref, n):
        for kk in range(TOP_K):
            row_copy(sl, dref[0, kk, n], kk, n).start(priority=kk % 2)

    def issue_tile(sl, dref):
        def one(n, c):
            issue_token(sl, dref, n)
            return c
        lax.fori_loop(0, m, one, 0)

    def issue_ahead(lo, hi):
        for n in range(lo, hi):
            issue_token(ahead, dn_ref, n)

    def drain(sl):
        def one(n, c):
            for _ in range(DRAIN_UNROLL * TOP_K):
                row_copy(sl, 0, 0, 0).wait()
            return c
        lax.fori_loop(0, m // DRAIN_UNROLL, one, 0)

    @pl.when(i == 0)
    def _():
        issue_tile(0, d0_ref)
        issue_tile(1, d1_ref)

    drain(slot)

    wts = wts_ref[...]
    for kk in range(TOP_K):
        wk = jnp.broadcast_to(wts[:, kk:kk + 1], (m, LANES))
        wrep_s[kk] = _pack_bf16_pair(jnp.concatenate([wk, wk], axis=1))

    def combine(n, c):
        acc = jnp.zeros((2 * ROW_SUBLANES, LANES), BF16)
        for kk in range(TOP_K):
            row = pltpu.bitcast(rows_s[slot, kk, n], BF16)
            wk = jnp.broadcast_to(wrep_s[kk, pl.ds(n, 1), :], (ROW_SUBLANES, LANES))
            acc = acc + pltpu.bitcast(wk, BF16) * row
        sum_s[n] = pltpu.bitcast(acc, U32)
        return c

    lax.fori_loop(0, m, combine, 0, unroll=2)
    issue_ahead(0, m // 2)
    routed = jnp.concatenate(_unpack_bf16_pair(
        jnp.concatenate([sum_s[:, sub, :] for sub in range(ROW_SUBLANES)], axis=1)), axis=1)
    x2 = _layer_norm(base_ref[...] + routed, g2_ref[...], b2_ref[...])
    issue_ahead(m // 2, m)
    gate = jax.nn.sigmoid(jnp.dot(x2.astype(BF16), pg_ref[...], preferred_element_type=F32) + pb_ref[...])
    proj = jnp.dot(p_ref[...].reshape(m, p_ref.shape[-1]).astype(BF16), pp_ref[...], preferred_element_type=F32)
    out_ref[...] = (x2 + gate * proj).reshape(bb, tm, d)

    @pl.when(i == n_steps - 1)
    def _():
        drain(lax.rem(i + 1, ROW_BUFFERS))
        drain(ahead)


def _final(base, dest, wts, p, ys, w, B, T):
    n, D = base.shape
    tm = min(MIX_ROWS, T)
    bb = min(B, MIX_ROWS // tm)
    m = bb * tm
    nt = T // tm
    n_steps = (B // bb) * nt
    pd = p.shape[-1]
    body = functools.partial(_final_body, bb=bb, tm=tm, n_steps=n_steps)
    return pl.pallas_call(
        body,
        grid=(n_steps,),
        in_specs=[pl.BlockSpec((1, TOP_K, m), lambda i: (0, 0, 0), memory_space=pltpu.SMEM),
                  pl.BlockSpec((1, TOP_K, m), lambda i: (min(1, n_steps - 1), 0, 0), memory_space=pltpu.SMEM),
                  pl.BlockSpec((1, TOP_K, m), lambda i: (jnp.minimum(i + 2, n_steps - 1), 0, 0),
                               memory_space=pltpu.SMEM),
                  pl.BlockSpec((m, TOP_K), lambda i: (i, 0)),
                  pl.BlockSpec((m, D), lambda i: (i, 0)),
                  pl.BlockSpec((bb, tm, pd), lambda i: (i // nt, i % nt, 0)),
                  pl.BlockSpec(memory_space=pl.ANY),
                  _const_spec((D, D)),
                  _const_spec((1, D)),
                  _const_spec((pd, D)),
                  _const_spec((1, D)),
                  _const_spec((1, D))],
        out_specs=pl.BlockSpec((bb, tm, D), lambda i: (i // nt, i % nt, 0)),
        out_shape=jax.ShapeDtypeStruct((B, T, D), F32),
        scratch_shapes=[pltpu.VMEM((ROW_BUFFERS, TOP_K, m, ROW_SUBLANES, LANES), U32),
                        pltpu.VMEM((m, ROW_SUBLANES, LANES), U32),
                        pltpu.VMEM((TOP_K, m, LANES), U32),
                        pltpu.SemaphoreType.DMA((ROW_BUFFERS,))],
        compiler_params=_params(1),
        name="combine_final",
    )(dest, dest, dest, wts, base, p, ys, w["pg"], w["pb"], w["pp"], w["g2"], w["b2"])


def _prep_layer(prm):
    (w_in, conv_w, conv_b, lru_wa, lru_ba, lru_wx, lru_bx, lru_lambda, ret_gn, w_out, ln1_g, ln1_b,
     router_w, router_b, e_gate, e_up, e_down, s_gate, s_up, s_down, ln2_g, ln2_b,
     ple_w_proj, ple_w_gate, ple_b_gate) = prm
    row = lambda v: v.reshape(1, -1)
    rw2 = jnp.concatenate([router_w, router_w], axis=1).astype(BF16)
    return dict(
        w_in=w_in.astype(BF16), conv_w=conv_w, conv_b=row(conv_b), wa=lru_wa.astype(BF16), ba=row(lru_ba),
        wx=lru_wx.astype(BF16), bx=row(lru_bx), lam=row(lru_lambda), gn=row(ret_gn), wo=w_out.astype(BF16),
        g1=row(ln1_g), b1=row(ln1_b), rw=rw2, rb=row(jnp.concatenate([router_b, router_b])),
        e_gate=e_gate, e_up=e_up, e_down=e_down, sg=s_gate.astype(BF16), su=s_up.astype(BF16),
        sd=s_down.astype(BF16), g2=row(ln2_g), b2=row(ln2_b), pp=ple_w_proj.astype(BF16),
        pg=ple_w_gate.astype(BF16), pb=row(ple_b_gate))


def _mixers(x, conv_st, lru_st, ret_st, pos0, chunk, w):
    B, T, D = x.shape
    W = conv_st.shape[-1]
    z_lru, z_ret = _inproj(x, w["w_in"], 2 * W)
    y_ret, new_ret = _retention(z_ret, ret_st, w["gn"], pos0, chunk)
    y_lru, conv_tm, new_lru = _lru(z_lru, jnp.transpose(conv_st, (1, 0, 2)), lru_st,
                                   w["conv_w"], w["conv_b"], w["wa"], w["ba"], w["wx"], w["bx"], w["lam"],
                                   after=new_ret)
    return y_lru.reshape(T, B * W), y_ret, jnp.transpose(conv_tm, (1, 0, 2)), new_lru, new_ret


def kernel(x_prompt, x_sample, p_prompt, p_sample, state_conv, state_lru, state_ret, w_in, conv_w, conv_b,
           lru_wa, lru_ba, lru_wx, lru_bx, lru_lambda, ret_gn, w_out, ln1_g, ln1_b, router_w, router_b,
           exp_w_gate, exp_w_up, exp_w_down, sh_w_gate, sh_w_up, sh_w_down, ln2_g, ln2_b,
           ple_w_proj, ple_w_gate, ple_b_gate):
    depth = w_in.shape[0]
    alpha = (2 * depth) ** 0.25
    b_p, t_p, _ = x_prompt.shape
    b_s, t_s, _ = x_sample.shape
    W = state_conv.shape[-1]
    H, dh = state_ret.shape[2], state_ret.shape[3]
    n_tokens = b_p * t_p + b_s * t_s
    cap = -(-n_tokens // GMM_ROWS) * GMM_ROWS
    hp, hs = x_prompt, x_sample
    outs = [[] for _ in range(6)]
    for i in range(depth):
        prm = (w_in[i], conv_w[i], conv_b[i], lru_wa[i], lru_ba[i], lru_wx[i], lru_bx[i], lru_lambda[i],
               ret_gn[i], w_out[i], ln1_g[i], ln1_b[i], router_w[i], router_b[i], exp_w_gate[i], exp_w_up[i],
               exp_w_down[i], sh_w_gate[i], sh_w_up[i], sh_w_down[i], ln2_g[i], ln2_b[i],
               ple_w_proj[i], ple_w_gate[i], ple_b_gate[i])
        w = _prep_layer(prm)
        zc = jnp.zeros((b_p, CONV_WIDTH - 1, W), x_prompt.dtype)
        zl = jnp.zeros((b_p, W), F32)
        zr = jnp.zeros((b_p, H, dh, dh), F32)
        yl_p, yr_p, c_p, l_p, r_p = _mixers(hp, zc, zl, zr, 0, CHUNK, w)
        yl_s, yr_s, c_s, l_s, r_s = _mixers(hs, state_conv[i], state_lru[i], state_ret[i], PAST_LEN, t_s, w)
        for o, val in zip(outs, (c_p, l_p, r_p, c_s, l_s, r_s)):
            o.append(val)
        base_p, dest_p, wts_p, cnt_p, xs = _mix(yl_p, yr_p, hp, w, alpha, jnp.zeros((1, LANES), I32), None, cap)
        base_s, dest_s, wts_s, cnt_all, xs = _mix(yl_s, yr_s, hs, w, alpha, cnt_p, xs, cap)
        ys = _gmm(xs, cnt_all[0, :N_EXPERTS], cap, n_tokens, w["e_gate"], w["e_up"], w["e_down"])
        hp = _final(base_p, dest_p, wts_p, p_prompt[i], ys, w, b_p, t_p)
        hs = _final(base_s, dest_s, wts_s, p_sample[i], ys, w, b_s, t_s)
    return (hp, hs) + tuple(jnp.stack(o) for o in outs)
```

```python
import functools

import jax
import jax.numpy as jnp
from jax import lax
from jax.experimental import pallas as pl
from jax.experimental.pallas import tpu as pltpu

F32 = jnp.float32
BF16 = jnp.bfloat16
U32 = jnp.uint32
I32 = jnp.int32

CHUNK = 64
PAST_LEN = 1024
CONV_WIDTH = 4
LRU_C = 8.0
RET_HEADS = 8
ROPE_BASE = 10000.0
N_EXPERTS = 64
TOP_K = 8
N_GROUPS = 8
TOPK_GROUPS = 4
ROUTED_SCALE = 2.5
LN_EPS = 1e-5
GN_EPS = 1e-6

LANES = 128
ROW_SUBLANES = 8
DRAIN_UNROLL = 8
ISSUE_CHUNKS = 4
ROW_BUFFERS = 3
ROW_COPY_PRIORITY = 1
ROWS_PER_STEP = 512
INPROJ_COLS = 512
MIX_ROWS = 256
GMM_ROWS = 512
RET_GROUP_ROWS = 256
LRU_TIME_TILE = 64
LRU_GATE_ROWS = 128
RET_ALL_HEADS_ELEMS = 64 * 1024
VMEM_LIMIT = 56 * 1024 * 1024


def _const_spec(shape):
    zeros = (0,) * len(shape)
    return pl.BlockSpec(shape, lambda *_: zeros, pipeline_mode=pl.Buffered(1))


def _params(n_axes):
    return pltpu.CompilerParams(dimension_semantics=("arbitrary",) * n_axes,
                                vmem_limit_bytes=VMEM_LIMIT)


def _layer_norm(x, g, b):
    mu = jnp.mean(x, axis=-1, keepdims=True)
    xc = x - mu
    var = jnp.mean(xc * xc, axis=-1, keepdims=True)
    return xc * lax.rsqrt(var + LN_EPS) * g + b


def _pack_bf16_pair(x):
    c = x.shape[1] // 2
    xb = x.astype(BF16).astype(F32)
    hi = pltpu.bitcast(xb[:, :c], U32)
    lo = pltpu.bitcast(xb[:, c:], U32)
    return hi | (lo >> 16)


def _unpack_bf16_pair(pk):
    hi = pltpu.bitcast(pk & jnp.uint32(0xFFFF0000), F32)
    lo = pltpu.bitcast(pk << 16, F32)
    return hi, lo


def _inproj_body(x_ref, w_ref, zl_ref, zr_ref, *, bb, tm, lru_cols, tn):
    d = x_ref.shape[-1]
    x = x_ref[...].reshape(bb * tm, d).astype(BF16)
    for j in range(w_ref.shape[1] // tn):
        c0 = j * tn
        acc = jnp.dot(x, w_ref[:, c0:c0 + tn], preferred_element_type=F32)
        for b in range(bb):
            rows = acc[b * tm:(b + 1) * tm]
            if c0 < lru_cols:
                zl_ref[:, b * lru_cols + c0:b * lru_cols + c0 + tn] = rows
            else:
                zr_ref[b, :, c0 - lru_cols:c0 - lru_cols + tn] = rows.astype(BF16)


def _inproj_cols_body(x_ref, w_ref, zl_ref, zr_ref, xb_s, *, n_lru):
    j = pl.program_id(0)
    bb, tm, d = x_ref.shape

    @pl.when(j == 0)
    def _():
        xb_s[...] = x_ref[...].reshape(bb * tm, d).astype(BF16)

    acc = jnp.dot(xb_s[...], w_ref[...], preferred_element_type=F32)

    @pl.when(j < n_lru)
    def _():
        for b in range(bb):
            zl_ref[:, b, :] = acc[b * tm:(b + 1) * tm]

    @pl.when(j >= n_lru)
    def _():
        zr_ref[...] = acc.reshape(zr_ref.shape).astype(BF16)


def _inproj_cols(x, w_in_b, lru_cols):
    B, T, D = x.shape
    n_cols = w_in_b.shape[1]
    tn = INPROJ_COLS
    n_lru = lru_cols // tn
    body = functools.partial(_inproj_cols_body, n_lru=n_lru)
    return pl.pallas_call(
        body,
        grid=(n_cols // tn,),
        in_specs=[_const_spec((B, T, D)),
                  pl.BlockSpec((D, tn), lambda j: (0, j))],
        out_specs=[pl.BlockSpec((T, B, tn), lambda j: (0, 0, jnp.minimum(j, n_lru - 1))),
                   pl.BlockSpec((B, T, tn), lambda j: (0, 0, jnp.maximum(j - n_lru, 0)))],
        out_shape=[jax.ShapeDtypeStruct((T, B, lru_cols), F32),
                   jax.ShapeDtypeStruct((B, T, n_cols - lru_cols), BF16)],
        scratch_shapes=[pltpu.VMEM((B * T, D), BF16)],
        compiler_params=_params(1),
        name="inproj",
    )(x, w_in_b)


def _inproj(x, w_in_b, lru_cols):
    B, T, D = x.shape
    if B * T <= ROWS_PER_STEP:
        return _inproj_cols(x, w_in_b, lru_cols)
    z_lru, z_ret = _inproj_rows(x, w_in_b, lru_cols)
    return z_lru.reshape(T, B, lru_cols), z_ret


def _inproj_rows(x, w_in_b, lru_cols):
    B, T, D = x.shape
    n_cols = w_in_b.shape[1]
    ret_cols = n_cols - lru_cols
    tm = min(ROWS_PER_STEP, T)
    bb = min(B, ROWS_PER_STEP // tm)
    body = functools.partial(_inproj_body, bb=bb, tm=tm, lru_cols=lru_cols, tn=INPROJ_COLS)
    return pl.pallas_call(
        body,
        grid=(B // bb, T // tm),
        in_specs=[pl.BlockSpec((bb, tm, D), lambda b, t: (b, t, 0)),
                  _const_spec((D, n_cols))],
        out_specs=[pl.BlockSpec((tm, bb * lru_cols), lambda b, t: (t, b)),
                   pl.BlockSpec((bb, tm, ret_cols), lambda b, t: (b, t, 0))],
        out_shape=[jax.ShapeDtypeStruct((T, B * lru_cols), F32),
                   jax.ShapeDtypeStruct((B, T, ret_cols), BF16)],
        compiler_params=_params(2),
        name="inproj",
    )(x, w_in_b)


def _lru_body(xl_ref, gl_ref, conv0_ref, h0_ref, cw_ref, cb_ref, wa_ref, ba_ref, wx_ref, bx_ref, lam_ref,
              after_ref, y_ref, conv_out_ref, h_out_ref, xp_s, a_s, b_s, h_s, *, tt, rows):
    del after_ref
    i = pl.program_id(0)
    B, W = h0_ref.shape
    nblk = wa_ref.shape[0]
    blk = W // nblk

    @pl.when(i == 0)
    def _():
        xp_s[0:CONV_WIDTH - 1] = conv0_ref[...]
        h_s[...] = h0_ref[...]

    xp_s[CONV_WIDTH - 1:] = xl_ref[...]

    lam = lam_ref[...]
    neg = -lam
    softplus = jnp.maximum(neg, 0.0) + jnp.log1p(jnp.exp(-jnp.abs(neg)))
    decay = (-LRU_C) * softplus

    def gates(c, carry):
        t0 = pl.multiple_of(c * rows, rows)
        xc = cb_ref[...].reshape(1, 1, W)
        for j in range(CONV_WIDTH):
            xc = xc + xp_s[pl.ds(t0 + j, rows)] * cw_ref[j:j + 1].reshape(1, 1, W)
        xc2 = xc.reshape(rows * B, W)
        xcb = xc2.astype(BF16)
        r_parts, i_parts = [], []
        for n in range(nblk):
            xb = xcb[:, n * blk:(n + 1) * blk]
            r_parts.append(jnp.dot(xb, wa_ref[n], preferred_element_type=F32))
            i_parts.append(jnp.dot(xb, wx_ref[n], preferred_element_type=F32))
        r = jax.nn.sigmoid(jnp.concatenate(r_parts, axis=1) + ba_ref[...])
        ig = jax.nn.sigmoid(jnp.concatenate(i_parts, axis=1) + bx_ref[...])
        a = jnp.exp(decay * r)
        bterm = jnp.sqrt(1.0 - a * a) * (ig * xc2)
        a_s[pl.ds(t0, rows)] = a.reshape(rows, B, W)
        b_s[pl.ds(t0, rows)] = bterm.reshape(rows, B, W)
        return carry

    lax.fori_loop(0, tt // rows, gates, 0)

    def step(t, h):
        hn = a_s[t] * h + b_s[t]
        y_ref[t] = hn * jax.nn.gelu(gl_ref[t])
        return hn

    h_last = lax.fori_loop(0, tt, step, h_s[...], unroll=8)
    h_s[...] = h_last
    tail = xp_s[tt:tt + CONV_WIDTH - 1]
    xp_s[0:CONV_WIDTH - 1] = tail
    conv_out_ref[...] = tail
    h_out_ref[...] = h_last


def _lru(z_lru3, conv0_tm, h0, conv_w, conv_b, wa_b, ba, wx_b, bx, lam, after):
    T, B, W2 = z_lru3.shape
    W = W2 // 2
    tt = min(LRU_TIME_TILE, T)
    rows = max(1, min(tt, LRU_GATE_ROWS // B))
    body = functools.partial(_lru_body, tt=tt, rows=rows)
    nb = wa_b.shape[0]
    blk = W // nb
    return pl.pallas_call(
        body,
        grid=(T // tt,),
        in_specs=[pl.BlockSpec((tt, B, W), lambda t: (t, 0, 0)),
                  pl.BlockSpec((tt, B, W), lambda t: (t, 0, 1)),
                  _const_spec((CONV_WIDTH - 1, B, W)),
                  _const_spec((B, W)),
                  _const_spec((CONV_WIDTH, W)),
                  _const_spec((1, W)),
                  _const_spec((nb, blk, blk)),
                  _const_spec((1, W)),
                  _const_spec((nb, blk, blk)),
                  _const_spec((1, W)),
                  _const_spec((1, W)),
                  pl.BlockSpec(memory_space=pl.ANY)],
        out_specs=[pl.BlockSpec((tt, B, W), lambda t: (t, 0, 0)),
                   pl.BlockSpec((CONV_WIDTH - 1, B, W), lambda t: (0, 0, 0)),
                   pl.BlockSpec((B, W), lambda t: (0, 0))],
        out_shape=[jax.ShapeDtypeStruct((T, B, W), F32),
                   jax.ShapeDtypeStruct((CONV_WIDTH - 1, B, W), F32),
                   jax.ShapeDtypeStruct((B, W), F32)],
        scratch_shapes=[pltpu.VMEM((tt + CONV_WIDTH - 1, B, W), F32),
                        pltpu.VMEM((tt, B, W), F32),
                        pltpu.VMEM((tt, B, W), F32),
                        pltpu.VMEM((B, W), F32)],
        compiler_params=_params(1),
        name="rglru",
    )(z_lru3, z_lru3, conv0_tm, h0, conv_w, conv_b, wa_b, ba, wx_b, bx, lam, after)


def _ret_body(q_ref, k_ref, v_ref, g_ref, cos_ref, sin_ref, mask_ref, qdec_ref, kdec_ref, cdec_ref, gn_ref,
              s0_ref, y_ref, s_out_ref, *, rg, hb, dh):
    T = q_ref.shape[1]
    scale = dh ** -0.5

    def rope(t, cos, sin):
        return t * cos + pltpu.roll(t, dh // 2, axis=1) * sin

    for hh in range(hb):
        cols = slice(hh * dh, (hh + 1) * dh)

        def group(c, s, hh=hh, cols=cols):
            r0 = pl.multiple_of(c * rg, rg)
            rws = pl.ds(r0, rg)
            cos = cos_ref[rws, :]
            sin = sin_ref[rws, :]
            q = rope(q_ref[0, rws, cols].astype(F32), cos, sin)
            k = rope(k_ref[0, rws, cols].astype(F32), cos, sin) * scale
            v = v_ref[0, rws, cols]
            scores = lax.dot_general(q.astype(BF16), k.astype(BF16), (((1,), (1,)), ((), ())),
                                     preferred_element_type=F32)
            scores = scores * mask_ref[hh]
            o = jnp.dot(scores.astype(BF16), v, preferred_element_type=F32)
            o = o + jnp.dot((q * qdec_ref[hh]).astype(BF16), s.astype(BF16), preferred_element_type=F32)
            kd = (k * kdec_ref[hh]).astype(BF16)
            kv = lax.dot_general(kd, v, (((0,), (0,)), ((), ())), preferred_element_type=F32)
            s_new = cdec_ref[hh] * s + kv
            mu = jnp.mean(o, axis=-1, keepdims=True)
            oc = o - mu
            var = jnp.mean(oc * oc, axis=-1, keepdims=True)
            on = oc * lax.rsqrt(var + GN_EPS) * gn_ref[:, cols]
            g = g_ref[0, rws, cols].astype(F32)
            y_ref[0, rws, cols] = (g * jax.nn.sigmoid(g) * on).astype(y_ref.dtype)
            return s_new

        n_groups = T // rg
        s_out_ref[0, hh] = lax.fori_loop(0, n_groups, group, s0_ref[0, hh],
                                         unroll=4 if n_groups % 4 == 0 else 1)


def _retention_tables(T, pos0, chunk, rg, dh):
    half = dh // 2
    inv = ROPE_BASE ** (-jnp.arange(half, dtype=F32) / half)
    pos = pos0 + jnp.arange(T)
    ang = pos.astype(F32)[:, None] * inv[None, :]
    cos, sin = jnp.cos(ang), jnp.sin(ang)
    cos2 = jnp.concatenate([cos, cos], axis=1)
    sin2 = jnp.concatenate([-sin, sin], axis=1)
    log_g = jnp.log1p(-jnp.exp2(-5.0 - jnp.arange(RET_HEADS, dtype=F32)))[:, None, None]
    idx = jnp.arange(rg, dtype=F32)
    ci = jnp.floor(idx / chunk)
    diff = idx[:, None] - idx[None, :]
    same = ci[:, None] == ci[None, :]
    earlier = ci[None, :] < ci[:, None]
    dist = jnp.where(same, jnp.abs(diff), diff)
    mask = jnp.where(same | earlier, jnp.exp(dist[None] * log_g), 0.0)
    ones = jnp.ones((1, 1, dh), F32)
    qdec = jnp.exp((idx + 1.0)[None, :, None] * log_g) * ones
    kdec = jnp.exp((rg - 1.0 - idx)[None, :, None] * log_g) * ones
    cdec = jnp.exp(rg * log_g) * ones
    return cos2, sin2, mask, qdec, kdec, cdec


def _retention(z_ret, s0, gn, pos0, chunk):
    B, T, C4 = z_ret.shape
    H = RET_HEADS
    dh = C4 // (4 * H)
    rg = min(T, max(chunk, (RET_GROUP_ROWS // chunk) * chunk))
    hb = H if T * H * dh <= RET_ALL_HEADS_ELEMS else 1
    nh = H // hb
    cos2, sin2, mask, qdec, kdec, cdec = _retention_tables(T, pos0, chunk, rg, dh)
    body = functools.partial(_ret_body, rg=rg, hb=hb, dh=dh)
    col = lambda off: (lambda b, h: (b, 0, off * nh + h))
    return pl.pallas_call(
        body,
        grid=(B, nh),
        in_specs=[pl.BlockSpec((1, T, hb * dh), col(0)),
                  pl.BlockSpec((1, T, hb * dh), col(1)),
                  pl.BlockSpec((1, T, hb * dh), col(2)),
                  pl.BlockSpec((1, T, hb * dh), col(3)),
                  _const_spec((T, dh)),
                  _const_spec((T, dh)),
                  pl.BlockSpec((hb, rg, rg), lambda b, h: (h, 0, 0)),
                  pl.BlockSpec((hb, rg, dh), lambda b, h: (h, 0, 0)),
                  pl.BlockSpec((hb, rg, dh), lambda b, h: (h, 0, 0)),
                  pl.BlockSpec((hb, 1, dh), lambda b, h: (h, 0, 0)),
                  pl.BlockSpec((1, hb * dh), lambda b, h: (0, h)),
                  pl.BlockSpec((1, hb, dh, dh), lambda b, h: (b, h, 0, 0))],
        out_specs=[pl.BlockSpec((1, T, hb * dh), lambda b, h: (b, 0, h)),
                   pl.BlockSpec((1, hb, dh, dh), lambda b, h: (b, h, 0, 0))],
        out_shape=[jax.ShapeDtypeStruct((B, T, H * dh), BF16),
                   jax.ShapeDtypeStruct((B, H, dh, dh), F32)],
        compiler_params=_params(2),
        name="retention",
    )(z_ret, z_ret, z_ret, z_ret, cos2, sin2, mask, qdec, kdec, cdec, gn, s0)


def _seg_allreduce(v, lane, op):
    for s in (1, 2, 4):
        up = pltpu.roll(v, LANES - s, axis=1)
        dn = pltpu.roll(v, s, axis=1)
        v = op(v, jnp.where((lane & s) == 0, up, dn))
    return v


def _mix_body(*refs, bb, tm, alpha, cap, n_steps, aliased):
    (yl_ref, yr_ref, x_ref, wo_ref, g1_ref, b1_ref, rw_ref, rb_ref, sg_ref, su_ref, sd_ref,
     tri_ref, cnt_in_ref) = refs[:13]
    refs = refs[14:] if aliased else refs[13:]
    base_ref, dest_ref, wts_ref, cnt_ref, xs_ref, carry_s, xpk_s, dv_s, ds_s, row_sems, idx_sem = refs
    i = pl.program_id(0)
    m = bb * tm
    d = x_ref.shape[-1]
    w = yl_ref.shape[1] // bb
    slot = i % 2
    chunk = m // ISSUE_CHUNKS

    def row_copy(sl, n, dst):
        return pltpu.make_async_copy(xpk_s.at[sl, n], xs_ref.at[dst], row_sems.at[sl])

    def issue_rows(sl, lo, hi):
        def one(n, c):
            for kk in range(TOP_K):
                row_copy(sl, n, ds_s[sl, kk, n]).start(priority=ROW_COPY_PRIORITY)
            return c
        lax.fori_loop(lo, hi, one, 0)

    def issue_prev_chunk(c):
        for n in range(c * chunk, (c + 1) * chunk):
            for kk in range(TOP_K):
                row_copy(1 - slot, n, ds_s[1 - slot, kk, n]).start(priority=ROW_COPY_PRIORITY)

    def drain(sl):
        def one(n, c):
            for _ in range(DRAIN_UNROLL * TOP_K):
                row_copy(sl, 0, 0).wait()
            return c
        lax.fori_loop(0, m // DRAIN_UNROLL, one, 0)

    @pl.when(i == 0)
    def _():
        carry_s[...] = jnp.zeros_like(carry_s)
        carry_s[0:1, :] = cnt_in_ref[...].astype(F32)
        xpk_s[1] = jnp.zeros(xpk_s.shape[1:], U32)
        spare = (N_EXPERTS * cap + lax.broadcasted_iota(I32, (TOP_K, m), 0) * m
                 + lax.broadcasted_iota(I32, (TOP_K, m), 1))
        dv_s[...] = spare
        first = pltpu.make_async_copy(dv_s, ds_s.at[1], idx_sem)
        first.start()
        first.wait()

    issue_prev_chunk(0)
    yl = jnp.concatenate([yl_ref[:, b * w:(b + 1) * w] for b in range(bb)], axis=0).astype(BF16)
    yr = yr_ref[...].reshape(m, yr_ref.shape[-1])
    mix = jnp.dot(jnp.concatenate([yl, yr], axis=1), wo_ref[...], preferred_element_type=F32)
    x1 = _layer_norm(alpha * x_ref[...].reshape(m, d) + mix, g1_ref[...], b1_ref[...])
    x1b = x1.astype(BF16)
    issue_prev_chunk(1)

    logits = jnp.dot(x1b, rw_ref[...], preferred_element_type=F32)

    s = jax.nn.sigmoid(logits)
    sb = s + rb_ref[...]
    lane = lax.broadcasted_iota(I32, (m, LANES), 1)
    e_id = lane & (N_EXPERTS - 1)
    e_f = e_id.astype(F32)
    low = lane < N_EXPERTS
    e_low = jnp.where(low, e_f, -1.0)
    grp = e_id >> 3
    big = jnp.float32(1e9)
    ninf = jnp.float32(-jnp.inf)

    m1 = _seg_allreduce(sb, lane, jnp.maximum)
    first_max = _seg_allreduce(jnp.where(sb == m1, e_f, big), lane, jnp.minimum)
    m2 = _seg_allreduce(jnp.where(e_f == first_max, ninf, sb), lane, jnp.maximum)
    gs = m1 + m2
    hg = jnp.dot(x1b, sg_ref[...], preferred_element_type=F32)
    issue_prev_chunk(2)
    rank = jnp.zeros((m, LANES), F32)
    for dgrp in range(1, N_GROUPS):
        other = pltpu.roll(gs, 8 * dgrp, axis=1)
        tie = jnp.where(grp >= dgrp, 1.0, 0.0)
        rank = rank + jnp.where(other > gs, 1.0, jnp.where(other == gs, tie, 0.0))
    v = jnp.where(rank < TOPK_GROUPS, jnp.where(low, sb, ninf), ninf)
    hu = jnp.dot(x1b, su_ref[...], preferred_element_type=F32)
    hs = (hg * jax.nn.sigmoid(hg) * hu).astype(BF16)

    idx_cols, w_cols = [], []
    sel = jnp.zeros((m, LANES), F32)
    for rnd in range(TOP_K):
        mx = jnp.max(v, axis=1, keepdims=True)
        idx = jnp.min(jnp.where(v == mx, e_f, big), axis=1, keepdims=True)
        hit = e_low == idx
        w_cols.append(jnp.sum(jnp.where(hit, s, 0.0), axis=1, keepdims=True))
        idx_cols.append(idx)
        v = jnp.where(hit, ninf, v)
        sel = jnp.where(hit, 1.0, sel)
        if rnd == TOP_K // 2 - 1:
            base_ref[...] = alpha * x1 + jnp.dot(hs, sd_ref[...], preferred_element_type=F32)
            issue_prev_chunk(3)

    packed = _pack_bf16_pair(x1)
    for sub in range(ROW_SUBLANES):
        xpk_s[slot, :, sub, :] = packed[:, sub * LANES:(sub + 1) * LANES]

    cum = jnp.dot(tri_ref[...], sel.astype(BF16), preferred_element_type=F32) + carry_s[0:1, :]
    carry_s[0:1, :] = carry_s[0:1, :] + jnp.sum(sel, axis=0, keepdims=True)
    cnt_ref[...] = carry_s[0:1, :].astype(I32)

    wsum = w_cols[0]
    for c in w_cols[1:]:
        wsum = wsum + c
    d_out = jnp.zeros((m, LANES), F32)
    w_out = jnp.zeros((m, LANES), F32)
    for kk in range(TOP_K):
        hit = e_low == idx_cols[kk]
        pk = jnp.sum(jnp.where(hit, cum, 0.0), axis=1, keepdims=True)
        d_out = jnp.where(lane == kk, idx_cols[kk] * float(cap) + pk, d_out)
        w_out = jnp.where(lane == kk, w_cols[kk] / wsum * ROUTED_SCALE, w_out)
    wts_ref[...] = w_out[:, :TOP_K]
    dest_t = jnp.transpose(d_out)[:TOP_K].astype(I32)
    dest_ref[0] = dest_t
    dv_s[...] = dest_t
    to_smem = pltpu.make_async_copy(dv_s, ds_s.at[slot], idx_sem)
    to_smem.start()
    to_smem.wait()

    drain(1 - slot)

    @pl.when(i == n_steps - 1)
    def _():
        issue_rows(slot, 0, m)
        drain(slot)


def _mix(y_lru2, y_ret, x, w, alpha, cnt_in, xs, cap):
    B, T, D = x.shape
    assert D == 2 * ROW_SUBLANES * LANES, "a packed token row must fill exactly one (8, 128) tile"
    W = y_ret.shape[-1]
    tm = min(MIX_ROWS, T)
    bb = min(B, MIX_ROWS // tm)
    m = bb * tm
    n = B * T
    nt = T // tm
    n_steps = (B // bb) * nt
    hs = w["sg"].shape[1]
    aliased = xs is not None
    tri = (lax.broadcasted_iota(I32, (m, m), 1) < lax.broadcasted_iota(I32, (m, m), 0)).astype(BF16)
    body = functools.partial(_mix_body, bb=bb, tm=tm, alpha=alpha, cap=cap, n_steps=n_steps, aliased=aliased)
    in_specs = [pl.BlockSpec((tm, bb * W), lambda i: (i % nt, i // nt)),
                pl.BlockSpec((bb, tm, W), lambda i: (i // nt, i % nt, 0)),
                pl.BlockSpec((bb, tm, D), lambda i: (i // nt, i % nt, 0)),
                _const_spec((2 * W, D)),
                _const_spec((1, D)),
                _const_spec((1, D)),
                _const_spec((D, LANES)),
                _const_spec((1, LANES)),
                _const_spec((D, hs)),
                _const_spec((D, hs)),
                _const_spec((hs, D)),
                _const_spec((m, m)),
                _const_spec((1, LANES))]
    args = [y_lru2, y_ret, x, w["wo"], w["g1"], w["b1"], w["rw"], w["rb"], w["sg"], w["su"], w["sd"],
            tri, cnt_in]
    if aliased:
        in_specs.append(pl.BlockSpec(memory_space=pl.ANY))
        args.append(xs)
    return pl.pallas_call(
        body,
        grid=(n_steps,),
        in_specs=in_specs,
        out_specs=[pl.BlockSpec((m, D), lambda i: (i, 0)),
                   pl.BlockSpec((1, TOP_K, m), lambda i: (i, 0, 0)),
                   pl.BlockSpec((m, TOP_K), lambda i: (i, 0)),
                   pl.BlockSpec((1, LANES), lambda i: (0, 0)),
                   pl.BlockSpec(memory_space=pl.ANY)],
        out_shape=[jax.ShapeDtypeStruct((n, D), F32),
                   jax.ShapeDtypeStruct((n_steps, TOP_K, m), I32),
                   jax.ShapeDtypeStruct((n, TOP_K), F32),
                   jax.ShapeDtypeStruct((1, LANES), I32),
                   jax.ShapeDtypeStruct((N_EXPERTS * cap + MIX_ROWS * TOP_K, ROW_SUBLANES, LANES), U32)],
        scratch_shapes=[pltpu.VMEM((8, LANES), F32),
                        pltpu.VMEM((2, m, ROW_SUBLANES, LANES), U32),
                        pltpu.VMEM((TOP_K, m), I32),
                        pltpu.SMEM((2, TOP_K, m), I32),
                        pltpu.SemaphoreType.DMA((2,)),
                        pltpu.SemaphoreType.DMA(())],
        input_output_aliases={13: 4} if aliased else {},
        compiler_params=_params(1),
        name="mix_router",
    )(*args)


def _gmm_body(ge_ref, gr_ref, gn_ref, gt_ref, gx_ref, gs_ref, xs_ref, wg_ref, wu_ref, wd_ref, ys_ref,
              wgf_s, wuf_s, wdf_s, wgu_s, wd_s, x_s, y_s, w_sems, in_sems, out_sems, *, tm, n_items):
    i = pl.program_id(0)
    hid = wg_ref.shape[2]
    slot = i % 2

    def weight_copies(expert, sl):
        return [pltpu.make_async_copy(wg_ref.at[expert], wgf_s.at[sl], w_sems.at[sl]),
                pltpu.make_async_copy(wu_ref.at[expert], wuf_s.at[sl], w_sems.at[sl]),
                pltpu.make_async_copy(wd_ref.at[expert], wdf_s.at[sl], w_sems.at[sl])]

    def tile_copies(to_vmem, item, sl):
        r0 = pl.multiple_of(gr_ref[item] * tm, tm)
        out = []
        for sub in range(ROW_SUBLANES):
            cols = pl.ds(sub * LANES, LANES)
            if to_vmem:
                out.append(pltpu.make_async_copy(xs_ref.at[pl.ds(r0, tm), sub, :], x_s.at[sl, :, cols],
                                                 in_sems.at[sl]))
            else:
                out.append(pltpu.make_async_copy(y_s.at[sl, :, cols], ys_ref.at[pl.ds(r0, tm), sub, :],
                                                 out_sems.at[sl]))
        return out

    @pl.when(i == 0)
    def _():
        for cp in weight_copies(ge_ref[0], 0):
            cp.start()
        for cp in tile_copies(True, 0, 0):
            cp.start()
        y_s[...] = jnp.zeros_like(y_s)

    nxt = jnp.minimum(i + 1, n_items - 1)

    @pl.when((i + 1 < n_items) & (gn_ref[nxt] > 0))
    def _():
        for cp in tile_copies(True, nxt, 1 - slot):
            cp.start()

    e = ge_ref[i]
    e_prev = ge_ref[jnp.maximum(i - 1, 0)]

    @pl.when((i == 0) | (e != e_prev))
    def _():
        wsl = gs_ref[i]
        nxt_e = gx_ref[i]

        @pl.when(nxt_e >= 0)
        def _():
            for cp in weight_copies(nxt_e, 1 - wsl):
                cp.start()

        for cp in weight_copies(e, wsl):
            cp.wait()
        wgu_s[:, :hid] = wgf_s[wsl].astype(BF16)
        wgu_s[:, hid:] = wuf_s[wsl].astype(BF16)
        wd_s[...] = wdf_s[wsl].astype(BF16)

    n_valid = gn_ref[i]

    @pl.when(n_valid > 0)
    def _():
        for cp in tile_copies(True, i, slot):
            cp.wait()

        @pl.when(i >= 2)
        def _():
            for cp in tile_copies(False, i, slot):
                cp.wait()

        def expert_mlp(rows):
            pk = x_s[slot, :rows]
            valid = lax.broadcasted_iota(I32, pk.shape, 0) < n_valid
            xa, xb = _unpack_bf16_pair(jnp.where(valid, pk, jnp.uint32(0)))
            x = jnp.concatenate([xa.astype(BF16), xb.astype(BF16)], axis=1)
            h2 = jnp.dot(x, wgu_s[...], preferred_element_type=F32)
            hg = h2[:, :hid]
            h = (hg * jax.nn.sigmoid(hg) * h2[:, hid:]).astype(BF16)
            y_s[slot, :rows] = _pack_bf16_pair(jnp.dot(h, wd_s[...], preferred_element_type=F32))

        @pl.when(n_valid > tm // 2)
        def _():
            expert_mlp(tm)

        @pl.when((n_valid > tm // 4) & (n_valid <= tm // 2))
        def _():
            expert_mlp(tm // 2)

        @pl.when(n_valid <= tm // 4)
        def _():
            expert_mlp(tm // 4)

        for cp in tile_copies(False, i, slot):
            cp.start()

    @pl.when(i == n_items - 1)
    def _():
        total = gt_ref[0]

        @pl.when(total >= 2)
        def _():
            for cp in tile_copies(False, 0, total % 2):
                cp.wait()

        for cp in tile_copies(False, 0, (total + 1) % 2):
            cp.wait()


def _gmm_metadata(counts, cap, tm, n_items):
    e = counts.shape[0]
    tiles = (counts + tm - 1) // tm
    item_end = jnp.cumsum(tiles)
    total = item_end[-1]
    it = jnp.arange(n_items, dtype=I32)
    itc = jnp.minimum(it, total - 1)
    ge = jnp.sum((item_end[None, :] <= itc[:, None]).astype(I32), axis=1)
    onehot = ge[:, None] == jnp.arange(e, dtype=I32)[None, :]
    start = jnp.sum(jnp.where(onehot, (item_end - tiles)[None, :], 0), axis=1)
    cnt = jnp.sum(jnp.where(onehot, counts[None, :], 0), axis=1)
    j = itc - start
    gr = ge * (cap // tm) + j
    gn = jnp.where(it < total, jnp.clip(cnt - j * tm, 0, tm), 0)
    ids = jnp.arange(e, dtype=I32)
    live = tiles > 0
    later = live[None, :] & (ids[None, :] > ids[:, None])
    next_e = jnp.min(jnp.where(later, ids[None, :], e), axis=1)
    next_e = jnp.where(next_e < e, next_e, -1)
    wslot = (jnp.cumsum(live.astype(I32)) - 1) % 2
    gx = jnp.sum(jnp.where(onehot, next_e[None, :], 0), axis=1)
    gs = jnp.sum(jnp.where(onehot, wslot[None, :], 0), axis=1)
    return tuple(a.astype(I32) for a in (ge, gr, gn, total.reshape(1), gx, gs))


def _gmm(xs, counts, cap, n_tokens, wg, wu, wd):
    tm = GMM_ROWS
    e, d, hid = wg.shape
    n_items = (n_tokens * TOP_K) // tm + e
    meta = _gmm_metadata(counts, cap, tm, n_items)
    body = functools.partial(_gmm_body, tm=tm, n_items=n_items)
    grid_spec = pltpu.PrefetchScalarGridSpec(
        num_scalar_prefetch=6,
        grid=(n_items,),
        in_specs=[pl.BlockSpec(memory_space=pl.ANY)] * 4,
        out_specs=pl.BlockSpec(memory_space=pl.ANY),
        scratch_shapes=[pltpu.VMEM((2, d, hid), F32), pltpu.VMEM((2, d, hid), F32),
                        pltpu.VMEM((2, hid, d), F32),
                        pltpu.VMEM((d, 2 * hid), BF16), pltpu.VMEM((hid, d), BF16),
                        pltpu.VMEM((2, tm, d // 2), U32), pltpu.VMEM((2, tm, d // 2), U32),
                        pltpu.SemaphoreType.DMA((2,)), pltpu.SemaphoreType.DMA((2,)),
                        pltpu.SemaphoreType.DMA((2,))],
    )
    return pl.pallas_call(
        body,
        grid_spec=grid_spec,
        out_shape=jax.ShapeDtypeStruct(xs.shape, U32),
        compiler_params=_params(1),
        name="expert_gmm",
    )(*meta, xs, wg, wu, wd)


def _final_body(d0_ref, d1_ref, dn_ref, wts_ref, base_ref, p_ref, ys_ref, pg_ref, pb_ref, pp_ref, g2_ref, b2_ref,
                out_ref, rows_s, sum_s, wrep_s, sems, *, bb, tm, n_steps):
    i = pl.program_id(0)
    m = bb * tm
    d = base_ref.shape[1]
    slot = lax.rem(i, ROW_BUFFERS)
    ahead = lax.rem(i + 2, ROW_BUFFERS)

    def row_copy(sl, d_row, kk, n):
        return pltpu.make_async_copy(ys_ref.at[d_row], rows_s.at[sl, kk, n], sems.at[sl])

    def issue_token(sl, dref, n):
        for kk in range(TOP_K):
            row_copy(sl, dref[0, kk, n], kk, n).start(priority=ROW_COPY_PRIORITY)

    def issue_tile(sl, dref):
        def one(n, c):
            issue_token(sl, dref, n)
            return c
        lax.fori_loop(0, m, one, 0)

    def issue_ahead(lo, hi):
        for n in range(lo, hi):
            issue_token(ahead, dn_ref, n)

    def drain(sl):
        def one(n, c):
            for _ in range(DRAIN_UNROLL * TOP_K):
                row_copy(sl, 0, 0, 0).wait()
            return c
        lax.fori_loop(0, m // DRAIN_UNROLL, one, 0)

    @pl.when(i == 0)
    def _():
        issue_tile(0, d0_ref)
        issue_tile(1, d1_ref)

    drain(slot)

    wts = wts_ref[...]
    for kk in range(TOP_K):
        wk = jnp.broadcast_to(wts[:, kk:kk + 1], (m, LANES))
        wrep_s[kk] = _pack_bf16_pair(jnp.concatenate([wk, wk], axis=1))

    def combine(n, c):
        acc = jnp.zeros((2 * ROW_SUBLANES, LANES), BF16)
        for kk in range(TOP_K):
            row = pltpu.bitcast(rows_s[slot, kk, n], BF16)
            wk = jnp.broadcast_to(wrep_s[kk, pl.ds(n, 1), :], (ROW_SUBLANES, LANES))
            acc = acc + pltpu.bitcast(wk, BF16) * row
        sum_s[n] = pltpu.bitcast(acc, U32)
        return c

    lax.fori_loop(0, m, combine, 0, unroll=2)
    issue_ahead(0, m // 2)
    routed = jnp.concatenate(_unpack_bf16_pair(
        jnp.concatenate([sum_s[:, sub, :] for sub in range(ROW_SUBLANES)], axis=1)), axis=1)
    x2 = _layer_norm(base_ref[...] + routed, g2_ref[...], b2_ref[...])
    issue_ahead(m // 2, m)
    gate = jax.nn.sigmoid(jnp.dot(x2.astype(BF16), pg_ref[...], preferred_element_type=F32) + pb_ref[...])
    proj = jnp.dot(p_ref[...].reshape(m, p_ref.shape[-1]).astype(BF16), pp_ref[...], preferred_element_type=F32)
    out_ref[...] = (x2 + gate * proj).reshape(bb, tm, d)

    @pl.when(i == n_steps - 1)
    def _():
        drain(lax.rem(i + 1, ROW_BUFFERS))
        drain(ahead)


def _final(base, dest, wts, p, ys, w, B, T):
    n, D = base.shape
    tm = min(MIX_ROWS, T)
    bb = min(B, MIX_ROWS // tm)
    m = bb * tm
    nt = T // tm
    n_steps = (B // bb) * nt
    pd = p.shape[-1]
    body = functools.partial(_final_body, bb=bb, tm=tm, n_steps=n_steps)
    return pl.pallas_call(
        body,
        grid=(n_steps,),
        in_specs=[pl.BlockSpec((1, TOP_K, m), lambda i: (0, 0, 0), memory_space=pltpu.SMEM),
                  pl.BlockSpec((1, TOP_K, m), lambda i: (min(1, n_steps - 1), 0, 0), memory_space=pltpu.SMEM),
                  pl.BlockSpec((1, TOP_K, m), lambda i: (jnp.minimum(i + 2, n_steps - 1), 0, 0),
                               memory_space=pltpu.SMEM),
                  pl.BlockSpec((m, TOP_K), lambda i: (i, 0)),
                  pl.BlockSpec((m, D), lambda i: (i, 0)),
                  pl.BlockSpec((bb, tm, pd), lambda i: (i // nt, i % nt, 0)),
                  pl.BlockSpec(memory_space=pl.ANY),
                  _const_spec((D, D)),
                  _const_spec((1, D)),
                  _const_spec((pd, D)),
                  _const_spec((1, D)),
                  _const_spec((1, D))],
        out_specs=pl.BlockSpec((bb, tm, D), lambda i: (i // nt, i % nt, 0)),
        out_shape=jax.ShapeDtypeStruct((B, T, D), F32),
        scratch_shapes=[pltpu.VMEM((ROW_BUFFERS, TOP_K, m, ROW_SUBLANES, LANES), U32),
                        pltpu.VMEM((m, ROW_SUBLANES, LANES), U32),
                        pltpu.VMEM((TOP_K, m, LANES), U32),
                        pltpu.SemaphoreType.DMA((ROW_BUFFERS,))],
        compiler_params=_params(1),
        name="combine_final",
    )(dest, dest, dest, wts, base, p, ys, w["pg"], w["pb"], w["pp"], w["g2"], w["b2"])


def _prep_layer(prm):
    (w_in, conv_w, conv_b, lru_wa, lru_ba, lru_wx, lru_bx, lru_lambda, ret_gn, w_out, ln1_g, ln1_b,
     router_w, router_b, e_gate, e_up, e_down, s_gate, s_up, s_down, ln2_g, ln2_b,
     ple_w_proj, ple_w_gate, ple_b_gate) = prm
    row = lambda v: v.reshape(1, -1)
    rw2 = jnp.concatenate([router_w, router_w], axis=1).astype(BF16)
    return dict(
        w_in=w_in.astype(BF16), conv_w=conv_w, conv_b=row(conv_b), wa=lru_wa.astype(BF16), ba=row(lru_ba),
        wx=lru_wx.astype(BF16), bx=row(lru_bx), lam=row(lru_lambda), gn=row(ret_gn), wo=w_out.astype(BF16),
        g1=row(ln1_g), b1=row(ln1_b), rw=rw2, rb=row(jnp.concatenate([router_b, router_b])),
        e_gate=e_gate, e_up=e_up, e_down=e_down, sg=s_gate.astype(BF16), su=s_up.astype(BF16),
        sd=s_down.astype(BF16), g2=row(ln2_g), b2=row(ln2_b), pp=ple_w_proj.astype(BF16),
        pg=ple_w_gate.astype(BF16), pb=row(ple_b_gate))


def _mixers(x, conv_st, lru_st, ret_st, pos0, chunk, w):
    B, T, D = x.shape
    W = conv_st.shape[-1]
    z_lru, z_ret = _inproj(x, w["w_in"], 2 * W)
    y_ret, new_ret = _retention(z_ret, ret_st, w["gn"], pos0, chunk)
    y_lru, conv_tm, new_lru = _lru(z_lru, jnp.transpose(conv_st, (1, 0, 2)), lru_st,
                                   w["conv_w"], w["conv_b"], w["wa"], w["ba"], w["wx"], w["bx"], w["lam"],
                                   after=new_ret)
    return y_lru.reshape(T, B * W), y_ret, jnp.transpose(conv_tm, (1, 0, 2)), new_lru, new_ret


def kernel(x_prompt, x_sample, p_prompt, p_sample, state_conv, state_lru, state_ret, w_in, conv_w, conv_b,
           lru_wa, lru_ba, lru_wx, lru_bx, lru_lambda, ret_gn, w_out, ln1_g, ln1_b, router_w, router_b,
           exp_w_gate, exp_w_up, exp_w_down, sh_w_gate, sh_w_up, sh_w_down, ln2_g, ln2_b,
           ple_w_proj, ple_w_gate, ple_b_gate):
    depth = w_in.shape[0]
    alpha = (2 * depth) ** 0.25
    b_p, t_p, _ = x_prompt.shape
    b_s, t_s, _ = x_sample.shape
    W = state_conv.shape[-1]
    H, dh = state_ret.shape[2], state_ret.shape[3]
    n_tokens = b_p * t_p + b_s * t_s
    cap = -(-n_tokens // GMM_ROWS) * GMM_ROWS
    hp, hs = x_prompt, x_sample
    outs = [[] for _ in range(6)]
    for i in range(depth):
        prm = (w_in[i], conv_w[i], conv_b[i], lru_wa[i], lru_ba[i], lru_wx[i], lru_bx[i], lru_lambda[i],
               ret_gn[i], w_out[i], ln1_g[i], ln1_b[i], router_w[i], router_b[i], exp_w_gate[i], exp_w_up[i],
               exp_w_down[i], sh_w_gate[i], sh_w_up[i], sh_w_down[i], ln2_g[i], ln2_b[i],
               ple_w_proj[i], ple_w_gate[i], ple_b_gate[i])
        w = _prep_layer(prm)
        zc = jnp.zeros((b_p, CONV_WIDTH - 1, W), x_prompt.dtype)
        zl = jnp.zeros((b_p, W), F32)
        zr = jnp.zeros((b_p, H, dh, dh), F32)
        yl_p, yr_p, c_p, l_p, r_p = _mixers(hp, zc, zl, zr, 0, CHUNK, w)
        yl_s, yr_s, c_s, l_s, r_s = _mixers(hs, state_conv[i], state_lru[i], state_ret[i], PAST_LEN, t_s, w)
        for o, val in zip(outs, (c_p, l_p, r_p, c_s, l_s, r_s)):
            o.append(val)
        base_p, dest_p, wts_p, cnt_p, xs = _mix(yl_p, yr_p, hp, w, alpha, jnp.zeros((1, LANES), I32), None, cap)
        base_s, dest_s, wts_s, cnt_all, xs = _mix(yl_s, yr_s, hs, w, alpha, cnt_p, xs, cap)
        ys = _gmm(xs, cnt_all[0, :N_EXPERTS], cap, n_tokens, w["e_gate"], w["e_up"], w["e_down"])
        hp = _final(base_p, dest_p, wts_p, p_prompt[i], ys, w, b_p, t_p)
        hs = _final(base_s, dest_s, wts_s, p_sample[i], ys, w, b_s, t_s)
    return (hp, hs) + tuple(jnp.stack(o) for o in outs)
```

```python
import functools

import jax
import jax.numpy as jnp
from jax import lax
from jax.experimental import pallas as pl
from jax.experimental.pallas import tpu as pltpu

F32 = jnp.float32
BF16 = jnp.bfloat16
U32 = jnp.uint32
I32 = jnp.int32

CHUNK = 64
PAST_LEN = 1024
CONV_WIDTH = 4
LRU_C = 8.0
RET_HEADS = 8
ROPE_BASE = 10000.0
N_EXPERTS = 64
TOP_K = 8
N_GROUPS = 8
TOPK_GROUPS = 4
ROUTED_SCALE = 2.5
LN_EPS = 1e-5
GN_EPS = 1e-6

LANES = 128
ROW_SUBLANES = 8
DRAIN_UNROLL = 8
ISSUE_CHUNKS = 4
ROW_BUFFERS = 3
SHARED_THREAD_COPIES = 3
ROWS_PER_STEP = 512
INPROJ_COLS = 512
MIX_ROWS = 256
GMM_ROWS = 512
RET_GROUP_ROWS = 256
LRU_TIME_TILE = 64
LRU_GATE_ROWS = 128
RET_ALL_HEADS_ELEMS = 64 * 1024
VMEM_LIMIT = 56 * 1024 * 1024


def _const_spec(shape):
    zeros = (0,) * len(shape)
    return pl.BlockSpec(shape, lambda *_: zeros, pipeline_mode=pl.Buffered(1))


def _params(n_axes):
    return pltpu.CompilerParams(dimension_semantics=("arbitrary",) * n_axes,
                                vmem_limit_bytes=VMEM_LIMIT)


def _layer_norm(x, g, b):
    mu = jnp.mean(x, axis=-1, keepdims=True)
    xc = x - mu
    var = jnp.mean(xc * xc, axis=-1, keepdims=True)
    return xc * lax.rsqrt(var + LN_EPS) * g + b


def _pack_bf16_pair(x):
    c = x.shape[1] // 2
    xb = x.astype(BF16).astype(F32)
    hi = pltpu.bitcast(xb[:, :c], U32)
    lo = pltpu.bitcast(xb[:, c:], U32)
    return hi | (lo >> 16)


def _unpack_bf16_pair(pk):
    hi = pltpu.bitcast(pk & jnp.uint32(0xFFFF0000), F32)
    lo = pltpu.bitcast(pk << 16, F32)
    return hi, lo


def _inproj_body(x_ref, w_ref, zl_ref, zr_ref, *, bb, tm, lru_cols, tn):
    d = x_ref.shape[-1]
    x = x_ref[...].reshape(bb * tm, d).astype(BF16)
    for j in range(w_ref.shape[1] // tn):
        c0 = j * tn
        acc = jnp.dot(x, w_ref[:, c0:c0 + tn], preferred_element_type=F32)
        for b in range(bb):
            rows = acc[b * tm:(b + 1) * tm]
            if c0 < lru_cols:
                zl_ref[:, b * lru_cols + c0:b * lru_cols + c0 + tn] = rows
            else:
                zr_ref[b, :, c0 - lru_cols:c0 - lru_cols + tn] = rows.astype(BF16)


def _inproj_cols_body(x_ref, w_ref, zl_ref, zr_ref, xb_s, *, n_lru):
    j = pl.program_id(0)
    bb, tm, d = x_ref.shape

    @pl.when(j == 0)
    def _():
        xb_s[...] = x_ref[...].reshape(bb * tm, d).astype(BF16)

    acc = jnp.dot(xb_s[...], w_ref[...], preferred_element_type=F32)

    @pl.when(j < n_lru)
    def _():
        for b in range(bb):
            zl_ref[:, b, :] = acc[b * tm:(b + 1) * tm]

    @pl.when(j >= n_lru)
    def _():
        zr_ref[...] = acc.reshape(zr_ref.shape).astype(BF16)


def _inproj_cols(x, w_in_b, lru_cols):
    B, T, D = x.shape
    n_cols = w_in_b.shape[1]
    tn = INPROJ_COLS
    n_lru = lru_cols // tn
    body = functools.partial(_inproj_cols_body, n_lru=n_lru)
    return pl.pallas_call(
        body,
        grid=(n_cols // tn,),
        in_specs=[_const_spec((B, T, D)),
                  pl.BlockSpec((D, tn), lambda j: (0, j))],
        out_specs=[pl.BlockSpec((T, B, tn), lambda j: (0, 0, jnp.minimum(j, n_lru - 1))),
                   pl.BlockSpec((B, T, tn), lambda j: (0, 0, jnp.maximum(j - n_lru, 0)))],
        out_shape=[jax.ShapeDtypeStruct((T, B, lru_cols), F32),
                   jax.ShapeDtypeStruct((B, T, n_cols - lru_cols), BF16)],
        scratch_shapes=[pltpu.VMEM((B * T, D), BF16)],
        compiler_params=_params(1),
        name="inproj",
    )(x, w_in_b)


def _inproj(x, w_in_b, lru_cols):
    B, T, D = x.shape
    if B * T <= ROWS_PER_STEP:
        return _inproj_cols(x, w_in_b, lru_cols)
    z_lru, z_ret = _inproj_rows(x, w_in_b, lru_cols)
    return z_lru.reshape(T, B, lru_cols), z_ret


def _inproj_rows(x, w_in_b, lru_cols):
    B, T, D = x.shape
    n_cols = w_in_b.shape[1]
    ret_cols = n_cols - lru_cols
    tm = min(ROWS_PER_STEP, T)
    bb = min(B, ROWS_PER_STEP // tm)
    body = functools.partial(_inproj_body, bb=bb, tm=tm, lru_cols=lru_cols, tn=INPROJ_COLS)
    return pl.pallas_call(
        body,
        grid=(B // bb, T // tm),
        in_specs=[pl.BlockSpec((bb, tm, D), lambda b, t: (b, t, 0)),
                  _const_spec((D, n_cols))],
        out_specs=[pl.BlockSpec((tm, bb * lru_cols), lambda b, t: (t, b)),
                   pl.BlockSpec((bb, tm, ret_cols), lambda b, t: (b, t, 0))],
        out_shape=[jax.ShapeDtypeStruct((T, B * lru_cols), F32),
                   jax.ShapeDtypeStruct((B, T, ret_cols), BF16)],
        compiler_params=_params(2),
        name="inproj",
    )(x, w_in_b)


def _lru_body(xl_ref, gl_ref, conv0_ref, h0_ref, cw_ref, cb_ref, wa_ref, ba_ref, wx_ref, bx_ref, lam_ref,
              after_ref, y_ref, conv_out_ref, h_out_ref, xp_s, a_s, b_s, h_s, *, tt, rows):
    del after_ref
    i = pl.program_id(0)
    B, W = h0_ref.shape
    nblk = wa_ref.shape[0]
    blk = W // nblk

    @pl.when(i == 0)
    def _():
        xp_s[0:CONV_WIDTH - 1] = conv0_ref[...]
        h_s[...] = h0_ref[...]

    xp_s[CONV_WIDTH - 1:] = xl_ref[...]

    lam = lam_ref[...]
    neg = -lam
    softplus = jnp.maximum(neg, 0.0) + jnp.log1p(jnp.exp(-jnp.abs(neg)))
    decay = (-LRU_C) * softplus

    def gates(c, carry):
        t0 = pl.multiple_of(c * rows, rows)
        xc = cb_ref[...].reshape(1, 1, W)
        for j in range(CONV_WIDTH):
            xc = xc + xp_s[pl.ds(t0 + j, rows)] * cw_ref[j:j + 1].reshape(1, 1, W)
        xc2 = xc.reshape(rows * B, W)
        xcb = xc2.astype(BF16)
        r_parts, i_parts = [], []
        for n in range(nblk):
            xb = xcb[:, n * blk:(n + 1) * blk]
            r_parts.append(jnp.dot(xb, wa_ref[n], preferred_element_type=F32))
            i_parts.append(jnp.dot(xb, wx_ref[n], preferred_element_type=F32))
        r = jax.nn.sigmoid(jnp.concatenate(r_parts, axis=1) + ba_ref[...])
        ig = jax.nn.sigmoid(jnp.concatenate(i_parts, axis=1) + bx_ref[...])
        a = jnp.exp(decay * r)
        bterm = jnp.sqrt(1.0 - a * a) * (ig * xc2)
        a_s[pl.ds(t0, rows)] = a.reshape(rows, B, W)
        b_s[pl.ds(t0, rows)] = bterm.reshape(rows, B, W)
        return carry

    lax.fori_loop(0, tt // rows, gates, 0)

    def step(t, h):
        hn = a_s[t] * h + b_s[t]
        y_ref[t] = hn * jax.nn.gelu(gl_ref[t])
        return hn

    h_last = lax.fori_loop(0, tt, step, h_s[...], unroll=8)
    h_s[...] = h_last
    tail = xp_s[tt:tt + CONV_WIDTH - 1]
    xp_s[0:CONV_WIDTH - 1] = tail
    conv_out_ref[...] = tail
    h_out_ref[...] = h_last


def _lru(z_lru3, conv0_tm, h0, conv_w, conv_b, wa_b, ba, wx_b, bx, lam, after):
    T, B, W2 = z_lru3.shape
    W = W2 // 2
    tt = min(LRU_TIME_TILE, T)
    rows = max(1, min(tt, LRU_GATE_ROWS // B))
    body = functools.partial(_lru_body, tt=tt, rows=rows)
    nb = wa_b.shape[0]
    blk = W // nb
    return pl.pallas_call(
        body,
        grid=(T // tt,),
        in_specs=[pl.BlockSpec((tt, B, W), lambda t: (t, 0, 0)),
                  pl.BlockSpec((tt, B, W), lambda t: (t, 0, 1)),
                  _const_spec((CONV_WIDTH - 1, B, W)),
                  _const_spec((B, W)),
                  _const_spec((CONV_WIDTH, W)),
                  _const_spec((1, W)),
                  _const_spec((nb, blk, blk)),
                  _const_spec((1, W)),
                  _const_spec((nb, blk, blk)),
                  _const_spec((1, W)),
                  _const_spec((1, W)),
                  pl.BlockSpec(memory_space=pl.ANY)],
        out_specs=[pl.BlockSpec((tt, B, W), lambda t: (t, 0, 0)),
                   pl.BlockSpec((CONV_WIDTH - 1, B, W), lambda t: (0, 0, 0)),
                   pl.BlockSpec((B, W), lambda t: (0, 0))],
        out_shape=[jax.ShapeDtypeStruct((T, B, W), F32),
                   jax.ShapeDtypeStruct((CONV_WIDTH - 1, B, W), F32),
                   jax.ShapeDtypeStruct((B, W), F32)],
        scratch_shapes=[pltpu.VMEM((tt + CONV_WIDTH - 1, B, W), F32),
                        pltpu.VMEM((tt, B, W), F32),
                        pltpu.VMEM((tt, B, W), F32),
                        pltpu.VMEM((B, W), F32)],
        compiler_params=_params(1),
        name="rglru",
    )(z_lru3, z_lru3, conv0_tm, h0, conv_w, conv_b, wa_b, ba, wx_b, bx, lam, after)


def _ret_body(q_ref, k_ref, v_ref, g_ref, cos_ref, sin_ref, mask_ref, qdec_ref, kdec_ref, cdec_ref, gn_ref,
              s0_ref, y_ref, s_out_ref, *, rg, hb, dh):
    T = q_ref.shape[1]
    scale = dh ** -0.5

    def rope(t, cos, sin):
        return t * cos + pltpu.roll(t, dh // 2, axis=1) * sin

    for hh in range(hb):
        cols = slice(hh * dh, (hh + 1) * dh)

        def group(c, s, hh=hh, cols=cols):
            r0 = pl.multiple_of(c * rg, rg)
            rws = pl.ds(r0, rg)
            cos = cos_ref[rws, :]
            sin = sin_ref[rws, :]
            q = rope(q_ref[0, rws, cols].astype(F32), cos, sin)
            k = rope(k_ref[0, rws, cols].astype(F32), cos, sin) * scale
            v = v_ref[0, rws, cols]
            scores = lax.dot_general(q.astype(BF16), k.astype(BF16), (((1,), (1,)), ((), ())),
                                     preferred_element_type=F32)
            scores = scores * mask_ref[hh]
            o = jnp.dot(scores.astype(BF16), v, preferred_element_type=F32)
            o = o + jnp.dot((q * qdec_ref[hh]).astype(BF16), s.astype(BF16), preferred_element_type=F32)
            kd = (k * kdec_ref[hh]).astype(BF16)
            kv = lax.dot_general(kd, v, (((0,), (0,)), ((), ())), preferred_element_type=F32)
            s_new = cdec_ref[hh] * s + kv
            mu = jnp.mean(o, axis=-1, keepdims=True)
            oc = o - mu
            var = jnp.mean(oc * oc, axis=-1, keepdims=True)
            on = oc * lax.rsqrt(var + GN_EPS) * gn_ref[:, cols]
            g = g_ref[0, rws, cols].astype(F32)
            y_ref[0, rws, cols] = (g * jax.nn.sigmoid(g) * on).astype(y_ref.dtype)
            return s_new

        n_groups = T // rg
        s_out_ref[0, hh] = lax.fori_loop(0, n_groups, group, s0_ref[0, hh],
                                         unroll=4 if n_groups % 4 == 0 else 1)


def _retention_tables(T, pos0, chunk, rg, dh):
    half = dh // 2
    inv = ROPE_BASE ** (-jnp.arange(half, dtype=F32) / half)
    pos = pos0 + jnp.arange(T)
    ang = pos.astype(F32)[:, None] * inv[None, :]
    cos, sin = jnp.cos(ang), jnp.sin(ang)
    cos2 = jnp.concatenate([cos, cos], axis=1)
    sin2 = jnp.concatenate([-sin, sin], axis=1)
    log_g = jnp.log1p(-jnp.exp2(-5.0 - jnp.arange(RET_HEADS, dtype=F32)))[:, None, None]
    idx = jnp.arange(rg, dtype=F32)
    ci = jnp.floor(idx / chunk)
    diff = idx[:, None] - idx[None, :]
    same = ci[:, None] == ci[None, :]
    earlier = ci[None, :] < ci[:, None]
    dist = jnp.where(same, jnp.abs(diff), diff)
    mask = jnp.where(same | earlier, jnp.exp(dist[None] * log_g), 0.0)
    ones = jnp.ones((1, 1, dh), F32)
    qdec = jnp.exp((idx + 1.0)[None, :, None] * log_g) * ones
    kdec = jnp.exp((rg - 1.0 - idx)[None, :, None] * log_g) * ones
    cdec = jnp.exp(rg * log_g) * ones
    return cos2, sin2, mask, qdec, kdec, cdec


def _retention(z_ret, s0, gn, pos0, chunk):
    B, T, C4 = z_ret.shape
    H = RET_HEADS
    dh = C4 // (4 * H)
    rg = min(T, max(chunk, (RET_GROUP_ROWS // chunk) * chunk))
    hb = H if T * H * dh <= RET_ALL_HEADS_ELEMS else 1
    nh = H // hb
    cos2, sin2, mask, qdec, kdec, cdec = _retention_tables(T, pos0, chunk, rg, dh)
    body = functools.partial(_ret_body, rg=rg, hb=hb, dh=dh)
    col = lambda off: (lambda b, h: (b, 0, off * nh + h))
    return pl.pallas_call(
        body,
        grid=(B, nh),
        in_specs=[pl.BlockSpec((1, T, hb * dh), col(0)),
                  pl.BlockSpec((1, T, hb * dh), col(1)),
                  pl.BlockSpec((1, T, hb * dh), col(2)),
                  pl.BlockSpec((1, T, hb * dh), col(3)),
                  _const_spec((T, dh)),
                  _const_spec((T, dh)),
                  pl.BlockSpec((hb, rg, rg), lambda b, h: (h, 0, 0)),
                  pl.BlockSpec((hb, rg, dh), lambda b, h: (h, 0, 0)),
                  pl.BlockSpec((hb, rg, dh), lambda b, h: (h, 0, 0)),
                  pl.BlockSpec((hb, 1, dh), lambda b, h: (h, 0, 0)),
                  pl.BlockSpec((1, hb * dh), lambda b, h: (0, h)),
                  pl.BlockSpec((1, hb, dh, dh), lambda b, h: (b, h, 0, 0))],
        out_specs=[pl.BlockSpec((1, T, hb * dh), lambda b, h: (b, 0, h)),
                   pl.BlockSpec((1, hb, dh, dh), lambda b, h: (b, h, 0, 0))],
        out_shape=[jax.ShapeDtypeStruct((B, T, H * dh), BF16),
                   jax.ShapeDtypeStruct((B, H, dh, dh), F32)],
        compiler_params=_params(2),
        name="retention",
    )(z_ret, z_ret, z_ret, z_ret, cos2, sin2, mask, qdec, kdec, cdec, gn, s0)


def _seg_allreduce(v, lane, op):
    for s in (1, 2, 4):
        up = pltpu.roll(v, LANES - s, axis=1)
        dn = pltpu.roll(v, s, axis=1)
        v = op(v, jnp.where((lane & s) == 0, up, dn))
    return v


def _mix_body(*refs, bb, tm, alpha, cap, n_steps, aliased):
    (yl_ref, yr_ref, x_ref, wo_ref, g1_ref, b1_ref, rw_ref, rb_ref, sg_ref, su_ref, sd_ref,
     tri_ref, cnt_in_ref) = refs[:13]
    refs = refs[14:] if aliased else refs[13:]
    base_ref, dest_ref, wts_ref, cnt_ref, xs_ref, carry_s, xpk_s, dv_s, ds_s, row_sems, idx_sem = refs
    i = pl.program_id(0)
    m = bb * tm
    d = x_ref.shape[-1]
    w = yl_ref.shape[1] // bb
    slot = i % 2
    chunk = m // ISSUE_CHUNKS

    def row_copy(sl, n, dst):
        return pltpu.make_async_copy(xpk_s.at[sl, n], xs_ref.at[dst], row_sems.at[sl])

    def issue_rows(sl, lo, hi):
        def one(n, c):
            for kk in range(TOP_K):
                row_copy(sl, n, ds_s[sl, kk, n]).start(priority=int(kk >= SHARED_THREAD_COPIES))
            return c
        lax.fori_loop(lo, hi, one, 0)

    def issue_prev_chunk(c):
        for n in range(c * chunk, (c + 1) * chunk):
            for kk in range(TOP_K):
                row_copy(1 - slot, n, ds_s[1 - slot, kk, n]).start(priority=int(kk >= SHARED_THREAD_COPIES))

    def drain(sl):
        def one(n, c):
            for _ in range(DRAIN_UNROLL * TOP_K):
                row_copy(sl, 0, 0).wait()
            return c
        lax.fori_loop(0, m // DRAIN_UNROLL, one, 0)

    @pl.when(i == 0)
    def _():
        carry_s[...] = jnp.zeros_like(carry_s)
        carry_s[0:1, :] = cnt_in_ref[...].astype(F32)
        xpk_s[1] = jnp.zeros(xpk_s.shape[1:], U32)
        spare = (N_EXPERTS * cap + lax.broadcasted_iota(I32, (TOP_K, m), 0) * m
                 + lax.broadcasted_iota(I32, (TOP_K, m), 1))
        dv_s[...] = spare
        first = pltpu.make_async_copy(dv_s, ds_s.at[1], idx_sem)
        first.start()
        first.wait()

    issue_prev_chunk(0)
    yl = jnp.concatenate([yl_ref[:, b * w:(b + 1) * w] for b in range(bb)], axis=0).astype(BF16)
    yr = yr_ref[...].reshape(m, yr_ref.shape[-1])
    mix = jnp.dot(jnp.concatenate([yl, yr], axis=1), wo_ref[...], preferred_element_type=F32)
    x1 = _layer_norm(alpha * x_ref[...].reshape(m, d) + mix, g1_ref[...], b1_ref[...])
    x1b = x1.astype(BF16)
    issue_prev_chunk(1)

    logits = jnp.dot(x1b, rw_ref[...], preferred_element_type=F32)

    s = jax.nn.sigmoid(logits)
    sb = s + rb_ref[...]
    lane = lax.broadcasted_iota(I32, (m, LANES), 1)
    e_id = lane & (N_EXPERTS - 1)
    e_f = e_id.astype(F32)
    low = lane < N_EXPERTS
    e_low = jnp.where(low, e_f, -1.0)
    grp = e_id >> 3
    big = jnp.float32(1e9)
    ninf = jnp.float32(-jnp.inf)

    m1 = _seg_allreduce(sb, lane, jnp.maximum)
    first_max = _seg_allreduce(jnp.where(sb == m1, e_f, big), lane, jnp.minimum)
    m2 = _seg_allreduce(jnp.where(e_f == first_max, ninf, sb), lane, jnp.maximum)
    gs = m1 + m2
    hg = jnp.dot(x1b, sg_ref[...], preferred_element_type=F32)
    issue_prev_chunk(2)
    rank = jnp.zeros((m, LANES), F32)
    for dgrp in range(1, N_GROUPS):
        other = pltpu.roll(gs, 8 * dgrp, axis=1)
        tie = jnp.where(grp >= dgrp, 1.0, 0.0)
        rank = rank + jnp.where(other > gs, 1.0, jnp.where(other == gs, tie, 0.0))
    v = jnp.where(rank < TOPK_GROUPS, jnp.where(low, sb, ninf), ninf)
    hu = jnp.dot(x1b, su_ref[...], preferred_element_type=F32)
    hs = (hg * jax.nn.sigmoid(hg) * hu).astype(BF16)

    idx_cols, w_cols = [], []
    sel = jnp.zeros((m, LANES), F32)
    for rnd in range(TOP_K):
        mx = jnp.max(v, axis=1, keepdims=True)
        idx = jnp.min(jnp.where(v == mx, e_f, big), axis=1, keepdims=True)
        hit = e_low == idx
        w_cols.append(jnp.sum(jnp.where(hit, s, 0.0), axis=1, keepdims=True))
        idx_cols.append(idx)
        v = jnp.where(hit, ninf, v)
        sel = jnp.where(hit, 1.0, sel)
        if rnd == TOP_K // 2 - 1:
            base_ref[...] = alpha * x1 + jnp.dot(hs, sd_ref[...], preferred_element_type=F32)
            issue_prev_chunk(3)

    packed = _pack_bf16_pair(x1)
    for sub in range(ROW_SUBLANES):
        xpk_s[slot, :, sub, :] = packed[:, sub * LANES:(sub + 1) * LANES]

    cum = jnp.dot(tri_ref[...], sel.astype(BF16), preferred_element_type=F32) + carry_s[0:1, :]
    carry_s[0:1, :] = carry_s[0:1, :] + jnp.sum(sel, axis=0, keepdims=True)
    cnt_ref[...] = carry_s[0:1, :].astype(I32)

    wsum = w_cols[0]
    for c in w_cols[1:]:
        wsum = wsum + c
    d_out = jnp.zeros((m, LANES), F32)
    w_out = jnp.zeros((m, LANES), F32)
    for kk in range(TOP_K):
        hit = e_low == idx_cols[kk]
        pk = jnp.sum(jnp.where(hit, cum, 0.0), axis=1, keepdims=True)
        d_out = jnp.where(lane == kk, idx_cols[kk] * float(cap) + pk, d_out)
        w_out = jnp.where(lane == kk, w_cols[kk] / wsum * ROUTED_SCALE, w_out)
    wts_ref[...] = w_out[:, :TOP_K]
    dest_t = jnp.transpose(d_out)[:TOP_K].astype(I32)
    dest_ref[0] = dest_t
    dv_s[...] = dest_t
    to_smem = pltpu.make_async_copy(dv_s, ds_s.at[slot], idx_sem)
    to_smem.start()
    to_smem.wait()

    drain(1 - slot)

    @pl.when(i == n_steps - 1)
    def _():
        issue_rows(slot, 0, m)
        drain(slot)


def _mix(y_lru2, y_ret, x, w, alpha, cnt_in, xs, cap):
    B, T, D = x.shape
    assert D == 2 * ROW_SUBLANES * LANES, "a packed token row must fill exactly one (8, 128) tile"
    W = y_ret.shape[-1]
    tm = min(MIX_ROWS, T)
    bb = min(B, MIX_ROWS // tm)
    m = bb * tm
    n = B * T
    nt = T // tm
    n_steps = (B // bb) * nt
    hs = w["sg"].shape[1]
    aliased = xs is not None
    tri = (lax.broadcasted_iota(I32, (m, m), 1) < lax.broadcasted_iota(I32, (m, m), 0)).astype(BF16)
    body = functools.partial(_mix_body, bb=bb, tm=tm, alpha=alpha, cap=cap, n_steps=n_steps, aliased=aliased)
    in_specs = [pl.BlockSpec((tm, bb * W), lambda i: (i % nt, i // nt)),
                pl.BlockSpec((bb, tm, W), lambda i: (i // nt, i % nt, 0)),
                pl.BlockSpec((bb, tm, D), lambda i: (i // nt, i % nt, 0)),
                _const_spec((2 * W, D)),
                _const_spec((1, D)),
                _const_spec((1, D)),
                _const_spec((D, LANES)),
                _const_spec((1, LANES)),
                _const_spec((D, hs)),
                _const_spec((D, hs)),
                _const_spec((hs, D)),
                _const_spec((m, m)),
                _const_spec((1, LANES))]
    args = [y_lru2, y_ret, x, w["wo"], w["g1"], w["b1"], w["rw"], w["rb"], w["sg"], w["su"], w["sd"],
            tri, cnt_in]
    if aliased:
        in_specs.append(pl.BlockSpec(memory_space=pl.ANY))
        args.append(xs)
    return pl.pallas_call(
        body,
        grid=(n_steps,),
        in_specs=in_specs,
        out_specs=[pl.BlockSpec((m, D), lambda i: (i, 0)),
                   pl.BlockSpec((1, TOP_K, m), lambda i: (i, 0, 0)),
                   pl.BlockSpec((m, TOP_K), lambda i: (i, 0)),
                   pl.BlockSpec((1, LANES), lambda i: (0, 0)),
                   pl.BlockSpec(memory_space=pl.ANY)],
        out_shape=[jax.ShapeDtypeStruct((n, D), F32),
                   jax.ShapeDtypeStruct((n_steps, TOP_K, m), I32),
                   jax.ShapeDtypeStruct((n, TOP_K), F32),
                   jax.ShapeDtypeStruct((1, LANES), I32),
                   jax.ShapeDtypeStruct((N_EXPERTS * cap + MIX_ROWS * TOP_K, ROW_SUBLANES, LANES), U32)],
        scratch_shapes=[pltpu.VMEM((8, LANES), F32),
                        pltpu.VMEM((2, m, ROW_SUBLANES, LANES), U32),
                        pltpu.VMEM((TOP_K, m), I32),
                        pltpu.SMEM((2, TOP_K, m), I32),
                        pltpu.SemaphoreType.DMA((2,)),
                        pltpu.SemaphoreType.DMA(())],
        input_output_aliases={13: 4} if aliased else {},
        compiler_params=_params(1),
        name="mix_router",
    )(*args)


def _gmm_body(ge_ref, gr_ref, gn_ref, gt_ref, gx_ref, gs_ref, xs_ref, wg_ref, wu_ref, wd_ref, ys_ref,
              wgf_s, wuf_s, wdf_s, wgu_s, wd_s, x_s, y_s, w_sems, in_sems, out_sems, *, tm, n_items):
    i = pl.program_id(0)
    hid = wg_ref.shape[2]
    slot = i % 2

    def weight_copies(expert, sl):
        return [pltpu.make_async_copy(wg_ref.at[expert], wgf_s.at[sl], w_sems.at[sl]),
                pltpu.make_async_copy(wu_ref.at[expert], wuf_s.at[sl], w_sems.at[sl]),
                pltpu.make_async_copy(wd_ref.at[expert], wdf_s.at[sl], w_sems.at[sl])]

    def tile_copies(to_vmem, item, sl):
        r0 = pl.multiple_of(gr_ref[item] * tm, tm)
        out = []
        for sub in range(ROW_SUBLANES):
            cols = pl.ds(sub * LANES, LANES)
            if to_vmem:
                out.append(pltpu.make_async_copy(xs_ref.at[pl.ds(r0, tm), sub, :], x_s.at[sl, :, cols],
                                                 in_sems.at[sl]))
            else:
                out.append(pltpu.make_async_copy(y_s.at[sl, :, cols], ys_ref.at[pl.ds(r0, tm), sub, :],
                                                 out_sems.at[sl]))
        return out

    @pl.when(i == 0)
    def _():
        for cp in weight_copies(ge_ref[0], 0):
            cp.start()
        for cp in tile_copies(True, 0, 0):
            cp.start()
        y_s[...] = jnp.zeros_like(y_s)

    nxt = jnp.minimum(i + 1, n_items - 1)

    @pl.when((i + 1 < n_items) & (gn_ref[nxt] > 0))
    def _():
        for cp in tile_copies(True, nxt, 1 - slot):
            cp.start()

    e = ge_ref[i]
    e_prev = ge_ref[jnp.maximum(i - 1, 0)]

    @pl.when((i == 0) | (e != e_prev))
    def _():
        wsl = gs_ref[i]
        nxt_e = gx_ref[i]

        @pl.when(nxt_e >= 0)
        def _():
            for cp in weight_copies(nxt_e, 1 - wsl):
                cp.start()

        for cp in weight_copies(e, wsl):
            cp.wait()
        wgu_s[:, :hid] = wgf_s[wsl].astype(BF16)
        wgu_s[:, hid:] = wuf_s[wsl].astype(BF16)
        wd_s[...] = wdf_s[wsl].astype(BF16)

    n_valid = gn_ref[i]

    @pl.when(n_valid > 0)
    def _():
        for cp in tile_copies(True, i, slot):
            cp.wait()

        @pl.when(i >= 2)
        def _():
            for cp in tile_copies(False, i, slot):
                cp.wait()

        def expert_mlp(rows):
            pk = x_s[slot, :rows]
            valid = lax.broadcasted_iota(I32, pk.shape, 0) < n_valid
            xa, xb = _unpack_bf16_pair(jnp.where(valid, pk, jnp.uint32(0)))
            x = jnp.concatenate([xa.astype(BF16), xb.astype(BF16)], axis=1)
            h2 = jnp.dot(x, wgu_s[...], preferred_element_type=F32)
            hg = h2[:, :hid]
            h = (hg * jax.nn.sigmoid(hg) * h2[:, hid:]).astype(BF16)
            y_s[slot, :rows] = _pack_bf16_pair(jnp.dot(h, wd_s[...], preferred_element_type=F32))

        @pl.when(n_valid > tm // 2)
        def _():
            expert_mlp(tm)

        @pl.when((n_valid > tm // 4) & (n_valid <= tm // 2))
        def _():
            expert_mlp(tm // 2)

        @pl.when(n_valid <= tm // 4)
        def _():
            expert_mlp(tm // 4)

        for cp in tile_copies(False, i, slot):
            cp.start()

    @pl.when(i == n_items - 1)
    def _():
        total = gt_ref[0]

        @pl.when(total >= 2)
        def _():
            for cp in tile_copies(False, 0, total % 2):
                cp.wait()

        for cp in tile_copies(False, 0, (total + 1) % 2):
            cp.wait()


def _gmm_metadata(counts, cap, tm, n_items):
    e = counts.shape[0]
    tiles = (counts + tm - 1) // tm
    item_end = jnp.cumsum(tiles)
    total = item_end[-1]
    it = jnp.arange(n_items, dtype=I32)
    itc = jnp.minimum(it, total - 1)
    ge = jnp.sum((item_end[None, :] <= itc[:, None]).astype(I32), axis=1)
    onehot = ge[:, None] == jnp.arange(e, dtype=I32)[None, :]
    start = jnp.sum(jnp.where(onehot, (item_end - tiles)[None, :], 0), axis=1)
    cnt = jnp.sum(jnp.where(onehot, counts[None, :], 0), axis=1)
    j = itc - start
    gr = ge * (cap // tm) + j
    gn = jnp.where(it < total, jnp.clip(cnt - j * tm, 0, tm), 0)
    ids = jnp.arange(e, dtype=I32)
    live = tiles > 0
    later = live[None, :] & (ids[None, :] > ids[:, None])
    next_e = jnp.min(jnp.where(later, ids[None, :], e), axis=1)
    next_e = jnp.where(next_e < e, next_e, -1)
    wslot = (jnp.cumsum(live.astype(I32)) - 1) % 2
    gx = jnp.sum(jnp.where(onehot, next_e[None, :], 0), axis=1)
    gs = jnp.sum(jnp.where(onehot, wslot[None, :], 0), axis=1)
    return tuple(a.astype(I32) for a in (ge, gr, gn, total.reshape(1), gx, gs))


def _gmm(xs, counts, cap, n_tokens, wg, wu, wd):
    tm = GMM_ROWS
    e, d, hid = wg.shape
    n_items = (n_tokens * TOP_K) // tm + e
    meta = _gmm_metadata(counts, cap, tm, n_items)
    body = functools.partial(_gmm_body, tm=tm, n_items=n_items)
    grid_spec = pltpu.PrefetchScalarGridSpec(
        num_scalar_prefetch=6,
        grid=(n_items,),
        in_specs=[pl.BlockSpec(memory_space=pl.ANY)] * 4,
        out_specs=pl.BlockSpec(memory_space=pl.ANY),
        scratch_shapes=[pltpu.VMEM((2, d, hid), F32), pltpu.VMEM((2, d, hid), F32),
                        pltpu.VMEM((2, hid, d), F32),
                        pltpu.VMEM((d, 2 * hid), BF16), pltpu.VMEM((hid, d), BF16),
                        pltpu.VMEM((2, tm, d // 2), U32), pltpu.VMEM((2, tm, d // 2), U32),
                        pltpu.SemaphoreType.DMA((2,)), pltpu.SemaphoreType.DMA((2,)),
                        pltpu.SemaphoreType.DMA((2,))],
    )
    return pl.pallas_call(
        body,
        grid_spec=grid_spec,
        out_shape=jax.ShapeDtypeStruct(xs.shape, U32),
        compiler_params=_params(1),
        name="expert_gmm",
    )(*meta, xs, wg, wu, wd)


def _final_body(d0_ref, d1_ref, dn_ref, wts_ref, base_ref, p_ref, ys_ref, pg_ref, pb_ref, pp_ref, g2_ref, b2_ref,
                out_ref, rows_s, sum_s, wrep_s, sems, *, bb, tm, n_steps):
    i = pl.program_id(0)
    m = bb * tm
    d = base_ref.shape[1]
    slot = lax.rem(i, ROW_BUFFERS)
    ahead = lax.rem(i + 2, ROW_BUFFERS)

    def row_copy(sl, d_row, kk, n):
        return pltpu.make_async_copy(ys_ref.at[d_row], rows_s.at[sl, kk, n], sems.at[sl])

    def issue_token(sl, dref, n):
        for kk in range(TOP_K):
            row_copy(sl, dref[0, kk, n], kk, n).start(priority=int(kk >= SHARED_THREAD_COPIES))

    def issue_tile(sl, dref):
        def one(n, c):
            issue_token(sl, dref, n)
            return c
        lax.fori_loop(0, m, one, 0)

    def issue_ahead(lo, hi):
        for n in range(lo, hi):
            issue_token(ahead, dn_ref, n)

    def drain(sl):
        def one(n, c):
            for _ in range(DRAIN_UNROLL * TOP_K):
                row_copy(sl, 0, 0, 0).wait()
            return c
        lax.fori_loop(0, m // DRAIN_UNROLL, one, 0)

    @pl.when(i == 0)
    def _():
        issue_tile(0, d0_ref)
        issue_tile(1, d1_ref)

    drain(slot)

    wts = wts_ref[...]
    for kk in range(TOP_K):
        wk = jnp.broadcast_to(wts[:, kk:kk + 1], (m, LANES))
        wrep_s[kk] = _pack_bf16_pair(jnp.concatenate([wk, wk], axis=1))

    def combine(n, c):
        acc = jnp.zeros((2 * ROW_SUBLANES, LANES), BF16)
        for kk in range(TOP_K):
            row = pltpu.bitcast(rows_s[slot, kk, n], BF16)
            wk = jnp.broadcast_to(wrep_s[kk, pl.ds(n, 1), :], (ROW_SUBLANES, LANES))
            acc = acc + pltpu.bitcast(wk, BF16) * row
        sum_s[n] = pltpu.bitcast(acc, U32)
        return c

    lax.fori_loop(0, m, combine, 0, unroll=2)
    issue_ahead(0, m // 2)
    routed = jnp.concatenate(_unpack_bf16_pair(
        jnp.concatenate([sum_s[:, sub, :] for sub in range(ROW_SUBLANES)], axis=1)), axis=1)
    x2 = _layer_norm(base_ref[...] + routed, g2_ref[...], b2_ref[...])
    issue_ahead(m // 2, m)
    gate = jax.nn.sigmoid(jnp.dot(x2.astype(BF16), pg_ref[...], preferred_element_type=F32) + pb_ref[...])
    proj = jnp.dot(p_ref[...].reshape(m, p_ref.shape[-1]).astype(BF16), pp_ref[...], preferred_element_type=F32)
    out_ref[...] = (x2 + gate * proj).reshape(bb, tm, d)

    @pl.when(i == n_steps - 1)
    def _():
        drain(lax.rem(i + 1, ROW_BUFFERS))
        drain(ahead)


def _final(base, dest, wts, p, ys, w, B, T):
    n, D = base.shape
    tm = min(MIX_ROWS, T)
    bb = min(B, MIX_ROWS // tm)
    m = bb * tm
    nt = T // tm
    n_steps = (B // bb) * nt
    pd = p.shape[-1]
    body = functools.partial(_final_body, bb=bb, tm=tm, n_steps=n_steps)
    return pl.pallas_call(
        body,
        grid=(n_steps,),
        in_specs=[pl.BlockSpec((1, TOP_K, m), lambda i: (0, 0, 0), memory_space=pltpu.SMEM),
                  pl.BlockSpec((1, TOP_K, m), lambda i: (min(1, n_steps - 1), 0, 0), memory_space=pltpu.SMEM),
                  pl.BlockSpec((1, TOP_K, m), lambda i: (jnp.minimum(i + 2, n_steps - 1), 0, 0),
                               memory_space=pltpu.SMEM),
                  pl.BlockSpec((m, TOP_K), lambda i: (i, 0)),
                  pl.BlockSpec((m, D), lambda i: (i, 0)),
                  pl.BlockSpec((bb, tm, pd), lambda i: (i // nt, i % nt, 0)),
                  pl.BlockSpec(memory_space=pl.ANY),
                  _const_spec((D, D)),
                  _const_spec((1, D)),
                  _const_spec((pd, D)),
                  _const_spec((1, D)),
                  _const_spec((1, D))],
        out_specs=pl.BlockSpec((bb, tm, D), lambda i: (i // nt, i % nt, 0)),
        out_shape=jax.ShapeDtypeStruct((B, T, D), F32),
        scratch_shapes=[pltpu.VMEM((ROW_BUFFERS, TOP_K, m, ROW_SUBLANES, LANES), U32),
                        pltpu.VMEM((m, ROW_SUBLANES, LANES), U32),
                        pltpu.VMEM((TOP_K, m, LANES), U32),
                        pltpu.SemaphoreType.DMA((ROW_BUFFERS,))],
        compiler_params=_params(1),
        name="combine_final",
    )(dest, dest, dest, wts, base, p, ys, w["pg"], w["pb"], w["pp"], w["g2"], w["b2"])


def _prep_layer(prm):
    (w_in, conv_w, conv_b, lru_wa, lru_ba, lru_wx, lru_bx, lru_lambda, ret_gn, w_out, ln1_g, ln1_b,
     router_w, router_b, e_gate, e_up, e_down, s_gate, s_up, s_down, ln2_g, ln2_b,
     ple_w_proj, ple_w_gate, ple_b_gate) = prm
    row = lambda v: v.reshape(1, -1)
    rw2 = jnp.concatenate([router_w, router_w], axis=1).astype(BF16)
    return dict(
        w_in=w_in.astype(BF16), conv_w=conv_w, conv_b=row(conv_b), wa=lru_wa.astype(BF16), ba=row(lru_ba),
        wx=lru_wx.astype(BF16), bx=row(lru_bx), lam=row(lru_lambda), gn=row(ret_gn), wo=w_out.astype(BF16),
        g1=row(ln1_g), b1=row(ln1_b), rw=rw2, rb=row(jnp.concatenate([router_b, router_b])),
        e_gate=e_gate, e_up=e_up, e_down=e_down, sg=s_gate.astype(BF16), su=s_up.astype(BF16),
        sd=s_down.astype(BF16), g2=row(ln2_g), b2=row(ln2_b), pp=ple_w_proj.astype(BF16),
        pg=ple_w_gate.astype(BF16), pb=row(ple_b_gate))


def _mixers(x, conv_st, lru_st, ret_st, pos0, chunk, w):
    B, T, D = x.shape
    W = conv_st.shape[-1]
    z_lru, z_ret = _inproj(x, w["w_in"], 2 * W)
    y_ret, new_ret = _retention(z_ret, ret_st, w["gn"], pos0, chunk)
    y_lru, conv_tm, new_lru = _lru(z_lru, jnp.transpose(conv_st, (1, 0, 2)), lru_st,
                                   w["conv_w"], w["conv_b"], w["wa"], w["ba"], w["wx"], w["bx"], w["lam"],
                                   after=new_ret)
    return y_lru.reshape(T, B * W), y_ret, jnp.transpose(conv_tm, (1, 0, 2)), new_lru, new_ret


def kernel(x_prompt, x_sample, p_prompt, p_sample, state_conv, state_lru, state_ret, w_in, conv_w, conv_b,
           lru_wa, lru_ba, lru_wx, lru_bx, lru_lambda, ret_gn, w_out, ln1_g, ln1_b, router_w, router_b,
           exp_w_gate, exp_w_up, exp_w_down, sh_w_gate, sh_w_up, sh_w_down, ln2_g, ln2_b,
           ple_w_proj, ple_w_gate, ple_b_gate):
    depth = w_in.shape[0]
    alpha = (2 * depth) ** 0.25
    b_p, t_p, _ = x_prompt.shape
    b_s, t_s, _ = x_sample.shape
    W = state_conv.shape[-1]
    H, dh = state_ret.shape[2], state_ret.shape[3]
    n_tokens = b_p * t_p + b_s * t_s
    cap = -(-n_tokens // GMM_ROWS) * GMM_ROWS
    hp, hs = x_prompt, x_sample
    outs = [[] for _ in range(6)]
    for i in range(depth):
        prm = (w_in[i], conv_w[i], conv_b[i], lru_wa[i], lru_ba[i], lru_wx[i], lru_bx[i], lru_lambda[i],
               ret_gn[i], w_out[i], ln1_g[i], ln1_b[i], router_w[i], router_b[i], exp_w_gate[i], exp_w_up[i],
               exp_w_down[i], sh_w_gate[i], sh_w_up[i], sh_w_down[i], ln2_g[i], ln2_b[i],
               ple_w_proj[i], ple_w_gate[i], ple_b_gate[i])
        w = _prep_layer(prm)
        zc = jnp.zeros((b_p, CONV_WIDTH - 1, W), x_prompt.dtype)
        zl = jnp.zeros((b_p, W), F32)
        zr = jnp.zeros((b_p, H, dh, dh), F32)
        yl_p, yr_p, c_p, l_p, r_p = _mixers(hp, zc, zl, zr, 0, CHUNK, w)
        yl_s, yr_s, c_s, l_s, r_s = _mixers(hs, state_conv[i], state_lru[i], state_ret[i], PAST_LEN, t_s, w)
        for o, val in zip(outs, (c_p, l_p, r_p, c_s, l_s, r_s)):
            o.append(val)
        base_p, dest_p, wts_p, cnt_p, xs = _mix(yl_p, yr_p, hp, w, alpha, jnp.zeros((1, LANES), I32), None, cap)
        base_s, dest_s, wts_s, cnt_all, xs = _mix(yl_s, yr_s, hs, w, alpha, cnt_p, xs, cap)
        ys = _gmm(xs, cnt_all[0, :N_EXPERTS], cap, n_tokens, w["e_gate"], w["e_up"], w["e_down"])
        hp = _final(base_p, dest_p, wts_p, p_prompt[i], ys, w, b_p, t_p)
        hs = _final(base_s, dest_s, wts_s, p_sample[i], ys, w, b_s, t_s)
    return (hp, hs) + tuple(jnp.stack(o) for o in outs)
```

```python
import functools

import jax
import jax.numpy as jnp
from jax import lax
from jax.experimental import pallas as pl
from jax.experimental.pallas import tpu as pltpu

F32 = jnp.float32
BF16 = jnp.bfloat16
U32 = jnp.uint32
I32 = jnp.int32

CHUNK = 64
PAST_LEN = 1024
CONV_WIDTH = 4
LRU_C = 8.0
RET_HEADS = 8
ROPE_BASE = 10000.0
N_EXPERTS = 64
TOP_K = 8
N_GROUPS = 8
TOPK_GROUPS = 4
ROUTED_SCALE = 2.5
LN_EPS = 1e-5
GN_EPS = 1e-6

LANES = 128
ROW_SUBLANES = 8
DRAIN_UNROLL = 8
ISSUE_CHUNKS = 4
ROW_BUFFERS = 3
ROWS_PER_STEP = 512
INPROJ_COLS = 512
MIX_ROWS = 256
GMM_ROWS = 512
RET_GROUP_ROWS = 256
LRU_TIME_TILE = 64
LRU_GATE_ROWS = 128
RET_ALL_HEADS_ELEMS = 64 * 1024
VMEM_LIMIT = 56 * 1024 * 1024


def _const_spec(shape):
    zeros = (0,) * len(shape)
    return pl.BlockSpec(shape, lambda *_: zeros, pipeline_mode=pl.Buffered(1))


def _params(n_axes):
    return pltpu.CompilerParams(dimension_semantics=("arbitrary",) * n_axes,
                                vmem_limit_bytes=VMEM_LIMIT)


def _layer_norm(x, g, b):
    mu = jnp.mean(x, axis=-1, keepdims=True)
    xc = x - mu
    var = jnp.mean(xc * xc, axis=-1, keepdims=True)
    return xc * lax.rsqrt(var + LN_EPS) * g + b


def _pack_bf16_pair(x):
    c = x.shape[1] // 2
    xb = x.astype(BF16).astype(F32)
    hi = pltpu.bitcast(xb[:, :c], U32)
    lo = pltpu.bitcast(xb[:, c:], U32)
    return hi | (lo >> 16)


def _unpack_bf16_pair(pk):
    hi = pltpu.bitcast(pk & jnp.uint32(0xFFFF0000), F32)
    lo = pltpu.bitcast(pk << 16, F32)
    return hi, lo


def _inproj_body(x_ref, w_ref, zl_ref, zr_ref, *, bb, tm, lru_cols, tn):
    d = x_ref.shape[-1]
    x = x_ref[...].reshape(bb * tm, d).astype(BF16)
    for j in range(w_ref.shape[1] // tn):
        c0 = j * tn
        acc = jnp.dot(x, w_ref[:, c0:c0 + tn], preferred_element_type=F32)
        for b in range(bb):
            rows = acc[b * tm:(b + 1) * tm]
            if c0 < lru_cols:
                zl_ref[:, b * lru_cols + c0:b * lru_cols + c0 + tn] = rows
            else:
                zr_ref[b, :, c0 - lru_cols:c0 - lru_cols + tn] = rows.astype(BF16)


def _inproj_cols_body(x_ref, w_ref, zl_ref, zr_ref, xb_s, *, n_lru):
    j = pl.program_id(0)
    bb, tm, d = x_ref.shape

    @pl.when(j == 0)
    def _():
        xb_s[...] = x_ref[...].reshape(bb * tm, d).astype(BF16)

    acc = jnp.dot(xb_s[...], w_ref[...], preferred_element_type=F32)

    @pl.when(j < n_lru)
    def _():
        for b in range(bb):
            zl_ref[:, b, :] = acc[b * tm:(b + 1) * tm]

    @pl.when(j >= n_lru)
    def _():
        zr_ref[...] = acc.reshape(zr_ref.shape).astype(BF16)


def _inproj_cols(x, w_in_b, lru_cols):
    B, T, D = x.shape
    n_cols = w_in_b.shape[1]
    tn = INPROJ_COLS
    n_lru = lru_cols // tn
    body = functools.partial(_inproj_cols_body, n_lru=n_lru)
    return pl.pallas_call(
        body,
        grid=(n_cols // tn,),
        in_specs=[_const_spec((B, T, D)),
                  pl.BlockSpec((D, tn), lambda j: (0, j))],
        out_specs=[pl.BlockSpec((T, B, tn), lambda j: (0, 0, jnp.minimum(j, n_lru - 1))),
                   pl.BlockSpec((B, T, tn), lambda j: (0, 0, jnp.maximum(j - n_lru, 0)))],
        out_shape=[jax.ShapeDtypeStruct((T, B, lru_cols), F32),
                   jax.ShapeDtypeStruct((B, T, n_cols - lru_cols), BF16)],
        scratch_shapes=[pltpu.VMEM((B * T, D), BF16)],
        compiler_params=_params(1),
        name="inproj",
    )(x, w_in_b)


def _inproj(x, w_in_b, lru_cols):
    B, T, D = x.shape
    if B * T <= ROWS_PER_STEP:
        return _inproj_cols(x, w_in_b, lru_cols)
    z_lru, z_ret = _inproj_rows(x, w_in_b, lru_cols)
    return z_lru.reshape(T, B, lru_cols), z_ret


def _inproj_rows(x, w_in_b, lru_cols):
    B, T, D = x.shape
    n_cols = w_in_b.shape[1]
    ret_cols = n_cols - lru_cols
    tm = min(ROWS_PER_STEP, T)
    bb = min(B, ROWS_PER_STEP // tm)
    body = functools.partial(_inproj_body, bb=bb, tm=tm, lru_cols=lru_cols, tn=INPROJ_COLS)
    return pl.pallas_call(
        body,
        grid=(B // bb, T // tm),
        in_specs=[pl.BlockSpec((bb, tm, D), lambda b, t: (b, t, 0)),
                  _const_spec((D, n_cols))],
        out_specs=[pl.BlockSpec((tm, bb * lru_cols), lambda b, t: (t, b)),
                   pl.BlockSpec((bb, tm, ret_cols), lambda b, t: (b, t, 0))],
        out_shape=[jax.ShapeDtypeStruct((T, B * lru_cols), F32),
                   jax.ShapeDtypeStruct((B, T, ret_cols), BF16)],
        compiler_params=_params(2),
        name="inproj",
    )(x, w_in_b)


def _lru_body(xl_ref, conv0_ref, h0_ref, cw_ref, cb_ref, wa_ref, ba_ref, wx_ref, bx_ref, lam_ref,
              after_ref, y_ref, conv_out_ref, h_out_ref, xp_s, a_s, b_s, h_s, *, tt, rows):
    del after_ref
    i = pl.program_id(0)
    B, W = h0_ref.shape
    nblk = wa_ref.shape[0]
    blk = W // nblk

    @pl.when(i == 0)
    def _():
        xp_s[0:CONV_WIDTH - 1] = conv0_ref[...]
        h_s[...] = h0_ref[...]

    xp_s[CONV_WIDTH - 1:] = xl_ref[...]

    lam = lam_ref[...]
    neg = -lam
    softplus = jnp.maximum(neg, 0.0) + jnp.log1p(jnp.exp(-jnp.abs(neg)))
    decay = (-LRU_C) * softplus

    def gates(c, carry):
        t0 = pl.multiple_of(c * rows, rows)
        xc = cb_ref[...].reshape(1, 1, W)
        for j in range(CONV_WIDTH):
            xc = xc + xp_s[pl.ds(t0 + j, rows)] * cw_ref[j:j + 1].reshape(1, 1, W)
        xc2 = xc.reshape(rows * B, W)
        xcb = xc2.astype(BF16)
        r_parts, i_parts = [], []
        for n in range(nblk):
            xb = xcb[:, n * blk:(n + 1) * blk]
            r_parts.append(jnp.dot(xb, wa_ref[n], preferred_element_type=F32))
            i_parts.append(jnp.dot(xb, wx_ref[n], preferred_element_type=F32))
        r = jax.nn.sigmoid(jnp.concatenate(r_parts, axis=1) + ba_ref[...])
        ig = jax.nn.sigmoid(jnp.concatenate(i_parts, axis=1) + bx_ref[...])
        a = jnp.exp(decay * r)
        bterm = jnp.sqrt(1.0 - a * a) * (ig * xc2)
        a_s[pl.ds(t0, rows)] = a.reshape(rows, B, W)
        b_s[pl.ds(t0, rows)] = bterm.reshape(rows, B, W)
        return carry

    lax.fori_loop(0, tt // rows, gates, 0)

    def step(t, h):
        hn = a_s[t] * h + b_s[t]
        y_ref[t] = hn
        return hn

    h_last = lax.fori_loop(0, tt, step, h_s[...], unroll=8)
    h_s[...] = h_last
    tail = xp_s[tt:tt + CONV_WIDTH - 1]
    xp_s[0:CONV_WIDTH - 1] = tail
    conv_out_ref[...] = tail
    h_out_ref[...] = h_last


def _lru(z_lru3, conv0_tm, h0, conv_w, conv_b, wa_b, ba, wx_b, bx, lam, after):
    T, B, W = z_lru3.shape
    tt = min(LRU_TIME_TILE, T)
    rows = max(1, min(tt, LRU_GATE_ROWS // B))
    body = functools.partial(_lru_body, tt=tt, rows=rows)
    nb = wa_b.shape[0]
    blk = W // nb
    return pl.pallas_call(
        body,
        grid=(T // tt,),
        in_specs=[pl.BlockSpec((tt, B, W), lambda t: (t, 0, 0)),
                  _const_spec((CONV_WIDTH - 1, B, W)),
                  _const_spec((B, W)),
                  _const_spec((CONV_WIDTH, W)),
                  _const_spec((1, W)),
                  _const_spec((nb, blk, blk)),
                  _const_spec((1, W)),
                  _const_spec((nb, blk, blk)),
                  _const_spec((1, W)),
                  _const_spec((1, W)),
                  pl.BlockSpec(memory_space=pl.ANY)],
        out_specs=[pl.BlockSpec((tt, B, W), lambda t: (t, 0, 0)),
                   pl.BlockSpec((CONV_WIDTH - 1, B, W), lambda t: (0, 0, 0)),
                   pl.BlockSpec((B, W), lambda t: (0, 0))],
        out_shape=[jax.ShapeDtypeStruct((T, B, W), F32),
                   jax.ShapeDtypeStruct((CONV_WIDTH - 1, B, W), F32),
                   jax.ShapeDtypeStruct((B, W), F32)],
        scratch_shapes=[pltpu.VMEM((tt + CONV_WIDTH - 1, B, W), F32),
                        pltpu.VMEM((tt, B, W), F32),
                        pltpu.VMEM((tt, B, W), F32),
                        pltpu.VMEM((B, W), F32)],
        compiler_params=_params(1),
        name="rglru",
    )(z_lru3, conv0_tm, h0, conv_w, conv_b, wa_b, ba, wx_b, bx, lam, after)


def _ret_body(q_ref, k_ref, v_ref, cos_ref, sin_ref, mask_ref, qdec_ref, kdec_ref, cdec_ref, gn_ref,
              s0_ref, y_ref, s_out_ref, *, rg, hb, dh):
    T = q_ref.shape[1]
    scale = dh ** -0.5

    def rope(t, cos, sin):
        return t * cos + pltpu.roll(t, dh // 2, axis=1) * sin

    for hh in range(hb):
        cols = slice(hh * dh, (hh + 1) * dh)

        def group(c, s, hh=hh, cols=cols):
            r0 = pl.multiple_of(c * rg, rg)
            rws = pl.ds(r0, rg)
            cos = cos_ref[rws, :]
            sin = sin_ref[rws, :]
            q = rope(q_ref[0, rws, cols].astype(F32), cos, sin)
            k = rope(k_ref[0, rws, cols].astype(F32), cos, sin) * scale
            v = v_ref[0, rws, cols]
            scores = lax.dot_general(q.astype(BF16), k.astype(BF16), (((1,), (1,)), ((), ())),
                                     preferred_element_type=F32)
            scores = scores * mask_ref[hh]
            o = jnp.dot(scores.astype(BF16), v, preferred_element_type=F32)
            o = o + jnp.dot((q * qdec_ref[hh]).astype(BF16), s.astype(BF16), preferred_element_type=F32)
            kd = (k * kdec_ref[hh]).astype(BF16)
            kv = lax.dot_general(kd, v, (((0,), (0,)), ((), ())), preferred_element_type=F32)
            s_new = cdec_ref[hh] * s + kv
            mu = jnp.mean(o, axis=-1, keepdims=True)
            oc = o - mu
            var = jnp.mean(oc * oc, axis=-1, keepdims=True)
            on = oc * lax.rsqrt(var + GN_EPS) * gn_ref[:, cols]
            y_ref[0, rws, cols] = on.astype(y_ref.dtype)
            return s_new

        n_groups = T // rg
        s_out_ref[0, hh] = lax.fori_loop(0, n_groups, group, s0_ref[0, hh],
                                         unroll=4 if n_groups % 4 == 0 else 1)


def _retention_tables(T, pos0, chunk, rg, dh):
    half = dh // 2
    inv = ROPE_BASE ** (-jnp.arange(half, dtype=F32) / half)
    pos = pos0 + jnp.arange(T)
    ang = pos.astype(F32)[:, None] * inv[None, :]
    cos, sin = jnp.cos(ang), jnp.sin(ang)
    cos2 = jnp.concatenate([cos, cos], axis=1)
    sin2 = jnp.concatenate([-sin, sin], axis=1)
    log_g = jnp.log1p(-jnp.exp2(-5.0 - jnp.arange(RET_HEADS, dtype=F32)))[:, None, None]
    idx = jnp.arange(rg, dtype=F32)
    ci = jnp.floor(idx / chunk)
    diff = idx[:, None] - idx[None, :]
    same = ci[:, None] == ci[None, :]
    earlier = ci[None, :] < ci[:, None]
    dist = jnp.where(same, jnp.abs(diff), diff)
    mask = jnp.where(same | earlier, jnp.exp(dist[None] * log_g), 0.0)
    ones = jnp.ones((1, 1, dh), F32)
    qdec = jnp.exp((idx + 1.0)[None, :, None] * log_g) * ones
    kdec = jnp.exp((rg - 1.0 - idx)[None, :, None] * log_g) * ones
    cdec = jnp.exp(rg * log_g) * ones
    return cos2, sin2, mask, qdec, kdec, cdec


def _retention(z_ret, s0, gn, pos0, chunk):
    B, T, C3 = z_ret.shape
    H = RET_HEADS
    dh = C3 // (3 * H)
    rg = min(T, max(chunk, (RET_GROUP_ROWS // chunk) * chunk))
    hb = H if T * H * dh <= RET_ALL_HEADS_ELEMS else 1
    nh = H // hb
    cos2, sin2, mask, qdec, kdec, cdec = _retention_tables(T, pos0, chunk, rg, dh)
    body = functools.partial(_ret_body, rg=rg, hb=hb, dh=dh)
    col = lambda off: (lambda b, h: (b, 0, off * nh + h))
    return pl.pallas_call(
        body,
        grid=(B, nh),
        in_specs=[pl.BlockSpec((1, T, hb * dh), col(0)),
                  pl.BlockSpec((1, T, hb * dh), col(1)),
                  pl.BlockSpec((1, T, hb * dh), col(2)),
                  _const_spec((T, dh)),
                  _const_spec((T, dh)),
                  pl.BlockSpec((hb, rg, rg), lambda b, h: (h, 0, 0)),
                  pl.BlockSpec((hb, rg, dh), lambda b, h: (h, 0, 0)),
                  pl.BlockSpec((hb, rg, dh), lambda b, h: (h, 0, 0)),
                  pl.BlockSpec((hb, 1, dh), lambda b, h: (h, 0, 0)),
                  pl.BlockSpec((1, hb * dh), lambda b, h: (0, h)),
                  pl.BlockSpec((1, hb, dh, dh), lambda b, h: (b, h, 0, 0))],
        out_specs=[pl.BlockSpec((1, T, hb * dh), lambda b, h: (b, 0, h)),
                   pl.BlockSpec((1, hb, dh, dh), lambda b, h: (b, h, 0, 0))],
        out_shape=[jax.ShapeDtypeStruct((B, T, H * dh), BF16),
                   jax.ShapeDtypeStruct((B, H, dh, dh), F32)],
        compiler_params=_params(2),
        name="retention",
    )(z_ret, z_ret, z_ret, cos2, sin2, mask, qdec, kdec, cdec, gn, s0)


def _seg_allreduce(v, lane, op):
    for s in (1, 2, 4):
        up = pltpu.roll(v, LANES - s, axis=1)
        dn = pltpu.roll(v, s, axis=1)
        v = op(v, jnp.where((lane & s) == 0, up, dn))
    return v


def _mix_body(*refs, bb, tm, alpha, cap, n_steps, aliased):
    (yl_ref, yr_ref, x_ref, wgt_ref, wo_ref, g1_ref, b1_ref, rw_ref, rb_ref, sg_ref, su_ref, sd_ref,
     tri_ref, cnt_in_ref) = refs[:14]
    refs = refs[15:] if aliased else refs[14:]
    base_ref, dest_ref, wts_ref, cnt_ref, xs_ref, carry_s, xpk_s, dv_s, ds_s, row_sems, idx_sem = refs
    i = pl.program_id(0)
    m = bb * tm
    d = x_ref.shape[-1]
    w = yl_ref.shape[1] // bb
    slot = lax.rem(i, ROW_BUFFERS)
    prev = lax.rem(i + 2, ROW_BUFFERS)
    prev2 = lax.rem(i + 1, ROW_BUFFERS)
    chunk = m // ISSUE_CHUNKS

    def row_copy(sl, n, dst):
        return pltpu.make_async_copy(xpk_s.at[sl, n], xs_ref.at[dst], row_sems.at[sl])

    def issue_rows(sl, lo, hi):
        def one(n, c):
            for kk in range(TOP_K):
                row_copy(sl, n, ds_s[sl, kk, n]).start(priority=kk % 2)
            return c
        lax.fori_loop(lo, hi, one, 0)

    def issue_prev_chunk(c):
        for n in range(c * chunk, (c + 1) * chunk):
            for kk in range(TOP_K):
                row_copy(prev, n, ds_s[prev, kk, n]).start(priority=kk % 2)

    def drain(sl):
        def one(n, c):
            for _ in range(DRAIN_UNROLL * TOP_K):
                row_copy(sl, 0, 0).wait()
            return c
        lax.fori_loop(0, m // DRAIN_UNROLL, one, 0)

    @pl.when(i == 0)
    def _():
        carry_s[...] = jnp.zeros_like(carry_s)
        carry_s[0:1, :] = cnt_in_ref[...].astype(F32)
        xpk_s[ROW_BUFFERS - 1] = jnp.zeros(xpk_s.shape[1:], U32)
        spare = (N_EXPERTS * cap + lax.broadcasted_iota(I32, (TOP_K, m), 0) * m
                 + lax.broadcasted_iota(I32, (TOP_K, m), 1))
        dv_s[...] = spare
        first = pltpu.make_async_copy(dv_s, ds_s.at[ROW_BUFFERS - 1], idx_sem)
        first.start()
        first.wait()

    issue_prev_chunk(0)
    x = x_ref[...].reshape(m, d)
    gates = jnp.dot(x.astype(BF16), wgt_ref[...], preferred_element_type=F32)
    h_lru = jnp.concatenate([yl_ref[:, b * w:(b + 1) * w] for b in range(bb)], axis=0)
    yl = (h_lru * jax.nn.gelu(gates[:, :w])).astype(BF16)
    g_ret = gates[:, w:]
    yr = (g_ret * jax.nn.sigmoid(g_ret) * yr_ref[...].reshape(m, yr_ref.shape[-1]).astype(F32)).astype(BF16)
    issue_prev_chunk(1)
    mix = jnp.dot(jnp.concatenate([yl, yr], axis=1), wo_ref[...], preferred_element_type=F32)
    x1 = _layer_norm(alpha * x + mix, g1_ref[...], b1_ref[...])
    x1b = x1.astype(BF16)
    issue_prev_chunk(2)

    logits = jnp.dot(x1b, rw_ref[...], preferred_element_type=F32)

    s = jax.nn.sigmoid(logits)
    sb = s + rb_ref[...]
    lane = lax.broadcasted_iota(I32, (m, LANES), 1)
    e_id = lane & (N_EXPERTS - 1)
    e_f = e_id.astype(F32)
    low = lane < N_EXPERTS
    e_low = jnp.where(low, e_f, -1.0)
    grp = e_id >> 3
    big = jnp.float32(1e9)
    ninf = jnp.float32(-jnp.inf)

    m1 = _seg_allreduce(sb, lane, jnp.maximum)
    first_max = _seg_allreduce(jnp.where(sb == m1, e_f, big), lane, jnp.minimum)
    m2 = _seg_allreduce(jnp.where(e_f == first_max, ninf, sb), lane, jnp.maximum)
    gs = m1 + m2
    hg = jnp.dot(x1b, sg_ref[...], preferred_element_type=F32)
    issue_prev_chunk(3)
    rank = jnp.zeros((m, LANES), F32)
    for dgrp in range(1, N_GROUPS):
        other = pltpu.roll(gs, 8 * dgrp, axis=1)
        tie = jnp.where(grp >= dgrp, 1.0, 0.0)
        rank = rank + jnp.where(other > gs, 1.0, jnp.where(other == gs, tie, 0.0))
    v = jnp.where(rank < TOPK_GROUPS, jnp.where(low, sb, ninf), ninf)
    hu = jnp.dot(x1b, su_ref[...], preferred_element_type=F32)
    hs = (hg * jax.nn.sigmoid(hg) * hu).astype(BF16)

    idx_cols, w_cols = [], []
    sel = jnp.zeros((m, LANES), F32)
    for rnd in range(TOP_K):
        mx = jnp.max(v, axis=1, keepdims=True)
        idx = jnp.min(jnp.where(v == mx, e_f, big), axis=1, keepdims=True)
        hit = e_low == idx
        w_cols.append(jnp.sum(jnp.where(hit, s, 0.0), axis=1, keepdims=True))
        idx_cols.append(idx)
        v = jnp.where(hit, ninf, v)
        sel = jnp.where(hit, 1.0, sel)
        if rnd == TOP_K // 2 - 1:
            base_ref[...] = alpha * x1 + jnp.dot(hs, sd_ref[...], preferred_element_type=F32)

    packed = _pack_bf16_pair(x1)
    for sub in range(ROW_SUBLANES):
        xpk_s[slot, :, sub, :] = packed[:, sub * LANES:(sub + 1) * LANES]

    cum = jnp.dot(tri_ref[...], sel.astype(BF16), preferred_element_type=F32) + carry_s[0:1, :]
    carry_s[0:1, :] = carry_s[0:1, :] + jnp.sum(sel, axis=0, keepdims=True)
    cnt_ref[...] = carry_s[0:1, :].astype(I32)

    wsum = w_cols[0]
    for c in w_cols[1:]:
        wsum = wsum + c
    d_out = jnp.zeros((m, LANES), F32)
    w_out = jnp.zeros((m, LANES), F32)
    for kk in range(TOP_K):
        hit = e_low == idx_cols[kk]
        pk = jnp.sum(jnp.where(hit, cum, 0.0), axis=1, keepdims=True)
        d_out = jnp.where(lane == kk, idx_cols[kk] * float(cap) + pk, d_out)
        w_out = jnp.where(lane == kk, w_cols[kk] / wsum * ROUTED_SCALE, w_out)
    wts_ref[...] = w_out[:, :TOP_K]
    dest_t = jnp.transpose(d_out)[:TOP_K].astype(I32)
    dest_ref[0] = dest_t
    dv_s[...] = dest_t
    to_smem = pltpu.make_async_copy(dv_s, ds_s.at[slot], idx_sem)
    to_smem.start()
    to_smem.wait()

    @pl.when(i > 0)
    def _():
        drain(prev2)

    @pl.when(i == n_steps - 1)
    def _():
        drain(prev)
        issue_rows(slot, 0, m)
        drain(slot)


def _mix(y_lru2, y_ret, x, w, alpha, cnt_in, xs, cap):
    B, T, D = x.shape
    assert D == 2 * ROW_SUBLANES * LANES, "a packed token row must fill exactly one (8, 128) tile"
    W = y_ret.shape[-1]
    tm = min(MIX_ROWS, T)
    bb = min(B, MIX_ROWS // tm)
    m = bb * tm
    n = B * T
    nt = T // tm
    n_steps = (B // bb) * nt
    hs = w["sg"].shape[1]
    aliased = xs is not None
    tri = (lax.broadcasted_iota(I32, (m, m), 1) < lax.broadcasted_iota(I32, (m, m), 0)).astype(BF16)
    body = functools.partial(_mix_body, bb=bb, tm=tm, alpha=alpha, cap=cap, n_steps=n_steps, aliased=aliased)
    in_specs = [pl.BlockSpec((tm, bb * W), lambda i: (i % nt, i // nt)),
                pl.BlockSpec((bb, tm, W), lambda i: (i // nt, i % nt, 0)),
                pl.BlockSpec((bb, tm, D), lambda i: (i // nt, i % nt, 0)),
                _const_spec((D, 2 * W)),
                _const_spec((2 * W, D)),
                _const_spec((1, D)),
                _const_spec((1, D)),
                _const_spec((D, LANES)),
                _const_spec((1, LANES)),
                _const_spec((D, hs)),
                _const_spec((D, hs)),
                _const_spec((hs, D)),
                _const_spec((m, m)),
                _const_spec((1, LANES))]
    args = [y_lru2, y_ret, x, w["w_gates"], w["wo"], w["g1"], w["b1"], w["rw"], w["rb"], w["sg"], w["su"],
            w["sd"], tri, cnt_in]
    if aliased:
        in_specs.append(pl.BlockSpec(memory_space=pl.ANY))
        args.append(xs)
    return pl.pallas_call(
        body,
        grid=(n_steps,),
        in_specs=in_specs,
        out_specs=[pl.BlockSpec((m, D), lambda i: (i, 0)),
                   pl.BlockSpec((1, TOP_K, m), lambda i: (i, 0, 0)),
                   pl.BlockSpec((m, TOP_K), lambda i: (i, 0)),
                   pl.BlockSpec((1, LANES), lambda i: (0, 0)),
                   pl.BlockSpec(memory_space=pl.ANY)],
        out_shape=[jax.ShapeDtypeStruct((n, D), F32),
                   jax.ShapeDtypeStruct((n_steps, TOP_K, m), I32),
                   jax.ShapeDtypeStruct((n, TOP_K), F32),
                   jax.ShapeDtypeStruct((1, LANES), I32),
                   jax.ShapeDtypeStruct((N_EXPERTS * cap + MIX_ROWS * TOP_K, ROW_SUBLANES, LANES), U32)],
        scratch_shapes=[pltpu.VMEM((8, LANES), F32),
                        pltpu.VMEM((ROW_BUFFERS, m, ROW_SUBLANES, LANES), U32),
                        pltpu.VMEM((TOP_K, m), I32),
                        pltpu.SMEM((ROW_BUFFERS, TOP_K, m), I32),
                        pltpu.SemaphoreType.DMA((ROW_BUFFERS,)),
                        pltpu.SemaphoreType.DMA(())],
        input_output_aliases={14: 4} if aliased else {},
        compiler_params=_params(1),
        name="mix_router",
    )(*args)


def _gmm_body(ge_ref, gr_ref, gn_ref, gt_ref, gx_ref, gs_ref, xs_ref, wg_ref, wu_ref, wd_ref, ys_ref,
              wgf_s, wuf_s, wdf_s, wgu_s, wd_s, x_s, y_s, w_sems, in_sems, out_sems, *, tm, n_items):
    i = pl.program_id(0)
    hid = wg_ref.shape[2]
    slot = i % 2

    def weight_copies(expert, sl):
        return [pltpu.make_async_copy(wg_ref.at[expert], wgf_s.at[sl], w_sems.at[sl]),
                pltpu.make_async_copy(wu_ref.at[expert], wuf_s.at[sl], w_sems.at[sl]),
                pltpu.make_async_copy(wd_ref.at[expert], wdf_s.at[sl], w_sems.at[sl])]

    def tile_copies(to_vmem, item, sl):
        r0 = pl.multiple_of(gr_ref[item] * tm, tm)
        out = []
        for sub in range(ROW_SUBLANES):
            cols = pl.ds(sub * LANES, LANES)
            if to_vmem:
                out.append(pltpu.make_async_copy(xs_ref.at[pl.ds(r0, tm), sub, :], x_s.at[sl, :, cols],
                                                 in_sems.at[sl]))
            else:
                out.append(pltpu.make_async_copy(y_s.at[sl, :, cols], ys_ref.at[pl.ds(r0, tm), sub, :],
                                                 out_sems.at[sl]))
        return out

    @pl.when(i == 0)
    def _():
        for cp in weight_copies(ge_ref[0], 0):
            cp.start()
        for cp in tile_copies(True, 0, 0):
            cp.start()
        y_s[...] = jnp.zeros_like(y_s)

    nxt = jnp.minimum(i + 1, n_items - 1)

    @pl.when((i + 1 < n_items) & (gn_ref[nxt] > 0))
    def _():
        for cp in tile_copies(True, nxt, 1 - slot):
            cp.start()

    e = ge_ref[i]
    e_prev = ge_ref[jnp.maximum(i - 1, 0)]

    @pl.when((i == 0) | (e != e_prev))
    def _():
        wsl = gs_ref[i]
        nxt_e = gx_ref[i]

        @pl.when(nxt_e >= 0)
        def _():
            for cp in weight_copies(nxt_e, 1 - wsl):
                cp.start()

        for cp in weight_copies(e, wsl):
            cp.wait()
        wgu_s[:, :hid] = wgf_s[wsl].astype(BF16)
        wgu_s[:, hid:] = wuf_s[wsl].astype(BF16)
        wd_s[...] = wdf_s[wsl].astype(BF16)

    n_valid = gn_ref[i]

    @pl.when(n_valid > 0)
    def _():
        for cp in tile_copies(True, i, slot):
            cp.wait()

        @pl.when(i >= 2)
        def _():
            for cp in tile_copies(False, i, slot):
                cp.wait()

        def expert_mlp(rows):
            pk = x_s[slot, :rows]
            valid = lax.broadcasted_iota(I32, pk.shape, 0) < n_valid
            xa, xb = _unpack_bf16_pair(jnp.where(valid, pk, jnp.uint32(0)))
            x = jnp.concatenate([xa.astype(BF16), xb.astype(BF16)], axis=1)
            h2 = jnp.dot(x, wgu_s[...], preferred_element_type=F32)
            hg = h2[:, :hid]
            h = (hg * jax.nn.sigmoid(hg) * h2[:, hid:]).astype(BF16)
            y_s[slot, :rows] = _pack_bf16_pair(jnp.dot(h, wd_s[...], preferred_element_type=F32))

        @pl.when(n_valid > tm // 2)
        def _():
            expert_mlp(tm)

        @pl.when((n_valid > tm // 4) & (n_valid <= tm // 2))
        def _():
            expert_mlp(tm // 2)

        @pl.when(n_valid <= tm // 4)
        def _():
            expert_mlp(tm // 4)

        for cp in tile_copies(False, i, slot):
            cp.start()

    @pl.when(i == n_items - 1)
    def _():
        total = gt_ref[0]

        @pl.when(total >= 2)
        def _():
            for cp in tile_copies(False, 0, total % 2):
                cp.wait()

        for cp in tile_copies(False, 0, (total + 1) % 2):
            cp.wait()


def _gmm_metadata(counts, cap, tm, n_items):
    e = counts.shape[0]
    tiles = (counts + tm - 1) // tm
    item_end = jnp.cumsum(tiles)
    total = item_end[-1]
    it = jnp.arange(n_items, dtype=I32)
    itc = jnp.minimum(it, total - 1)
    ge = jnp.sum((item_end[None, :] <= itc[:, None]).astype(I32), axis=1)
    onehot = ge[:, None] == jnp.arange(e, dtype=I32)[None, :]
    start = jnp.sum(jnp.where(onehot, (item_end - tiles)[None, :], 0), axis=1)
    cnt = jnp.sum(jnp.where(onehot, counts[None, :], 0), axis=1)
    j = itc - start
    gr = ge * (cap // tm) + j
    gn = jnp.where(it < total, jnp.clip(cnt - j * tm, 0, tm), 0)
    ids = jnp.arange(e, dtype=I32)
    live = tiles > 0
    later = live[None, :] & (ids[None, :] > ids[:, None])
    next_e = jnp.min(jnp.where(later, ids[None, :], e), axis=1)
    next_e = jnp.where(next_e < e, next_e, -1)
    wslot = (jnp.cumsum(live.astype(I32)) - 1) % 2
    gx = jnp.sum(jnp.where(onehot, next_e[None, :], 0), axis=1)
    gs = jnp.sum(jnp.where(onehot, wslot[None, :], 0), axis=1)
    return tuple(a.astype(I32) for a in (ge, gr, gn, total.reshape(1), gx, gs))


def _gmm(xs, counts, cap, n_tokens, wg, wu, wd):
    tm = GMM_ROWS
    e, d, hid = wg.shape
    n_items = (n_tokens * TOP_K) // tm + e
    meta = _gmm_metadata(counts, cap, tm, n_items)
    body = functools.partial(_gmm_body, tm=tm, n_items=n_items)
    grid_spec = pltpu.PrefetchScalarGridSpec(
        num_scalar_prefetch=6,
        grid=(n_items,),
        in_specs=[pl.BlockSpec(memory_space=pl.ANY)] * 4,
        out_specs=pl.BlockSpec(memory_space=pl.ANY),
        scratch_shapes=[pltpu.VMEM((2, d, hid), F32), pltpu.VMEM((2, d, hid), F32),
                        pltpu.VMEM((2, hid, d), F32),
                        pltpu.VMEM((d, 2 * hid), BF16), pltpu.VMEM((hid, d), BF16),
                        pltpu.VMEM((2, tm, d // 2), U32), pltpu.VMEM((2, tm, d // 2), U32),
                        pltpu.SemaphoreType.DMA((2,)), pltpu.SemaphoreType.DMA((2,)),
                        pltpu.SemaphoreType.DMA((2,))],
    )
    return pl.pallas_call(
        body,
        grid_spec=grid_spec,
        out_shape=jax.ShapeDtypeStruct(xs.shape, U32),
        compiler_params=_params(1),
        name="expert_gmm",
    )(*meta, xs, wg, wu, wd)


def _final_body(d0_ref, d1_ref, dn_ref, wts_ref, base_ref, p_ref, ys_ref, pg_ref, pb_ref, pp_ref, g2_ref, b2_ref,
                out_ref, rows_s, sum_s, wrep_s, sems, *, bb, tm, n_steps):
    i = pl.program_id(0)
    m = bb * tm
    d = base_ref.shape[1]
    slot = lax.rem(i, ROW_BUFFERS)
    ahead = lax.rem(i + 2, ROW_BUFFERS)

    def row_copy(sl, d_row, kk, n):
        return pltpu.make_async_copy(ys_ref.at[d_row], rows_s.at[sl, kk, n], sems.at[sl])

    def issue_token(sl, dref, n):
        for kk in range(TOP_K):
            row_copy(sl, dref[0, kk, n], kk, n).start(priority=kk % 2)

    def issue_tile(sl, dref):
        def one(n, c):
            issue_token(sl, dref, n)
            return c
        lax.fori_loop(0, m, one, 0)

    def issue_ahead(lo, hi):
        for n in range(lo, hi):
            issue_token(ahead, dn_ref, n)

    def drain(sl):
        def one(n, c):
            for _ in range(DRAIN_UNROLL * TOP_K):
                row_copy(sl, 0, 0, 0).wait()
            return c
        lax.fori_loop(0, m // DRAIN_UNROLL, one, 0)

    @pl.when(i == 0)
    def _():
        issue_tile(0, d0_ref)
        issue_tile(1, d1_ref)

    drain(slot)

    wts = wts_ref[...]
    for kk in range(TOP_K):
        wk = jnp.broadcast_to(wts[:, kk:kk + 1], (m, LANES))
        wrep_s[kk] = _pack_bf16_pair(jnp.concatenate([wk, wk], axis=1))

    def combine(n, c):
        acc = jnp.zeros((2 * ROW_SUBLANES, LANES), BF16)
        for kk in range(TOP_K):
            row = pltpu.bitcast(rows_s[slot, kk, n], BF16)
            wk = jnp.broadcast_to(wrep_s[kk, pl.ds(n, 1), :], (ROW_SUBLANES, LANES))
            acc = acc + pltpu.bitcast(wk, BF16) * row
        sum_s[n] = pltpu.bitcast(acc, U32)
        return c

    lax.fori_loop(0, m, combine, 0, unroll=2)
    issue_ahead(0, m // 2)
    routed = jnp.concatenate(_unpack_bf16_pair(
        jnp.concatenate([sum_s[:, sub, :] for sub in range(ROW_SUBLANES)], axis=1)), axis=1)
    x2 = _layer_norm(base_ref[...] + routed, g2_ref[...], b2_ref[...])
    issue_ahead(m // 2, m)
    gate = jax.nn.sigmoid(jnp.dot(x2.astype(BF16), pg_ref[...], preferred_element_type=F32) + pb_ref[...])
    proj = jnp.dot(p_ref[...].reshape(m, p_ref.shape[-1]).astype(BF16), pp_ref[...], preferred_element_type=F32)
    out_ref[...] = (x2 + gate * proj).reshape(bb, tm, d)

    @pl.when(i == n_steps - 1)
    def _():
        drain(lax.rem(i + 1, ROW_BUFFERS))
        drain(ahead)


def _final(base, dest, wts, p, ys, w, B, T):
    n, D = base.shape
    tm = min(MIX_ROWS, T)
    bb = min(B, MIX_ROWS // tm)
    m = bb * tm
    nt = T // tm
    n_steps = (B // bb) * nt
    pd = p.shape[-1]
    body = functools.partial(_final_body, bb=bb, tm=tm, n_steps=n_steps)
    return pl.pallas_call(
        body,
        grid=(n_steps,),
        in_specs=[pl.BlockSpec((1, TOP_K, m), lambda i: (0, 0, 0), memory_space=pltpu.SMEM),
                  pl.BlockSpec((1, TOP_K, m), lambda i: (min(1, n_steps - 1), 0, 0), memory_space=pltpu.SMEM),
                  pl.BlockSpec((1, TOP_K, m), lambda i: (jnp.minimum(i + 2, n_steps - 1), 0, 0),
                               memory_space=pltpu.SMEM),
                  pl.BlockSpec((m, TOP_K), lambda i: (i, 0)),
                  pl.BlockSpec((m, D), lambda i: (i, 0)),
                  pl.BlockSpec((bb, tm, pd), lambda i: (i // nt, i % nt, 0)),
                  pl.BlockSpec(memory_space=pl.ANY),
                  _const_spec((D, D)),
                  _const_spec((1, D)),
                  _const_spec((pd, D)),
                  _const_spec((1, D)),
                  _const_spec((1, D))],
        out_specs=pl.BlockSpec((bb, tm, D), lambda i: (i // nt, i % nt, 0)),
        out_shape=jax.ShapeDtypeStruct((B, T, D), F32),
        scratch_shapes=[pltpu.VMEM((ROW_BUFFERS, TOP_K, m, ROW_SUBLANES, LANES), U32),
                        pltpu.VMEM((m, ROW_SUBLANES, LANES), U32),
                        pltpu.VMEM((TOP_K, m, LANES), U32),
                        pltpu.SemaphoreType.DMA((ROW_BUFFERS,))],
        compiler_params=_params(1),
        name="combine_final",
    )(dest, dest, dest, wts, base, p, ys, w["pg"], w["pb"], w["pp"], w["g2"], w["b2"])


def _prep_layer(prm):
    (w_in, conv_w, conv_b, lru_wa, lru_ba, lru_wx, lru_bx, lru_lambda, ret_gn, w_out, ln1_g, ln1_b,
     router_w, router_b, e_gate, e_up, e_down, s_gate, s_up, s_down, ln2_g, ln2_b,
     ple_w_proj, ple_w_gate, ple_b_gate) = prm
    row = lambda v: v.reshape(1, -1)
    rw2 = jnp.concatenate([router_w, router_w], axis=1).astype(BF16)
    lw = conv_w.shape[-1]
    rw_ = (w_in.shape[1] - 2 * lw) // 4
    w_main = jnp.concatenate([w_in[:, :lw], w_in[:, 2 * lw:2 * lw + 3 * rw_]], axis=1).astype(BF16)
    w_gates = jnp.concatenate([w_in[:, lw:2 * lw], w_in[:, 2 * lw + 3 * rw_:]], axis=1).astype(BF16)
    return dict(
        w_in=w_main, w_gates=w_gates, conv_w=conv_w, conv_b=row(conv_b), wa=lru_wa.astype(BF16), ba=row(lru_ba),
        wx=lru_wx.astype(BF16), bx=row(lru_bx), lam=row(lru_lambda), gn=row(ret_gn), wo=w_out.astype(BF16),
        g1=row(ln1_g), b1=row(ln1_b), rw=rw2, rb=row(jnp.concatenate([router_b, router_b])),
        e_gate=e_gate, e_up=e_up, e_down=e_down, sg=s_gate.astype(BF16), su=s_up.astype(BF16),
        sd=s_down.astype(BF16), g2=row(ln2_g), b2=row(ln2_b), pp=ple_w_proj.astype(BF16),
        pg=ple_w_gate.astype(BF16), pb=row(ple_b_gate))


def _mixers(x, conv_st, lru_st, ret_st, pos0, chunk, w):
    B, T, D = x.shape
    W = conv_st.shape[-1]
    z_lru, z_ret = _inproj(x, w["w_in"], W)
    y_ret, new_ret = _retention(z_ret, ret_st, w["gn"], pos0, chunk)
    y_lru, conv_tm, new_lru = _lru(z_lru, jnp.transpose(conv_st, (1, 0, 2)), lru_st,
                                   w["conv_w"], w["conv_b"], w["wa"], w["ba"], w["wx"], w["bx"], w["lam"],
                                   after=new_ret)
    return y_lru.reshape(T, B * W), y_ret, jnp.transpose(conv_tm, (1, 0, 2)), new_lru, new_ret


def kernel(x_prompt, x_sample, p_prompt, p_sample, state_conv, state_lru, state_ret, w_in, conv_w, conv_b,
           lru_wa, lru_ba, lru_wx, lru_bx, lru_lambda, ret_gn, w_out, ln1_g, ln1_b, router_w, router_b,
           exp_w_gate, exp_w_up, exp_w_down, sh_w_gate, sh_w_up, sh_w_down, ln2_g, ln2_b,
           ple_w_proj, ple_w_gate, ple_b_gate):
    depth = w_in.shape[0]
    alpha = (2 * depth) ** 0.25
    b_p, t_p, _ = x_prompt.shape
    b_s, t_s, _ = x_sample.shape
    W = state_conv.shape[-1]
    H, dh = state_ret.shape[2], state_ret.shape[3]
    n_tokens = b_p * t_p + b_s * t_s
    cap = -(-n_tokens // GMM_ROWS) * GMM_ROWS
    hp, hs = x_prompt, x_sample
    outs = [[] for _ in range(6)]
    for i in range(depth):
        prm = (w_in[i], conv_w[i], conv_b[i], lru_wa[i], lru_ba[i], lru_wx[i], lru_bx[i], lru_lambda[i],
               ret_gn[i], w_out[i], ln1_g[i], ln1_b[i], router_w[i], router_b[i], exp_w_gate[i], exp_w_up[i],
               exp_w_down[i], sh_w_gate[i], sh_w_up[i], sh_w_down[i], ln2_g[i], ln2_b[i],
               ple_w_proj[i], ple_w_gate[i], ple_b_gate[i])
        w = _prep_layer(prm)
        zc = jnp.zeros((b_p, CONV_WIDTH - 1, W), x_prompt.dtype)
        zl = jnp.zeros((b_p, W), F32)
        zr = jnp.zeros((b_p, H, dh, dh), F32)
        yl_p, yr_p, c_p, l_p, r_p = _mixers(hp, zc, zl, zr, 0, CHUNK, w)
        yl_s, yr_s, c_s, l_s, r_s = _mixers(hs, state_conv[i], state_lru[i], state_ret[i], PAST_LEN, t_s, w)
        for o, val in zip(outs, (c_p, l_p, r_p, c_s, l_s, r_s)):
            o.append(val)
        base_p, dest_p, wts_p, cnt_p, xs = _mix(yl_p, yr_p, hp, w, alpha, jnp.zeros((1, LANES), I32), None, cap)
        base_s, dest_s, wts_s, cnt_all, xs = _mix(yl_s, yr_s, hs, w, alpha, cnt_p, xs, cap)
        ys = _gmm(xs, cnt_all[0, :N_EXPERTS], cap, n_tokens, w["e_gate"], w["e_up"], w["e_down"])
        hp = _final(base_p, dest_p, wts_p, p_prompt[i], ys, w, b_p, t_p)
        hs = _final(base_s, dest_s, wts_s, p_sample[i], ys, w, b_s, t_s)
    return (hp, hs) + tuple(jnp.stack(o) for o in outs)
```
